```python
import math
import jax, jax.numpy as jnp
from jax import lax
import numpy as np

D_MODEL = 4096
BATCH = 4
SEQ = 2048
DEPTH = 2
DEC_BATCH = 8
DEC_SEQ = 8
PAST_LEN = 16384
PAGE_SIZE = 128

N_A_LAYERS = (DEPTH + 1) // 2
N_C_LAYERS = DEPTH // 2
RMS_EPS = 1e-6

GLA_HEADS = 8
GLA_DK = 128
GLA_DV = 256
GLA_KW = GLA_HEADS * GLA_DK
GLA_VW = GLA_HEADS * GLA_DV
GLA_GATE_RANK = 16
GLA_GATE_TAU = 16.0
GLA_CHUNK = 64

MOBA_HEADS = 16
MOBA_DH = 128
MOBA_W = MOBA_HEADS * MOBA_DH
MOBA_BLOCK = 256
MOBA_TOPK = 3
MOBA_Q_CHUNK = 16
ROPE_THETA = 10000.0

A_IN = 2 * GLA_KW + 2 * GLA_VW + GLA_GATE_RANK + 4 * MOBA_W
A_MIX = GLA_VW + MOBA_W

S5_W = D_MODEL
S5_GROUP = 16
S5_GROUPS = S5_W // S5_GROUP
S5_P = 64
S5_DT_MIN = 1e-3
S5_DT_MAX = 1e-1

kernel_name = "gla_moba_s5_hybrid_decode_step"


def rms_norm(x, g):
    xf = x.astype(jnp.float32)
    y = xf * lax.rsqrt(jnp.mean(xf * xf, axis=-1, keepdims=True) + RMS_EPS)
    return (y * g.astype(jnp.float32)).astype(x.dtype)


def rope(x, pos):
    half = x.shape[-1] // 2
    inv_freq = ROPE_THETA ** (-jnp.arange(half, dtype=jnp.float32) / half)
    ang = pos.astype(jnp.float32)[:, None] * inv_freq[None, :]
    cos = jnp.cos(ang)[:, None, :]
    sin = jnp.sin(ang)[:, None, :]
    xf = x.astype(jnp.float32)
    x1, x2 = xf[..., :half], xf[..., half:]
    return jnp.concatenate([x1 * cos - x2 * sin, x2 * cos + x1 * sin], axis=-1).astype(x.dtype)


def gla_chunked(q, k, v, log_a, s0):
    B, T, H, _ = q.shape
    dv = v.shape[-1]
    c = GLA_CHUNK if T % GLA_CHUNK == 0 else T
    n = T // c

    def chunks(a):
        return a.reshape(B, n, c, H, a.shape[-1]).transpose(1, 0, 3, 2, 4)

    causal = jnp.tril(jnp.ones((c, c), dtype=bool))[:, :, None]

    def step(S, inp):
        qi, ki, vi, gi = inp
        qi = qi.astype(jnp.float32)
        ki = ki.astype(jnp.float32)
        vi = vi.astype(jnp.float32)
        b = jnp.cumsum(gi.astype(jnp.float32), axis=2)
        rel = jnp.exp(jnp.where(causal, b[:, :, :, None, :] - b[:, :, None, :, :], -jnp.inf))
        att = jnp.einsum('bhtd,bhsd,bhtsd->bhts', qi, ki, rel)
        o = (jnp.einsum('bhts,bhsv->bhtv', att, vi)
             + jnp.einsum('bhtd,bhdv->bhtv', qi * jnp.exp(b), S))
        b_end = b[:, :, -1:, :]
        S = (jnp.exp(b_end[:, :, 0, :])[..., None] * S
             + jnp.einsum('bhsd,bhsv->bhdv', ki * jnp.exp(b_end - b), vi))
        return S, o

    S, o = lax.scan(step, s0.astype(jnp.float32), (chunks(q), chunks(k), chunks(v), chunks(log_a)))
    o = o.transpose(1, 0, 3, 2, 4).reshape(B, T, H, dv)
    return o.astype(v.dtype), S


def joint_softmax(parts):
    logits = jnp.concatenate([p[0] for p in parts], axis=-1)
    prob = jax.nn.softmax(logits, axis=-1)
    sizes = [p[0].shape[-1] for p in parts]
    pieces = jnp.split(prob, np.cumsum(sizes)[:-1].tolist(), axis=-1)
    out = jnp.einsum(parts[0][1], pieces[0], parts[0][2].astype(jnp.float32))
    for pc, (_, spec, val) in zip(pieces[1:], parts[1:]):
        out = out + jnp.einsum(spec, pc, val.astype(jnp.float32))
    return out


def moba_prompt(q, k, v):
    B, T, H, dh = q.shape
    nb = -(-T // MOBA_BLOCK)
    pad = nb * MOBA_BLOCK - T
    scale = dh ** -0.5
    qh = q.transpose(0, 2, 1, 3)
    kh = jnp.pad(k, ((0, 0), (0, pad), (0, 0), (0, 0))).transpose(0, 2, 1, 3).reshape(B, H, nb, MOBA_BLOCK, dh)
    vh = jnp.pad(v, ((0, 0), (0, pad), (0, 0), (0, 0))).transpose(0, 2, 1, 3).reshape(B, H, nb, MOBA_BLOCK, dh)
    n_sel = min(MOBA_TOPK, nb - 1)
    if n_sel > 0:
        kmean = jnp.mean(kh.astype(jnp.float32), axis=3)
        gate = jnp.einsum('bhtd,bhnd->bhtn', qh.astype(jnp.float32), kmean)
        t_blk = jnp.arange(T) // MOBA_BLOCK
        fully_past = jnp.arange(nb)[None, :] < t_blk[:, None]
        gate = jnp.where(fully_past, gate, -jnp.inf)
        top_val, top_idx = lax.top_k(gate, n_sel)
        top_ok = jnp.isfinite(top_val)
    bi = jnp.arange(B)[:, None, None, None]
    hi = jnp.arange(H)[None, :, None, None]

    def one_chunk(ci):
        t0 = ci * MOBA_Q_CHUNK
        qc = lax.dynamic_slice_in_dim(qh, t0, MOBA_Q_CHUNK, axis=2)
        tq = t0 + jnp.arange(MOBA_Q_CHUNK)
        ob = t0 // MOBA_BLOCK
        k_own = lax.dynamic_index_in_dim(kh, ob, axis=2, keepdims=False)
        v_own = lax.dynamic_index_in_dim(vh, ob, axis=2, keepdims=False)
        own_pos = ob * MOBA_BLOCK + jnp.arange(MOBA_BLOCK)
        l_own = jnp.einsum('bhqd,bhsd->bhqs', qc, k_own, preferred_element_type=jnp.float32) * scale
        l_own = jnp.where(own_pos[None, :] <= tq[:, None], l_own, -jnp.inf)
        parts = []
        if n_sel > 0:
            idx = lax.dynamic_slice_in_dim(top_idx, t0, MOBA_Q_CHUNK, axis=2)
            ok = lax.dynamic_slice_in_dim(top_ok, t0, MOBA_Q_CHUNK, axis=2)
            kg = kh[bi, hi, idx].reshape(B, H, MOBA_Q_CHUNK, n_sel * MOBA_BLOCK, dh)
            vg = vh[bi, hi, idx].reshape(B, H, MOBA_Q_CHUNK, n_sel * MOBA_BLOCK, dh)
            l_sel = jnp.einsum('bhqd,bhqsd->bhqs', qc, kg, preferred_element_type=jnp.float32) * scale
            ok_rows = jnp.repeat(ok, MOBA_BLOCK, axis=-1)
            l_sel = jnp.where(ok_rows, l_sel, -jnp.inf)
            parts.append((l_sel, 'bhqs,bhqsd->bhqd', vg))
        parts.append((l_own, 'bhqs,bhsd->bhqd', v_own))
        return joint_softmax(parts)

    o = lax.map(one_chunk, jnp.arange(T // MOBA_Q_CHUNK))
    return o.transpose(1, 0, 3, 2, 4).reshape(B, T, H, dh).astype(v.dtype)


def moba_sample(q, k, v, k_pool, v_pool, page_table):
    Bd, Td, H, dh = q.shape
    n_pages = page_table.shape[1]
    past_len = n_pages * PAGE_SIZE
    ppb = MOBA_BLOCK // PAGE_SIZE
    n_full = past_len // MOBA_BLOCK
    ob = n_full
    n_own_past = past_len - ob * MOBA_BLOCK
    n_sel = min(MOBA_TOPK, n_full)
    scale = dh ** -0.5
    qh = q.transpose(0, 2, 1, 3)
    parts = []
    if n_sel > 0:
        page_mean = jnp.mean(k_pool, axis=1, dtype=jnp.float32)
        bmean = page_mean[page_table[:, :n_full * ppb]].reshape(Bd, n_full, ppb, H, dh).mean(axis=2)
        gate = jnp.einsum('bhqd,bnhd->bhqn', qh.astype(jnp.float32), bmean)
        _, idx = lax.top_k(gate, n_sel)
        bi5 = jnp.arange(Bd)[:, None, None, None, None]
        hi5 = jnp.arange(H)[None, :, None, None, None]
        logical = idx[..., None] * ppb + jnp.arange(ppb)
        phys = page_table[bi5, logical]
        kg = k_pool[phys, :, hi5].reshape(Bd, H, Td, n_sel * MOBA_BLOCK, dh)
        vg = v_pool[phys, :, hi5].reshape(Bd, H, Td, n_sel * MOBA_BLOCK, dh)
        l_sel = jnp.einsum('bhqd,bhqsd->bhqs', qh, kg, preferred_element_type=jnp.float32) * scale
        parts.append((l_sel, 'bhqs,bhqsd->bhqd', vg))
    if n_own_past > 0:
        own_pages = page_table[:, ob * ppb: ob * ppb + n_own_past // PAGE_SIZE]
        k_op = k_pool[own_pages].reshape(Bd, n_own_past, H, dh)
        v_op = v_pool[own_pages].reshape(Bd, n_own_past, H, dh)
        l_op = jnp.einsum('bhqd,bshd->bhqs', qh, k_op, preferred_element_type=jnp.float32) * scale
        parts.append((l_op, 'bhqs,bshd->bhqd', v_op))
    tq = jnp.arange(Td)
    l_new = jnp.einsum('bhqd,bshd->bhqs', qh, k, preferred_element_type=jnp.float32) * scale
    l_new = jnp.where(tq[None, :] <= tq[:, None], l_new, -jnp.inf)
    parts.append((l_new, 'bhqs,bshd->bhqd', v))
    o = joint_softmax(parts)
    return o.transpose(0, 2, 1, 3).astype(v.dtype)


def a_project(h, w_in, w_f2, b_f, q_norm, k_norm, pos):
    B, T, _ = h.shape
    sizes = [GLA_KW, GLA_KW, GLA_VW, GLA_VW, GLA_GATE_RANK, MOBA_W, MOBA_W, MOBA_W, MOBA_W]
    gq, gk, gv, gg, gf, mq, mk, mv, mg = jnp.split(h @ w_in, np.cumsum(sizes)[:-1].tolist(), axis=-1)
    gq = gq.reshape(B, T, GLA_HEADS, GLA_DK) * (GLA_DK ** -0.5)
    gk = gk.reshape(B, T, GLA_HEADS, GLA_DK)
    gv = gv.reshape(B, T, GLA_HEADS, GLA_DV)
    log_a = (jax.nn.log_sigmoid((gf @ w_f2 + b_f).astype(jnp.float32)) / GLA_GATE_TAU).reshape(B, T, GLA_HEADS, GLA_DK)
    mq = rope(rms_norm(mq.reshape(B, T, MOBA_HEADS, MOBA_DH), q_norm), pos)
    mk = rope(rms_norm(mk.reshape(B, T, MOBA_HEADS, MOBA_DH), k_norm), pos)
    mv = mv.reshape(B, T, MOBA_HEADS, MOBA_DH)
    return gq, gk, gv, log_a, gg, mq, mk, mv, mg


def a_merge(gla_o, gg, moba_o, mg, o_norm, w_out):
    B, T = gg.shape[:2]
    g = rms_norm(gla_o, o_norm).reshape(B, T, GLA_VW) * jax.nn.silu(gg)
    m = moba_o.reshape(B, T, MOBA_W) * jax.nn.silu(mg)
    return jnp.concatenate([g, m], axis=-1) @ w_out


def _cplx_combine(e1, e2):
    a1r, a1i, b1r, b1i = e1
    a2r, a2i, b2r, b2i = e2
    return (a2r * a1r - a2i * a1i,
            a2r * a1i + a2i * a1r,
            a2r * b1r - a2i * b1i + b2r,
            a2r * b1i + a2i * b1r + b2i)


def s5_mixer(h, w_in, lam_re, lam_im, log_dt, b_re, b_im, c_re, c_im, d, w_glu, b_glu, w_out, x0_re, x0_im):
    B, T, _ = h.shape
    u, z = jnp.split(h @ w_in, 2, axis=-1)
    ug = u.reshape(B, T, S5_GROUPS, S5_GROUP).astype(jnp.float32)
    dt = jnp.exp(log_dt.astype(jnp.float32))[:, None]
    lr = lam_re.astype(jnp.float32)
    li = lam_im.astype(jnp.float32)
    mag = jnp.exp(lr * dt)
    ab_re, ab_im = mag * jnp.cos(li * dt), mag * jnp.sin(li * dt)
    den = lr * lr + li * li
    nr = ab_re - 1.0
    cr = (nr * lr + ab_im * li) / den
    ci = (ab_im * lr - nr * li) / den
    br, bim = b_re.astype(jnp.float32), b_im.astype(jnp.float32)
    bb_re = cr[..., None] * br - ci[..., None] * bim
    bb_im = cr[..., None] * bim + ci[..., None] * br
    bu_re = jnp.einsum('btgi,gpi->tbgp', ug, bb_re)
    bu_im = jnp.einsum('btgi,gpi->tbgp', ug, bb_im)
    x0r = x0_re.astype(jnp.float32)
    x0i = x0_im.astype(jnp.float32)
    bu_re = bu_re.at[0].add(ab_re * x0r - ab_im * x0i)
    bu_im = bu_im.at[0].add(ab_re * x0i + ab_im * x0r)
    a_re = jnp.broadcast_to(ab_re[None, None], (T, 1, S5_GROUPS, S5_P))
    a_im = jnp.broadcast_to(ab_im[None, None], (T, 1, S5_GROUPS, S5_P))
    _, _, xr, xi = lax.associative_scan(_cplx_combine, (a_re, a_im, bu_re, bu_im), axis=0)
    y = (jnp.einsum('tbgp,gip->btgi', xr, c_re.astype(jnp.float32))
         - jnp.einsum('tbgp,gip->btgi', xi, c_im.astype(jnp.float32))
         + d.astype(jnp.float32).reshape(S5_GROUPS, S5_GROUP) * ug)
    y = jax.nn.gelu(y.reshape(B, T, S5_W), approximate=False).astype(h.dtype)
    y = y * jax.nn.sigmoid(y @ w_glu + b_glu)
    out = (y * jax.nn.silu(z)) @ w_out
    return out.astype(h.dtype), xr[-1], xi[-1]


def setup_inputs(seed: int = 0) -> dict:
    key = jax.random.key(seed)
    ks = iter(jax.random.split(key, 40))
    f32 = jnp.float32

    def normal(shape, scale):
        return scale * jax.random.normal(next(ks), shape, f32)

    n_pages = PAST_LEN // PAGE_SIZE
    n_pool = (5 * DEC_BATCH * n_pages) // 4
    x_prompt = normal((BATCH, SEQ, D_MODEL), 1.0)
    x_sample = normal((DEC_BATCH, DEC_SEQ, D_MODEL), 1.0)
    cache_k = normal((N_A_LAYERS, n_pool, PAGE_SIZE, MOBA_HEADS, MOBA_DH), 1.0)
    cache_v = normal((N_A_LAYERS, n_pool, PAGE_SIZE, MOBA_HEADS, MOBA_DH), 1.0)
    state_gla = normal((N_A_LAYERS, DEC_BATCH, GLA_HEADS, GLA_DK, GLA_DV), 0.5)
    state_s5_re = normal((N_C_LAYERS, DEC_BATCH, S5_GROUPS, S5_P), 0.1)
    state_s5_im = normal((N_C_LAYERS, DEC_BATCH, S5_GROUPS, S5_P), 0.1)
    perm = jax.random.permutation(next(ks), n_pool)[:DEC_BATCH * n_pages]
    page_table = perm.reshape(DEC_BATCH, n_pages).astype(jnp.int32)
    lam_im_base = math.pi * jnp.arange(S5_P, dtype=f32)
    return {
        'x_prompt': x_prompt,
        'x_sample': x_sample,
        'cache_k': cache_k,
        'cache_v': cache_v,
        'state_gla': state_gla,
        'state_s5_re': state_s5_re,
        'state_s5_im': state_s5_im,
        'page_table': page_table,
        'norm_a': 1.0 + normal((N_A_LAYERS, D_MODEL), 0.02),
        'w_in_a': normal((N_A_LAYERS, D_MODEL, A_IN), D_MODEL ** -0.5),
        'w_gla_f2': normal((N_A_LAYERS, GLA_GATE_RANK, GLA_KW), GLA_GATE_RANK ** -0.5),
        'b_gla_f': normal((N_A_LAYERS, GLA_KW), 0.1),
        'gla_out_norm': 1.0 + normal((N_A_LAYERS, GLA_DV), 0.02),
        'moba_q_norm': 1.0 + normal((N_A_LAYERS, MOBA_DH), 0.02),
        'moba_k_norm': 1.0 + normal((N_A_LAYERS, MOBA_DH), 0.02),
        'w_out_a': normal((N_A_LAYERS, A_MIX, D_MODEL), A_MIX ** -0.5),
        'norm_c': 1.0 + normal((N_C_LAYERS, D_MODEL), 0.02),
        'w_in_c': normal((N_C_LAYERS, D_MODEL, 2 * S5_W), D_MODEL ** -0.5),
        's5_lambda_re': -0.5 + normal((N_C_LAYERS, S5_GROUPS, S5_P), 0.01),
        's5_lambda_im': lam_im_base + normal((N_C_LAYERS, S5_GROUPS, S5_P), 0.01),
        's5_log_dt': jax.random.uniform(next(ks), (N_C_LAYERS, S5_GROUPS), f32, math.log(S5_DT_MIN), math.log(S5_DT_MAX)),
        's5_b_re': normal((N_C_LAYERS, S5_GROUPS, S5_P, S5_GROUP), (2 * S5_GROUP) ** -0.5),
        's5_b_im': normal((N_C_LAYERS, S5_GROUPS, S5_P, S5_GROUP), (2 * S5_GROUP) ** -0.5),
        's5_c_re': normal((N_C_LAYERS, S5_GROUPS, S5_GROUP, S5_P), S5_P ** -0.5),
        's5_c_im': normal((N_C_LAYERS, S5_GROUPS, S5_GROUP, S5_P), S5_P ** -0.5),
        's5_d': normal((N_C_LAYERS, S5_W), 1.0),
        'w_glu': normal((N_C_LAYERS, S5_W, S5_W), S5_W ** -0.5),
        'b_glu': normal((N_C_LAYERS, S5_W), 0.02),
        'w_out_c': normal((N_C_LAYERS, S5_W, D_MODEL), S5_W ** -0.5),
    }


def reference(x_prompt, x_sample, cache_k, cache_v, state_gla, state_s5_re, state_s5_im, page_table,
              norm_a, w_in_a, w_gla_f2, b_gla_f, gla_out_norm, moba_q_norm, moba_k_norm, w_out_a,
              norm_c, w_in_c, s5_lambda_re, s5_lambda_im, s5_log_dt, s5_b_re, s5_b_im, s5_c_re, s5_c_im,
              s5_d, w_glu, b_glu, w_out_c):
    B, T, _ = x_prompt.shape
    Bd, Td, _ = x_sample.shape
    past_len = page_table.shape[1] * PAGE_SIZE
    pos_p = jnp.arange(T)
    pos_s = past_len + jnp.arange(Td)
    xp, xs = x_prompt, x_sample
    pk, pv, sk, sv, pg, sg, psr, psi, ssr, ssi = [], [], [], [], [], [], [], [], [], []
    for layer in range(DEPTH):
        i = layer // 2
        if layer % 2 == 0:
            gq, gk, gv, la, gg, mq, mk, mv, mg = a_project(rms_norm(xp, norm_a[i]), w_in_a[i], w_gla_f2[i], b_gla_f[i],
                                                           moba_q_norm[i], moba_k_norm[i], pos_p)
            gla_o, s_p = gla_chunked(gq, gk, gv, la, jnp.zeros((B, GLA_HEADS, GLA_DK, GLA_DV), jnp.float32))
            moba_o = moba_prompt(mq, mk, mv)
            xp = xp + a_merge(gla_o, gg, moba_o, mg, gla_out_norm[i], w_out_a[i]).astype(xp.dtype)
            pk.append(mk)
            pv.append(mv)
            pg.append(s_p)
            gq, gk, gv, la, gg, mq, mk, mv, mg = a_project(rms_norm(xs, norm_a[i]), w_in_a[i], w_gla_f2[i], b_gla_f[i],
                                                           moba_q_norm[i], moba_k_norm[i], pos_s)
            gla_o, s_s = gla_chunked(gq, gk, gv, la, state_gla[i])
            moba_o = moba_sample(mq, mk, mv, cache_k[i], cache_v[i], page_table)
            xs = xs + a_merge(gla_o, gg, moba_o, mg, gla_out_norm[i], w_out_a[i]).astype(xs.dtype)
            sk.append(mk)
            sv.append(mv)
            sg.append(s_s)
        else:
            zeros_state = jnp.zeros((B, S5_GROUPS, S5_P), jnp.float32)
            out_p, xr_p, xi_p = s5_mixer(rms_norm(xp, norm_c[i]), w_in_c[i], s5_lambda_re[i], s5_lambda_im[i], s5_log_dt[i],
                                         s5_b_re[i], s5_b_im[i], s5_c_re[i], s5_c_im[i], s5_d[i], w_glu[i], b_glu[i], w_out_c[i],
                                         zeros_state, zeros_state)
            xp = xp + out_p
            psr.append(xr_p)
            psi.append(xi_p)
            out_s, xr_s, xi_s = s5_mixer(rms_norm(xs, norm_c[i]), w_in_c[i], s5_lambda_re[i], s5_lambda_im[i], s5_log_dt[i],
                                         s5_b_re[i], s5_b_im[i], s5_c_re[i], s5_c_im[i], s5_d[i], w_glu[i], b_glu[i], w_out_c[i],
                                         state_s5_re[i], state_s5_im[i])
            xs = xs + out_s
            ssr.append(xr_s)
            ssi.append(xi_s)
    return (xp, xs, jnp.stack(pk), jnp.stack(pv), jnp.stack(sk), jnp.stack(sv), jnp.stack(pg), jnp.stack(sg),
            jnp.stack(psr), jnp.stack(psi), jnp.stack(ssr), jnp.stack(ssi))
```

```python
import functools
import math

import jax
import jax.numpy as jnp
from jax import lax
from jax.experimental import pallas as pl
from jax.experimental.pallas import tpu as pltpu

F32 = jnp.float32
BF16 = jnp.bfloat16
I32 = jnp.int32

RMS_EPS = 1e-6
GLA_HEADS = 8
GLA_DK = 128
GLA_DV = 256
GLA_KW = GLA_HEADS * GLA_DK
GLA_VW = GLA_HEADS * GLA_DV
GLA_GATE_RANK = 16
GLA_GATE_TAU = 16.0
GLA_CHUNK = 64
MOBA_HEADS = 16
MOBA_DH = 128
MOBA_W = MOBA_HEADS * MOBA_DH
MOBA_BLOCK = 256
MOBA_TOPK = 3
ROPE_THETA = 10000.0
PAGE_SIZE = 128
S5_GROUP = 16
S5_P = 64
S5_SLAB = 256
S5_SLAB_STATES = (S5_SLAB // S5_GROUP) * S5_P

LANES = 128
SUBLANES = 8
VMEM_LIMIT = 52 * 1024 * 1024
NEG_INF = float("-inf")

COL_GQ = 0
COL_GK = COL_GQ + GLA_KW
COL_GV = COL_GK + GLA_KW
COL_GG = COL_GV + GLA_VW
COL_MQ = COL_GG + GLA_VW
COL_MK = COL_MQ + MOBA_W
COL_MV = COL_MK + MOBA_W
COL_MG = COL_MV + MOBA_W
COL_GF = COL_MG + MOBA_W
A_TN = 512
A_COLS = COL_GF + A_TN


def _params(*sem):
    return pltpu.CompilerParams(dimension_semantics=sem, vmem_limit_bytes=VMEM_LIMIT)


def _nt(a, b):
    return lax.dot_general(a, b, (((1,), (1,)), ((), ())), preferred_element_type=F32)


def _tn(a, b):
    return lax.dot_general(a, b, (((0,), (0,)), ((), ())), preferred_element_type=F32)


def _silu(x):
    return x / (1.0 + jnp.exp(-x))


def _log_sigmoid(x):
    return jnp.minimum(x, 0.0) - jnp.log(1.0 + jnp.exp(-jnp.abs(x)))


def _rms_rows(x, g):
    ms = jnp.mean(x * x, axis=-1, keepdims=True)
    return x * lax.rsqrt(ms + RMS_EPS) * g


def _aproj_body(x_ref, g_ref, w_ref, qn_ref, kn_ref, cos_ref, sin_ref, o_ref, h_ref, *, tn):
    j = pl.program_id(1)

    @pl.when(j == 0)
    def _():
        h_ref[...] = _rms_rows(x_ref[...], g_ref[...]).astype(BF16)

    acc = jnp.dot(h_ref[...], w_ref[...], preferred_element_type=F32)
    q_lo, k_lo, k_hi = COL_MQ // tn, COL_MK // tn, COL_MV // tn
    is_qk = jnp.logical_and(j >= q_lo, j < k_hi)

    @pl.when(is_qk)
    def _():
        gain = jnp.where(j < k_lo, qn_ref[...], kn_ref[...])
        cos = cos_ref[...]
        sin = sin_ref[...]
        for hh in range(tn // MOBA_DH):
            y = _rms_rows(acc[:, hh * MOBA_DH:(hh + 1) * MOBA_DH], gain)
            o_ref[:, hh * MOBA_DH:(hh + 1) * MOBA_DH] = y * cos + pltpu.roll(y, MOBA_DH // 2, 1) * sin

    @pl.when(jnp.logical_not(is_qk))
    def _():
        o_ref[...] = acc


def _aproj(x, g, wcat, qn, kn, cos, sin, *, tm):
    m, d = x.shape
    n = wcat.shape[1]
    tn = A_TN
    return pl.pallas_call(
        functools.partial(_aproj_body, tn=tn),
        grid=(m // tm, n // tn),
        in_specs=[
            pl.BlockSpec((tm, d), lambda i, j: (i, 0)),
            pl.BlockSpec((1, d), lambda i, j: (0, 0)),
            pl.BlockSpec((d, tn), lambda i, j: (0, j)),
            pl.BlockSpec((1, MOBA_DH), lambda i, j: (0, 0)),
            pl.BlockSpec((1, MOBA_DH), lambda i, j: (0, 0)),
            pl.BlockSpec((tm, MOBA_DH), lambda i, j: (i, 0)),
            pl.BlockSpec((tm, MOBA_DH), lambda i, j: (i, 0)),
        ],
        out_specs=pl.BlockSpec((tm, tn), lambda i, j: (i, j)),
        out_shape=jax.ShapeDtypeStruct((m, n), F32),
        scratch_shapes=[pltpu.VMEM((tm, d), BF16)],
        compiler_params=_params("parallel", "arbitrary"),
        name="a_proj",
    )(x, g.reshape(1, d), wcat, qn.reshape(1, -1), kn.reshape(1, -1), cos, sin)


def _gemm_norm_body(x_ref, g_ref, w_ref, o_ref, h_ref):
    @pl.when(pl.program_id(1) == 0)
    def _():
        h_ref[...] = _rms_rows(x_ref[...], g_ref[...]).astype(BF16)

    o_ref[...] = jnp.dot(h_ref[...], w_ref[...], preferred_element_type=F32)


def _gemm_norm(x, g, w, *, tm, tn):
    m, d = x.shape
    n = w.shape[1]
    return pl.pallas_call(
        _gemm_norm_body,
        grid=(m // tm, n // tn),
        in_specs=[
            pl.BlockSpec((tm, d), lambda i, j: (i, 0)),
            pl.BlockSpec((1, d), lambda i, j: (0, 0)),
            pl.BlockSpec((d, tn), lambda i, j: (0, j)),
        ],
        out_specs=pl.BlockSpec((tm, tn), lambda i, j: (i, j)),
        out_shape=jax.ShapeDtypeStruct((m, n), F32),
        scratch_shapes=[pltpu.VMEM((tm, d), BF16)],
        compiler_params=_params("parallel", "arbitrary"),
        name="c_proj",
    )(x, g.reshape(1, d), w)


def _gemm_res_body(a_ref, w_ref, r_ref, o_ref):
    o_ref[...] = r_ref[...] + jnp.dot(a_ref[...], w_ref[...], preferred_element_type=F32)


def _gemm_res(a, w, res, *, tm, tn):
    m, k = a.shape
    n = w.shape[1]
    return pl.pallas_call(
        _gemm_res_body,
        grid=(m // tm, n // tn),
        in_specs=[
            pl.BlockSpec((tm, k), lambda i, j: (i, 0)),
            pl.BlockSpec((k, tn), lambda i, j: (0, j)),
            pl.BlockSpec((tm, tn), lambda i, j: (i, j)),
        ],
        out_specs=pl.BlockSpec((tm, tn), lambda i, j: (i, j)),
        out_shape=jax.ShapeDtypeStruct((m, n), F32),
        compiler_params=_params("parallel", "arbitrary"),
        name="out_proj",
    )(a, w, res)


def _gemm_glu_body(a_ref, w_ref, y_ref, z_ref, b_ref, o_ref, h_ref):
    @pl.when(pl.program_id(1) == 0)
    def _():
        h_ref[...] = a_ref[...].astype(BF16)

    t = jnp.dot(h_ref[...], w_ref[...], preferred_element_type=F32) + b_ref[...]
    y = y_ref[...]
    o_ref[...] = ((y / (1.0 + jnp.exp(-t))) * _silu(z_ref[...])).astype(o_ref.dtype)


def _gemm_glu(y, w, uz, bias, *, tm, tn):
    m, k = y.shape
    n = w.shape[1]
    zoff = n // tn
    return pl.pallas_call(
        _gemm_glu_body,
        grid=(m // tm, n // tn),
        in_specs=[
            pl.BlockSpec((tm, k), lambda i, j: (i, 0)),
            pl.BlockSpec((k, tn), lambda i, j: (0, j)),
            pl.BlockSpec((tm, tn), lambda i, j: (i, j)),
            pl.BlockSpec((tm, tn), lambda i, j: (i, zoff + j)),
            pl.BlockSpec((1, tn), lambda i, j: (0, j)),
        ],
        out_specs=pl.BlockSpec((tm, tn), lambda i, j: (i, j)),
        out_shape=jax.ShapeDtypeStruct((m, n), BF16),
        scratch_shapes=[pltpu.VMEM((tm, k), BF16)],
        compiler_params=_params("parallel", "arbitrary"),
        name="glu_proj",
    )(y, w, y, uz, bias.reshape(1, n))


def _gla_body(q_ref, k_ref, v_ref, gf_ref, wf_ref, bf_ref, s0_ref, o_ref, s_ref, st_ref, *, chunk):
    c = chunk
    n = q_ref.shape[0] // c
    scale = GLA_DK ** -0.5
    st_ref[...] = s0_ref[...].T
    row = lax.broadcasted_iota(I32, (c, c), 0)
    col = lax.broadcasted_iota(I32, (c, c), 1)
    tril = col <= row
    trilf = tril.astype(F32)

    def step(i, carry):
        sl = pl.ds(pl.multiple_of(i * c, c), c)
        q = q_ref[sl, :] * scale
        k = k_ref[sl, :]
        v = v_ref[sl, :].astype(BF16)
        gf = gf_ref[sl, :][:, :GLA_GATE_RANK]
        pre = jnp.dot(gf.astype(BF16), wf_ref[...].astype(BF16), preferred_element_type=F32) + bf_ref[...]
        g = _log_sigmoid(pre) / GLA_GATE_TAU
        b = jnp.dot(trilf, g, precision=lax.Precision.HIGHEST, preferred_element_type=F32)
        bm = b[c // 2 - 1:c // 2, :]
        be = b[c - 1:c, :]
        qe = (q * jnp.exp(b - bm)).astype(BF16)
        ke = (k * jnp.exp(bm - b)).astype(BF16)
        att = jnp.where(tril, _nt(qe, ke), 0.0)
        st = st_ref[...]
        o = jnp.dot(att.astype(BF16), v, preferred_element_type=F32)
        o = o + _nt((q * jnp.exp(b)).astype(BF16), st.astype(BF16))
        o_ref[sl, :] = o
        kd = (k * jnp.exp(be - b)).astype(BF16)
        st_ref[...] = st * jnp.exp(be) + _tn(v, kd)
        return carry

    lax.fori_loop(0, n, step, 0)
    s_ref[...] = st_ref[...].T


def _gla_small_body(q_ref, k_ref, v_ref, gf_ref, wf_ref, bf_ref, s0_ref, o_ref, s_ref):
    c = q_ref.shape[0]
    scale = GLA_DK ** -0.5
    q = q_ref[...] * scale
    k = k_ref[...]
    v = v_ref[...]
    pre = bf_ref[...] + jnp.zeros((c, GLA_DK), F32)
    for r in range(GLA_GATE_RANK):
        pre = pre + gf_ref[:, r:r + 1] * wf_ref[r:r + 1, :]
    g = _log_sigmoid(pre) / GLA_GATE_TAU
    row = lax.broadcasted_iota(I32, (c, GLA_DK), 0)
    b = jnp.zeros((c, GLA_DK), F32)
    for s in range(c):
        b = b + jnp.where(row >= s, g[s:s + 1, :], 0.0)
    be = b[c - 1:c, :]
    s0 = s0_ref[...]
    o = jnp.dot(q * jnp.exp(b), s0, preferred_element_type=F32)
    for s in range(c):
        e = jnp.exp(jnp.where(row >= s, b - b[s:s + 1, :], NEG_INF))
        a_col = jnp.sum(q * k[s:s + 1, :] * e, axis=-1, keepdims=True)
        o = o + a_col * v[s:s + 1, :]
    o_ref[...] = o
    kd = k * jnp.exp(be - b)
    pad = jnp.concatenate([kd, jnp.exp(be), jnp.zeros((LANES - c - 1, GLA_DK), F32)], axis=0)
    padt = pad.T
    s_new = s0 * padt[:, c:c + 1]
    for s in range(c):
        s_new = s_new + padt[:, s:s + 1] * v[s:s + 1, :]
    s_ref[...] = s_new


def _gla(p, w_f2, b_f, s0, *, nb, t):
    small = t % GLA_CHUNK != 0
    kq, kk, kv = COL_GQ // GLA_DK, COL_GK // GLA_DK, COL_GV // GLA_DV
    kf = COL_GF // LANES
    in_specs = [
        pl.BlockSpec((t, GLA_DK), lambda b, h: (b, kq + h)),
        pl.BlockSpec((t, GLA_DK), lambda b, h: (b, kk + h)),
        pl.BlockSpec((t, GLA_DV), lambda b, h: (b, kv + h)),
        pl.BlockSpec((t, LANES), lambda b, h: (b, kf)),
        pl.BlockSpec((GLA_GATE_RANK, GLA_DK), lambda b, h: (0, h)),
        pl.BlockSpec((1, GLA_DK), lambda b, h: (0, h)),
        pl.BlockSpec((None, None, GLA_DK, GLA_DV), lambda b, h: (b, h, 0, 0)),
    ]
    out_specs = [
        pl.BlockSpec((t, GLA_DV), lambda b, h: (b, h)),
        pl.BlockSpec((None, None, GLA_DK, GLA_DV), lambda b, h: (b, h, 0, 0)),
    ]
    out_shape = [
        jax.ShapeDtypeStruct((nb * t, GLA_VW), F32),
        jax.ShapeDtypeStruct((nb, GLA_HEADS, GLA_DK, GLA_DV), F32),
    ]
    if small:
        body, scratch = _gla_small_body, []
    else:
        body = functools.partial(_gla_body, chunk=GLA_CHUNK)
        scratch = [pltpu.VMEM((GLA_DV, GLA_DK), F32)]
    return pl.pallas_call(
        body,
        grid=(nb, GLA_HEADS),
        in_specs=in_specs,
        out_specs=out_specs,
        out_shape=out_shape,
        scratch_shapes=scratch,
        compiler_params=_params("parallel", "parallel"),
        name="gla_small" if small else "gla",
    )(p, p, p, p, w_f2, b_f.reshape(1, -1), s0)


def _moba_prompt_body(q_ref, k_ref, v_ref, o_ref, *, nblk):
    qi = pl.program_id(2)
    blk = MOBA_BLOCK
    scale = MOBA_DH ** -0.5
    q = q_ref[...]
    lane = lax.broadcasted_iota(I32, (blk, LANES), 1)
    gate = jnp.zeros((blk, LANES), F32)
    for n in range(nblk):
        km = jnp.mean(k_ref[n * blk:(n + 1) * blk, :], axis=0, keepdims=True)
        gate = jnp.where(lane == n, jnp.sum(q * km, axis=-1, keepdims=True), gate)
    valid = lane < qi
    gm = jnp.where(valid, gate, NEG_INF)
    rank = jnp.zeros((blk, LANES), I32)
    for m in range(nblk):
        g_m = gm[:, m:m + 1]
        beats = jnp.logical_or(g_m > gm, jnp.logical_and(g_m == gm, m < lane))
        rank = rank + beats.astype(I32)
    sel = jnp.logical_and(valid, rank < MOBA_TOPK).astype(F32)

    qb = q.astype(BF16)
    own = pl.ds(pl.multiple_of(qi * blk, blk), blk)
    s = _nt(qb, k_ref[own, :].astype(BF16)) * scale
    r2 = lax.broadcasted_iota(I32, (blk, blk), 0)
    c2 = lax.broadcasted_iota(I32, (blk, blk), 1)
    s = jnp.where(c2 <= r2, s, NEG_INF)
    m0 = jnp.max(s, axis=-1, keepdims=True)
    p = jnp.exp(s - m0)
    l0 = jnp.sum(p, axis=-1, keepdims=True)
    acc0 = jnp.dot(p.astype(BF16), v_ref[own, :].astype(BF16), preferred_element_type=F32)

    def body(n, carry):
        m_run, l_run, acc = carry
        past = pl.ds(pl.multiple_of(n * blk, blk), blk)
        sn = _nt(qb, k_ref[past, :].astype(BF16)) * scale
        seln = jnp.max(jnp.where(lane == n, sel, 0.0), axis=-1, keepdims=True) > 0.0
        sn = jnp.where(seln, sn, NEG_INF)
        m_new = jnp.maximum(m_run, jnp.max(sn, axis=-1, keepdims=True))
        alpha = jnp.exp(m_run - m_new)
        pn = jnp.exp(sn - m_new)
        l_new = alpha * l_run + jnp.sum(pn, axis=-1, keepdims=True)
        acc = alpha * acc + jnp.dot(pn.astype(BF16), v_ref[past, :].astype(BF16), preferred_element_type=F32)
        return m_new, l_new, acc

    _, l_fin, acc = lax.fori_loop(0, qi, body, (m0, l0, acc0))
    o_ref[...] = acc / l_fin


def _moba_prompt(p, *, nb, t):
    nblk = t // MOBA_BLOCK
    kq, kk, kv = COL_MQ // MOBA_DH, COL_MK // MOBA_DH, COL_MV // MOBA_DH
    return pl.pallas_call(
        functools.partial(_moba_prompt_body, nblk=nblk),
        grid=(nb, MOBA_HEADS, nblk),
        in_specs=[
            pl.BlockSpec((MOBA_BLOCK, MOBA_DH), lambda b, h, i: (b * nblk + i, kq + h)),
            pl.BlockSpec((t, MOBA_DH), lambda b, h, i: (b, kk + h)),
            pl.BlockSpec((t, MOBA_DH), lambda b, h, i: (b, kv + h)),
        ],
        out_specs=pl.BlockSpec((MOBA_BLOCK, MOBA_DH), lambda b, h, i: (b * nblk + i, h)),
        out_shape=jax.ShapeDtypeStruct((nb * t, MOBA_W), F32),
        compiler_params=_params("parallel", "parallel", "arbitrary"),
        name="moba_prompt",
    )(p, p, p)


def _bmean_body(pt_ref, p0_ref, p1_ref, o_ref):
    rows = MOBA_BLOCK
    o_ref[...] = (jnp.sum(p0_ref[...], axis=0) + jnp.sum(p1_ref[...], axis=0)) * (1.0 / rows)


def _block_means(cache_k, page_table, layer):
    nb, n_pages = page_table.shape
    ppb = MOBA_BLOCK // PAGE_SIZE
    assert ppb == 2
    n_full = n_pages // ppb
    page_block = (None, None, PAGE_SIZE, MOBA_HEADS, MOBA_DH)
    return pl.pallas_call(
        _bmean_body,
        grid_spec=pltpu.PrefetchScalarGridSpec(
            num_scalar_prefetch=1,
            grid=(nb, n_full),
            in_specs=[
                pl.BlockSpec(page_block, lambda b, n, pt: (layer, pt[b, 2 * n], 0, 0, 0)),
                pl.BlockSpec(page_block, lambda b, n, pt: (layer, pt[b, 2 * n + 1], 0, 0, 0)),
            ],
            out_specs=pl.BlockSpec((None, None, MOBA_HEADS, MOBA_DH), lambda b, n, pt: (b, n, 0, 0)),
        ),
        out_shape=jax.ShapeDtypeStruct((nb, n_full, MOBA_HEADS, MOBA_DH), F32),
        compiler_params=_params("parallel", "arbitrary"),
        name="moba_block_means",
    )(page_table, cache_k, cache_k)


def _gate_topk_body(q_ref, bm_ref, o_ref, *, td):
    n_full = bm_ref.shape[1]
    lane = lax.broadcasted_iota(I32, (n_full, LANES), 1)
    row = lax.broadcasted_iota(I32, (n_full, LANES), 0)
    orow = lax.broadcasted_iota(I32, (SUBLANES, LANES), 0)
    for h in range(MOBA_HEADS):
        km = bm_ref[h]
        g = jnp.full((n_full, LANES), NEG_INF, F32)
        for t in range(td):
            qv = q_ref[t:t + 1, h * MOBA_DH:(h + 1) * MOBA_DH]
            g = jnp.where(lane == t, jnp.sum(km * qv, axis=-1, keepdims=True), g)
        outv = jnp.zeros((SUBLANES, LANES), I32)
        for j in range(MOBA_TOPK):
            mx = jnp.max(g, axis=0, keepdims=True)
            idx = jnp.min(jnp.where(g == mx, row, n_full), axis=0, keepdims=True)
            g = jnp.where(row == idx, NEG_INF, g)
            outv = jnp.where(orow == j, idx, outv)
        o_ref[h] = outv


def _gate_topk(p, bmean_t, *, nb, td):
    n_full = bmean_t.shape[2]
    assert td <= SUBLANES and MOBA_TOPK <= SUBLANES
    kq = COL_MQ // MOBA_W
    return pl.pallas_call(
        functools.partial(_gate_topk_body, td=td),
        grid=(nb,),
        in_specs=[
            pl.BlockSpec((td, MOBA_W), lambda b: (b, kq)),
            pl.BlockSpec((None, MOBA_HEADS, n_full, MOBA_DH), lambda b: (b, 0, 0, 0)),
        ],
        out_specs=pl.BlockSpec((None, MOBA_HEADS, SUBLANES, LANES), lambda b: (b, 0, 0, 0)),
        out_shape=jax.ShapeDtypeStruct((nb, MOBA_HEADS, SUBLANES, LANES), I32),
        compiler_params=_params("parallel"),
        name="moba_gate_topk",
    )(p, bmean_t)


def _moba_sample_body(idx_ref, pt_ref, q_ref, k_ref, v_ref, ck_ref, cv_ref, o_ref, kbuf, vbuf, sem, *, layer, td):
    b = pl.program_id(0)
    h = pl.program_id(1)
    ppb = MOBA_BLOCK // PAGE_SIZE
    per_q = MOBA_TOPK * ppb
    nsel = MOBA_TOPK * MOBA_BLOCK
    scale = MOBA_DH ** -0.5

    def copies(t, j, pg):
        blk = idx_ref[((b * MOBA_HEADS + h) * td + t) * MOBA_TOPK + j]
        page = pt_ref[b, blk * ppb + pg]
        dst = pl.ds((t * per_q + j * ppb + pg) * PAGE_SIZE, PAGE_SIZE)
        return (pltpu.make_async_copy(ck_ref.at[layer, page, :, h, :], kbuf.at[dst, :], sem.at[0]),
                pltpu.make_async_copy(cv_ref.at[layer, page, :, h, :], vbuf.at[dst, :], sem.at[1]))

    slots = [(t, j, pg) for t in range(td) for j in range(MOBA_TOPK) for pg in range(ppb)]
    for s in slots:
        ck, cv = copies(*s)
        ck.start()
        cv.start()
    for s in slots:
        ck, cv = copies(*s)
        ck.wait()
        cv.wait()

    rows = 2 * SUBLANES
    q = q_ref[...]
    qp = jnp.concatenate([q, jnp.zeros((rows - td, MOBA_DH), F32)], axis=0).astype(BF16)
    s_sel = _nt(qp, kbuf[...].astype(BF16)) * scale
    r = lax.broadcasted_iota(I32, (rows, td * nsel), 0)
    c = lax.broadcasted_iota(I32, (rows, td * nsel), 1)
    mine = jnp.logical_and(c >= r * nsel, c < (r + 1) * nsel)
    s_sel = jnp.where(mine, s_sel, NEG_INF)

    k_new = k_ref[...]
    v_new = v_ref[...]
    rn = lax.broadcasted_iota(I32, (td, LANES), 0)
    cn = lax.broadcasted_iota(I32, (td, LANES), 1)
    s_new = jnp.full((td, LANES), NEG_INF, F32)
    for t in range(td):
        col = jnp.sum(q * k_new[t:t + 1, :], axis=-1, keepdims=True) * scale
        s_new = jnp.where(jnp.logical_and(cn == t, rn >= t), col, s_new)

    s_sel = s_sel[:td, :]
    m = jnp.maximum(jnp.max(s_sel, axis=-1, keepdims=True), jnp.max(s_new, axis=-1, keepdims=True))
    p_sel = jnp.exp(s_sel - m)
    p_new = jnp.exp(s_new - m)
    l = jnp.sum(p_sel, axis=-1, keepdims=True) + jnp.sum(p_new, axis=-1, keepdims=True)
    pp = jnp.concatenate([p_sel, jnp.zeros((rows - td, td * nsel), F32)], axis=0).astype(BF16)
    acc = jnp.dot(pp, vbuf[...].astype(BF16), preferred_element_type=F32)[:td, :]
    for t in range(td):
        acc = acc + p_new[:, t:t + 1] * v_new[t:t + 1, :]
    o_ref[...] = acc / l


def _moba_sample(p, cache_k, cache_v, page_table, idx_flat, *, layer, nb, td):
    kq, kk, kv = COL_MQ // MOBA_DH, COL_MK // MOBA_DH, COL_MV // MOBA_DH
    nrows = td * MOBA_TOPK * MOBA_BLOCK
    row_spec = lambda off: pl.BlockSpec((td, MOBA_DH), lambda b, h, idx, pt: (b, off + h))
    return pl.pallas_call(
        functools.partial(_moba_sample_body, layer=layer, td=td),
        grid_spec=pltpu.PrefetchScalarGridSpec(
            num_scalar_prefetch=2,
            grid=(nb, MOBA_HEADS),
            in_specs=[
                row_spec(kq), row_spec(kk), row_spec(kv),
                pl.BlockSpec(memory_space=pl.ANY),
                pl.BlockSpec(memory_space=pl.ANY),
            ],
            out_specs=pl.BlockSpec((td, MOBA_DH), lambda b, h, idx, pt: (b, h)),
            scratch_shapes=[
                pltpu.VMEM((nrows, MOBA_DH), F32),
                pltpu.VMEM((nrows, MOBA_DH), F32),
                pltpu.SemaphoreType.DMA((2,)),
            ],
        ),
        out_shape=jax.ShapeDtypeStruct((nb * td, MOBA_W), F32),
        compiler_params=_params("arbitrary", "arbitrary"),
        name="moba_sample",
    )(idx_flat, page_table, p, p, p, cache_k, cache_v)


def _merge_body(go_ref, gg_ref, mo_ref, mg_ref, on_ref, a_ref):
    on = on_ref[...]
    for h in range(GLA_HEADS):
        sl = slice(h * GLA_DV, (h + 1) * GLA_DV)
        g = _rms_rows(go_ref[:, sl], on) * _silu(gg_ref[:, sl])
        a_ref[:, sl] = g.astype(a_ref.dtype)
    a_ref[:, GLA_VW:] = (mo_ref[...] * _silu(mg_ref[...])).astype(a_ref.dtype)


def _merge(gla_o, moba_o, p, o_norm, *, tm):
    m = gla_o.shape[0]
    kg, km = COL_GG // GLA_VW, COL_MG // MOBA_W
    return pl.pallas_call(
        _merge_body,
        grid=(m // tm,),
        in_specs=[
            pl.BlockSpec((tm, GLA_VW), lambda i: (i, 0)),
            pl.BlockSpec((tm, GLA_VW), lambda i: (i, kg)),
            pl.BlockSpec((tm, MOBA_W), lambda i: (i, 0)),
            pl.BlockSpec((tm, MOBA_W), lambda i: (i, km)),
            pl.BlockSpec((1, GLA_DV), lambda i: (0, 0)),
        ],
        out_specs=pl.BlockSpec((tm, GLA_VW + MOBA_W), lambda i: (i, 0)),
        out_shape=jax.ShapeDtypeStruct((m, GLA_VW + MOBA_W), BF16),
        compiler_params=_params("parallel"),
        name="a_merge",
    )(gla_o, p, moba_o, p, o_norm.reshape(1, -1))


def _s5_coef_body(lr_ref, li_ref, ldt_ref, br_ref, bi_ref, bbr_ref, bbi_ref, abr_ref, abi_ref):
    lr = lr_ref[...]
    li = li_ref[...]
    dt = jnp.exp(ldt_ref[...])
    mag = jnp.exp(lr * dt)
    ab_re = mag * jnp.cos(li * dt)
    ab_im = mag * jnp.sin(li * dt)
    den = lr * lr + li * li
    nr = ab_re - 1.0
    cr = (nr * lr + ab_im * li) / den
    ci = (ab_im * lr - nr * li) / den
    br = br_ref[...]
    bi = bi_ref[...]
    bbr_ref[...] = cr * br - ci * bi
    bbi_ref[...] = cr * bi + ci * br
    abr_ref[...] = ab_re
    abi_ref[...] = ab_im


def _s5_coef(lam_re, lam_im, log_dt, b_re, b_im):
    g, p = lam_re.shape
    w = p * S5_GROUP
    rep = lambda a: jnp.repeat(a, S5_GROUP, axis=1)
    full = pl.BlockSpec((g, w), lambda: (0, 0))
    outs = pl.pallas_call(
        _s5_coef_body,
        in_specs=[full] * 5,
        out_specs=[full] * 4,
        out_shape=[jax.ShapeDtypeStruct((g, w), F32)] * 4,
        name="s5_coef",
    )(rep(lam_re), rep(lam_im), jnp.broadcast_to(log_dt[:, None], (g, w)),
      b_re.reshape(g, w), b_im.reshape(g, w))
    bbr, bbi, abr, abi = outs
    return (bbr.reshape(g, p, S5_GROUP), bbi.reshape(g, p, S5_GROUP),
            abr[:, ::S5_GROUP], abi[:, ::S5_GROUP])


def _s5_body(u_ref, bb_ref, cc_ref, ar_ref, ai_ref, d_ref, x0r_ref, x0i_ref,
             y_ref, xr_ref, xi_ref, lhs, bu, ybuf, cr, ci, *, ns, nt):
    ti = pl.program_id(1)
    tpv = SUBLANES // ns
    nst = S5_SLAB_STATES
    lo_rows = SUBLANES - ns
    nlb = S5_SLAB // LANES

    @pl.when(ti == 0)
    def _():
        cr[...] = jnp.zeros_like(cr)
        ci[...] = jnp.zeros_like(ci)
        cr[lo_rows:, :] = x0r_ref[...]
        ci[lo_rows:, :] = x0i_ref[...]

    for b in range(ns):
        for j in range(nlb):
            lhs[j, pl.ds(b, nt, stride=ns), :] = u_ref[b, :, j * LANES:(j + 1) * LANES]
    u_rows = jnp.concatenate([lhs[j] for j in range(nlb)], axis=1)
    bu[...] = jnp.dot(u_rows.astype(BF16), bb_ref[...], preferred_element_type=F32)

    ar = ar_ref[...]
    ai = ai_ref[...]
    if tpv == 1:
        def step(v, carry):
            xr, xi = carry
            sl = pl.ds(pl.multiple_of(v * SUBLANES, SUBLANES), SUBLANES)
            nxr = ar * xr - ai * xi + bu[sl, :nst]
            nxi = ar * xi + ai * xr + bu[sl, nst:]
            bu[sl, :nst] = nxr
            bu[sl, nst:] = nxi
            return nxr, nxi
    else:
        assert tpv == 2
        lo = lax.broadcasted_iota(I32, (SUBLANES, nst), 0) < ns
        a2r = ar * ar - ai * ai
        a2i = 2.0 * ar * ai
        c1r = jnp.where(lo, 0.0, ar)
        c1i = jnp.where(lo, 0.0, ai)
        c2r = jnp.where(lo, ar, a2r)
        c2i = jnp.where(lo, ai, a2i)

        def step(v, carry):
            xr, xi = carry
            sl = pl.ds(pl.multiple_of(v * SUBLANES, SUBLANES), SUBLANES)
            br = bu[sl, :nst]
            bi = bu[sl, nst:]
            sbr = pltpu.roll(br, ns, 0)
            sbi = pltpu.roll(bi, ns, 0)
            pr = jnp.where(lo, pltpu.roll(xr, ns, 0), xr)
            pi = jnp.where(lo, pltpu.roll(xi, ns, 0), xi)
            nxr = br + c1r * sbr - c1i * sbi + c2r * pr - c2i * pi
            nxi = bi + c1r * sbi + c1i * sbr + c2r * pi + c2i * pr
            bu[sl, :nst] = nxr
            bu[sl, nst:] = nxi
            return nxr, nxi

    nv = (nt * ns) // SUBLANES
    xr, xi = lax.fori_loop(0, nv, step, (cr[...], ci[...]), unroll=2 if nv % 2 == 0 else 1)
    cr[...] = xr
    ci[...] = xi

    yv = jnp.dot(bu[...].astype(BF16), cc_ref[...], preferred_element_type=F32) + d_ref[...] * u_rows
    yv = 0.5 * yv * (1.0 + lax.erf(yv * math.sqrt(0.5)))
    for j in range(nlb):
        ybuf[j] = yv[:, j * LANES:(j + 1) * LANES]
    for b in range(ns):
        for j in range(nlb):
            y_ref[b, :, j * LANES:(j + 1) * LANES] = ybuf[j, pl.ds(b, nt, stride=ns), :]

    @pl.when(ti == pl.num_programs(1) - 1)
    def _():
        xr_ref[...] = cr[lo_rows:, :]
        xi_ref[...] = ci[lo_rows:, :]


def _s5_core(uz, bb, cc, ab_re, ab_im, d, x0_re, x0_im, *, ns, t, nt):
    w = d.shape[0]
    nslab = w // S5_SLAB
    nst = S5_SLAB_STATES
    assert SUBLANES % ns == 0 and t % nt == 0 and (nt * ns) % SUBLANES == 0
    u3 = uz.reshape(ns, t, uz.shape[1])
    to_slab = lambda a: jnp.transpose(a.reshape(ns, nslab, nst), (1, 0, 2))
    st_spec = pl.BlockSpec((None, ns, nst), lambda s, i: (s, 0, 0))
    y, xr, xi = pl.pallas_call(
        functools.partial(_s5_body, ns=ns, nt=nt),
        grid=(nslab, t // nt),
        in_specs=[
            pl.BlockSpec((ns, nt, S5_SLAB), lambda s, i: (0, i, s)),
            pl.BlockSpec((None, S5_SLAB, 2 * nst), lambda s, i: (s, 0, 0)),
            pl.BlockSpec((None, 2 * nst, S5_SLAB), lambda s, i: (s, 0, 0)),
            pl.BlockSpec((None, 1, nst), lambda s, i: (s, 0, 0)),
            pl.BlockSpec((None, 1, nst), lambda s, i: (s, 0, 0)),
            pl.BlockSpec((1, S5_SLAB), lambda s, i: (0, s)),
            st_spec, st_spec,
        ],
        out_specs=[
            pl.BlockSpec((ns, nt, S5_SLAB), lambda s, i: (0, i, s)),
            st_spec, st_spec,
        ],
        out_shape=[
            jax.ShapeDtypeStruct((ns, t, w), F32),
            jax.ShapeDtypeStruct((nslab, ns, nst), F32),
            jax.ShapeDtypeStruct((nslab, ns, nst), F32),
        ],
        scratch_shapes=[
            pltpu.VMEM((S5_SLAB // LANES, nt * ns, LANES), F32),
            pltpu.VMEM((nt * ns, 2 * nst), F32),
            pltpu.VMEM((S5_SLAB // LANES, nt * ns, LANES), F32),
            pltpu.VMEM((SUBLANES, nst), F32),
            pltpu.VMEM((SUBLANES, nst), F32),
        ],
        compiler_params=_params("parallel", "arbitrary"),
        name="s5_core",
    )(u3, bb, cc, ab_re.reshape(nslab, 1, nst), ab_im.reshape(nslab, 1, nst), d.reshape(1, w),
      to_slab(x0_re), to_slab(x0_im))
    g = w // S5_GROUP
    from_slab = lambda a: jnp.transpose(a, (1, 0, 2)).reshape(ns, g, S5_P)
    return y.reshape(ns * t, w), from_slab(xr), from_slab(xi)


def _block_diag(x):
    nslab, gl, r, c = x.shape
    eye = jnp.eye(gl, dtype=x.dtype)
    bd = x[:, :, :, None, :] * eye[None, :, None, :, None]
    return bd.reshape(nslab, gl * r, gl * c)


def _rope_tables(pos, reps):
    half = MOBA_DH // 2
    inv_freq = ROPE_THETA ** (-jnp.arange(half, dtype=F32) / half)
    ang = pos.astype(F32)[:, None] * inv_freq[None, :]
    cos = jnp.cos(ang)
    sin = jnp.sin(ang)
    cos2 = jnp.concatenate([cos, cos], axis=-1)
    sin2 = jnp.concatenate([-sin, sin], axis=-1)
    return jnp.tile(cos2, (reps, 1)), jnp.tile(sin2, (reps, 1))


def _layer_a(x, pos, nb, t, s0, weights, sample_ctx):
    norm, wcat, w_f2, b_f, o_norm, q_norm, k_norm, w_out = weights
    m = nb * t
    tm = min(m, 512)
    cos, sin = _rope_tables(pos, nb)
    p = _aproj(x, norm, wcat, q_norm, k_norm, cos, sin, tm=tm)
    gla_o, state = _gla(p, w_f2, b_f, s0, nb=nb, t=t)
    if sample_ctx is None:
        moba_o = _moba_prompt(p, nb=nb, t=t)
    else:
        cache_k, cache_v, page_table, layer = sample_ctx
        bmean = _block_means(cache_k, page_table, layer)
        ids = _gate_topk(p, jnp.transpose(bmean, (0, 2, 1, 3)), nb=nb, td=t)
        idx_flat = jnp.transpose(ids[:, :, :MOBA_TOPK, :t], (0, 1, 3, 2)).reshape(-1)
        moba_o = _moba_sample(p, cache_k, cache_v, page_table, idx_flat, layer=layer, nb=nb, td=t)
    a = _merge(gla_o, moba_o, p, o_norm, tm=min(m, 256))
    x_new = _gemm_res(a, w_out, x, tm=min(m, 1024), tn=512)
    mk = p[:, COL_MK:COL_MK + MOBA_W].reshape(nb, t, MOBA_HEADS, MOBA_DH)
    mv = p[:, COL_MV:COL_MV + MOBA_W].reshape(nb, t, MOBA_HEADS, MOBA_DH)
    return x_new, mk, mv, state


def _layer_c(x, ns, t, x0_re, x0_im, weights):
    norm, w_in, bb, cc, ab_re, ab_im, d, w_glu, b_glu, w_out = weights
    m = ns * t
    uz = _gemm_norm(x, norm, w_in, tm=min(m, 512), tn=512)
    y, xr, xi = _s5_core(uz, bb, cc, ab_re, ab_im, d, x0_re, x0_im, ns=ns, t=t, nt=min(t, 256))
    v = _gemm_glu(y, w_glu, uz, b_glu, tm=min(m, 512), tn=512)
    x_new = _gemm_res(v, w_out, x, tm=min(m, 1024), tn=512)
    return x_new, xr, xi


def kernel(x_prompt, x_sample, cache_k, cache_v, state_gla, state_s5_re, state_s5_im, page_table, norm_a, w_in_a, w_gla_f2, b_gla_f, gla_out_norm, moba_q_norm, moba_k_norm, w_out_a, norm_c, w_in_c, s5_lambda_re, s5_lambda_im, s5_log_dt, s5_b_re, s5_b_im, s5_c_re, s5_c_im, s5_d, w_glu, b_glu, w_out_c):
    nbp, tp, d = x_prompt.shape
    nbs, ts, _ = x_sample.shape
    depth = norm_a.shape[0] + norm_c.shape[0]
    past_len = page_table.shape[1] * PAGE_SIZE
    assert past_len % MOBA_BLOCK == 0 and past_len // MOBA_BLOCK >= MOBA_TOPK
    assert tp % MOBA_BLOCK == 0 and tp % GLA_CHUNK == 0
    pos_p = jnp.arange(tp)
    pos_s = past_len + jnp.arange(ts)
    xp = x_prompt.reshape(nbp * tp, d)
    xs = x_sample.reshape(nbs * ts, d)
    g = d // S5_GROUP
    nslab = d // S5_SLAB
    gl = S5_SLAB // S5_GROUP
    outs = [[] for _ in range(10)]
    for layer in range(depth):
        i = layer // 2
        if layer % 2 == 0:
            w = w_in_a[i]
            gf_cols = w[:, COL_GG + GLA_VW:COL_GG + GLA_VW + GLA_GATE_RANK]
            wcat = jnp.concatenate(
                [w[:, :COL_GG + GLA_VW], w[:, COL_GG + GLA_VW + GLA_GATE_RANK:],
                 jnp.pad(gf_cols, ((0, 0), (0, A_TN - GLA_GATE_RANK)))], axis=1).astype(BF16)
            weights = (norm_a[i], wcat, w_gla_f2[i], b_gla_f[i], gla_out_norm[i], moba_q_norm[i],
                       moba_k_norm[i], w_out_a[i].astype(BF16))
            zero_state = jnp.zeros((nbp, GLA_HEADS, GLA_DK, GLA_DV), F32)
            xp, mk, mv, sp = _layer_a(xp, pos_p, nbp, tp, zero_state, weights, None)
            outs[0].append(mk)
            outs[1].append(mv)
            outs[4].append(sp)
            xs, mk, mv, ss = _layer_a(xs, pos_s, nbs, ts, state_gla[i], weights,
                                      (cache_k, cache_v, page_table, i))
            outs[2].append(mk)
            outs[3].append(mv)
            outs[5].append(ss)
        else:
            bb_re, bb_im, ab_re, ab_im = _s5_coef(s5_lambda_re[i], s5_lambda_im[i], s5_log_dt[i],
                                                  s5_b_re[i], s5_b_im[i])
            to_in = lambda a: jnp.transpose(a.reshape(nslab, gl, S5_P, S5_GROUP), (0, 1, 3, 2))
            to_out = lambda a: jnp.transpose(a.reshape(nslab, gl, S5_GROUP, S5_P), (0, 1, 3, 2))
            bb = jnp.concatenate([_block_diag(to_in(bb_re)), _block_diag(to_in(bb_im))], axis=2).astype(BF16)
            cc = jnp.concatenate([_block_diag(to_out(s5_c_re[i])), -_block_diag(to_out(s5_c_im[i]))],
                                 axis=1).astype(BF16)
            weights = (norm_c[i], w_in_c[i].astype(BF16), bb, cc, ab_re, ab_im, s5_d[i],
                       w_glu[i].astype(BF16), b_glu[i], w_out_c[i].astype(BF16))
            zeros = jnp.zeros((nbp, g, S5_P), F32)
            xp, xr, xi = _layer_c(xp, nbp, tp, zeros, zeros, weights)
            outs[6].append(xr)
            outs[7].append(xi)
            xs, xr, xi = _layer_c(xs, nbs, ts, state_s5_re[i], state_s5_im[i], weights)
            outs[8].append(xr)
            outs[9].append(xi)
    pk, pv, sk, sv, pg, sg, psr, psi, ssr, ssi = [jnp.stack(o) for o in outs]
    return (xp.reshape(nbp, tp, d), xs.reshape(nbs, ts, d), pk, pv, sk, sv, pg, sg, psr, psi, ssr, ssi)
```

```python
import functools
import math

import jax
import jax.numpy as jnp
from jax import lax
from jax.experimental import pallas as pl
from jax.experimental.pallas import tpu as pltpu

F32 = jnp.float32
BF16 = jnp.bfloat16
I32 = jnp.int32

RMS_EPS = 1e-6
GLA_HEADS = 8
GLA_DK = 128
GLA_DV = 256
GLA_KW = GLA_HEADS * GLA_DK
GLA_VW = GLA_HEADS * GLA_DV
GLA_GATE_RANK = 16
GLA_GATE_TAU = 16.0
GLA_CHUNK = 64
GLA_GROUP_ROWS = 256
GLA_GROUP_UNROLL = 4
MOBA_HEADS = 16
MOBA_DH = 128
MOBA_W = MOBA_HEADS * MOBA_DH
MOBA_BLOCK = 256
MOBA_TOPK = 3
MOBA_PAIR = 2
ROPE_THETA = 10000.0
PAGE_SIZE = 128
BMEAN_BLOCKS_PER_STEP = 4
S5_GROUP = 16
S5_P = 64
S5_SLAB = 256
S5_SLAB_STATES = (S5_SLAB // S5_GROUP) * S5_P

LANES = 128
SUBLANES = 8
VMEM_LIMIT = 52 * 1024 * 1024
NEG_INF = float("-inf")

COL_GQ = 0
COL_GK = COL_GQ + GLA_KW
COL_GV = COL_GK + GLA_KW
COL_GG = COL_GV + GLA_VW
COL_MQ = COL_GG + GLA_VW
COL_MK = COL_MQ + MOBA_W
COL_MV = COL_MK + MOBA_W
COL_MG = COL_MV + MOBA_W
COL_GF = COL_MG + MOBA_W
A_TN = 512
A_COLS = COL_GF + A_TN


def _params(*sem):
    return pltpu.CompilerParams(dimension_semantics=sem, vmem_limit_bytes=VMEM_LIMIT)


def _nt(a, b):
    return lax.dot_general(a, b, (((1,), (1,)), ((), ())), preferred_element_type=F32)


def _tn(a, b):
    return lax.dot_general(a, b, (((0,), (0,)), ((), ())), preferred_element_type=F32)


def _silu(x):
    return x / (1.0 + jnp.exp(-x))


def _log_sigmoid(x):
    return jnp.minimum(x, 0.0) - jnp.log(1.0 + jnp.exp(-jnp.abs(x)))


def _rms_rows(x, g):
    ms = jnp.mean(x * x, axis=-1, keepdims=True)
    return x * lax.rsqrt(ms + RMS_EPS) * g


def _aproj_body(x_ref, g_ref, w_ref, qn_ref, kn_ref, cos_ref, sin_ref, o_ref, h_ref, *, tn):
    j = pl.program_id(1)

    @pl.when(j == 0)
    def _():
        h_ref[...] = _rms_rows(x_ref[...], g_ref[...]).astype(BF16)

    acc = jnp.dot(h_ref[...], w_ref[...], preferred_element_type=F32)
    q_lo, k_lo, k_hi = COL_MQ // tn, COL_MK // tn, COL_MV // tn
    is_qk = jnp.logical_and(j >= q_lo, j < k_hi)

    @pl.when(is_qk)
    def _():
        gain = jnp.where(j < k_lo, qn_ref[...], kn_ref[...])
        cos = cos_ref[...]
        sin = sin_ref[...]
        for hh in range(tn // MOBA_DH):
            y = _rms_rows(acc[:, hh * MOBA_DH:(hh + 1) * MOBA_DH], gain)
            o_ref[:, hh * MOBA_DH:(hh + 1) * MOBA_DH] = y * cos + pltpu.roll(y, MOBA_DH // 2, 1) * sin

    @pl.when(jnp.logical_not(is_qk))
    def _():
        o_ref[...] = acc


def _aproj(x, g, wcat, qn, kn, cos, sin, *, tm):
    m, d = x.shape
    n = wcat.shape[1]
    tn = A_TN
    return pl.pallas_call(
        functools.partial(_aproj_body, tn=tn),
        grid=(m // tm, n // tn),
        in_specs=[
            pl.BlockSpec((tm, d), lambda i, j: (i, 0)),
            pl.BlockSpec((1, d), lambda i, j: (0, 0)),
            pl.BlockSpec((d, tn), lambda i, j: (0, j)),
            pl.BlockSpec((1, MOBA_DH), lambda i, j: (0, 0)),
            pl.BlockSpec((1, MOBA_DH), lambda i, j: (0, 0)),
            pl.BlockSpec((tm, MOBA_DH), lambda i, j: (i, 0)),
            pl.BlockSpec((tm, MOBA_DH), lambda i, j: (i, 0)),
        ],
        out_specs=pl.BlockSpec((tm, tn), lambda i, j: (i, j)),
        out_shape=jax.ShapeDtypeStruct((m, n), F32),
        scratch_shapes=[pltpu.VMEM((tm, d), BF16)],
        compiler_params=_params("parallel", "arbitrary"),
        name="a_proj",
    )(x, g.reshape(1, d), wcat, qn.reshape(1, -1), kn.reshape(1, -1), cos, sin)


def _gemm_norm_body(x_ref, g_ref, w_ref, o_ref, h_ref):
    @pl.when(pl.program_id(1) == 0)
    def _():
        h_ref[...] = _rms_rows(x_ref[...], g_ref[...]).astype(BF16)

    o_ref[...] = jnp.dot(h_ref[...], w_ref[...], preferred_element_type=F32)


def _gemm_norm(x, g, w, *, tm, tn):
    m, d = x.shape
    n = w.shape[1]
    return pl.pallas_call(
        _gemm_norm_body,
        grid=(m // tm, n // tn),
        in_specs=[
            pl.BlockSpec((tm, d), lambda i, j: (i, 0)),
            pl.BlockSpec((1, d), lambda i, j: (0, 0)),
            pl.BlockSpec((d, tn), lambda i, j: (0, j)),
        ],
        out_specs=pl.BlockSpec((tm, tn), lambda i, j: (i, j)),
        out_shape=jax.ShapeDtypeStruct((m, n), F32),
        scratch_shapes=[pltpu.VMEM((tm, d), BF16)],
        compiler_params=_params("parallel", "arbitrary"),
        name="c_proj",
    )(x, g.reshape(1, d), w)


def _gemm_res_body(a_ref, w_ref, r_ref, o_ref):
    o_ref[...] = r_ref[...] + jnp.dot(a_ref[...], w_ref[...], preferred_element_type=F32)


def _gemm_res(a, w, res, *, tm, tn):
    m, k = a.shape
    n = w.shape[1]
    return pl.pallas_call(
        _gemm_res_body,
        grid=(m // tm, n // tn),
        in_specs=[
            pl.BlockSpec((tm, k), lambda i, j: (i, 0)),
            pl.BlockSpec((k, tn), lambda i, j: (0, j)),
            pl.BlockSpec((tm, tn), lambda i, j: (i, j)),
        ],
        out_specs=pl.BlockSpec((tm, tn), lambda i, j: (i, j)),
        out_shape=jax.ShapeDtypeStruct((m, n), F32),
        compiler_params=_params("parallel", "arbitrary"),
        name="out_proj",
    )(a, w, res)


def _gemm_glu_body(a_ref, w_ref, y_ref, z_ref, b_ref, o_ref, h_ref):
    @pl.when(pl.program_id(1) == 0)
    def _():
        h_ref[...] = a_ref[...].astype(BF16)

    t = jnp.dot(h_ref[...], w_ref[...], preferred_element_type=F32) + b_ref[...]
    y = y_ref[...]
    o_ref[...] = ((y / (1.0 + jnp.exp(-t))) * _silu(z_ref[...])).astype(o_ref.dtype)


def _gemm_glu(y, w, uz, bias, *, tm, tn):
    m, k = y.shape
    n = w.shape[1]
    zoff = n // tn
    return pl.pallas_call(
        _gemm_glu_body,
        grid=(m // tm, n // tn),
        in_specs=[
            pl.BlockSpec((tm, k), lambda i, j: (i, 0)),
            pl.BlockSpec((k, tn), lambda i, j: (0, j)),
            pl.BlockSpec((tm, tn), lambda i, j: (i, j)),
            pl.BlockSpec((tm, tn), lambda i, j: (i, zoff + j)),
            pl.BlockSpec((1, tn), lambda i, j: (0, j)),
        ],
        out_specs=pl.BlockSpec((tm, tn), lambda i, j: (i, j)),
        out_shape=jax.ShapeDtypeStruct((m, n), BF16),
        scratch_shapes=[pltpu.VMEM((tm, k), BF16)],
        compiler_params=_params("parallel", "arbitrary"),
        name="glu_proj",
    )(y, w, y, uz, bias.reshape(1, n))


def _gla_body(q_ref, k_ref, v_ref, gf_ref, wf_ref, bf_ref, s0_ref, o_ref, s_ref,
              qd_s, u_s, dec_s, st_s, *, chunk):
    c = chunk
    t = q_ref.shape[0]
    n = t // c
    grp = GLA_GROUP_ROWS
    cpg = grp // c
    shift = c.bit_length() - 1
    assert 1 << shift == c and t % grp == 0
    scale = GLA_DK ** -0.5
    r = lax.broadcasted_iota(I32, (grp, grp), 0)
    cc = lax.broadcasted_iota(I32, (grp, grp), 1)
    same = jnp.right_shift(r, shift) == jnp.right_shift(cc, shift)
    causal = jnp.logical_and(same, cc <= r)
    causal_b = causal.astype(BF16)
    wf = wf_ref[...].astype(BF16)
    bias = bf_ref[...]

    def group(i, carry):
        rows = pl.ds(pl.multiple_of(i * grp, grp), grp)
        gf = gf_ref[rows, :][:, :GLA_GATE_RANK]
        pre = jnp.dot(gf.astype(BF16), wf, preferred_element_type=F32) + bias
        g = _log_sigmoid(pre) / GLA_GATE_TAU
        g_hi = g.astype(BF16)
        g_md = (g - g_hi.astype(F32)).astype(BF16)
        g_lo = (g - g_hi.astype(F32) - g_md.astype(F32)).astype(BF16)
        b3 = jnp.dot(causal_b, jnp.concatenate([g_hi, g_md, g_lo], axis=1), preferred_element_type=F32)
        b = b3[:, :GLA_DK] + b3[:, GLA_DK:2 * GLA_DK] + b3[:, 2 * GLA_DK:]
        row_of = lambda r0: jnp.concatenate(
            [jnp.broadcast_to(b[j * c + r0:j * c + r0 + 1, :], (c, GLA_DK)) for j in range(cpg)], axis=0)
        bm = row_of(c // 2 - 1)
        be = row_of(c - 1)
        q = q_ref[rows, :] * scale
        k = k_ref[rows, :]
        v = v_ref[rows, :].astype(BF16)
        qe = (q * jnp.exp(b - bm)).astype(BF16)
        ke = (k * jnp.exp(bm - b)).astype(BF16)
        att = jnp.where(causal, _nt(qe, ke), 0.0)
        o_ref[rows, :] = jnp.dot(att.astype(BF16), v, preferred_element_type=F32)
        qd_s[rows, :] = (q * jnp.exp(b)).astype(BF16)
        kd = (k * jnp.exp(be - b)).astype(BF16)
        dec = jnp.exp(be)
        for j in range(cpg):
            cj = i * cpg + j
            u_s[cj] = _tn(v[j * c:(j + 1) * c, :], kd[j * c:(j + 1) * c, :])
            dec_s[cj] = dec[j * c:j * c + SUBLANES, :]
        return carry

    lax.fori_loop(0, t // grp, group, 0, unroll=GLA_GROUP_UNROLL)

    def advance(cj, st):
        st_s[cj] = st.astype(BF16)
        return st * dec_s[cj][0:1, :] + u_s[cj]

    st = lax.fori_loop(0, n, advance, s0_ref[...].T, unroll=4)
    s_ref[...] = st.T

    def inter(cj, carry):
        rows = pl.ds(pl.multiple_of(cj * c, c), c)
        o_ref[rows, :] = o_ref[rows, :] + _nt(qd_s[rows, :], st_s[cj])
        return carry

    lax.fori_loop(0, n, inter, 0, unroll=4)


def _gla_small_body(q_ref, k_ref, v_ref, gf_ref, wf_ref, bf_ref, s0_ref, o_ref, s_ref):
    c = q_ref.shape[0]
    scale = GLA_DK ** -0.5
    q = q_ref[...] * scale
    k = k_ref[...]
    v = v_ref[...]
    pre = bf_ref[...] + jnp.zeros((c, GLA_DK), F32)
    for r in range(GLA_GATE_RANK):
        pre = pre + gf_ref[:, r:r + 1] * wf_ref[r:r + 1, :]
    g = _log_sigmoid(pre) / GLA_GATE_TAU
    row = lax.broadcasted_iota(I32, (c, GLA_DK), 0)
    b = jnp.zeros((c, GLA_DK), F32)
    for s in range(c):
        b = b + jnp.where(row >= s, g[s:s + 1, :], 0.0)
    be = b[c - 1:c, :]
    s0 = s0_ref[...]
    o = jnp.dot(q * jnp.exp(b), s0, preferred_element_type=F32)
    for s in range(c):
        e = jnp.exp(jnp.where(row >= s, b - b[s:s + 1, :], NEG_INF))
        a_col = jnp.sum(q * k[s:s + 1, :] * e, axis=-1, keepdims=True)
        o = o + a_col * v[s:s + 1, :]
    o_ref[...] = o
    kd = k * jnp.exp(be - b)
    pad = jnp.concatenate([kd, jnp.exp(be), jnp.zeros((LANES - c - 1, GLA_DK), F32)], axis=0)
    padt = pad.T
    s_new = s0 * padt[:, c:c + 1]
    for s in range(c):
        s_new = s_new + padt[:, s:s + 1] * v[s:s + 1, :]
    s_ref[...] = s_new


def _gla(p, w_f2, b_f, s0, *, nb, t):
    small = t % GLA_CHUNK != 0
    kq, kk, kv = COL_GQ // GLA_DK, COL_GK // GLA_DK, COL_GV // GLA_DV
    kf = COL_GF // LANES
    in_specs = [
        pl.BlockSpec((t, GLA_DK), lambda b, h: (b, kq + h)),
        pl.BlockSpec((t, GLA_DK), lambda b, h: (b, kk + h)),
        pl.BlockSpec((t, GLA_DV), lambda b, h: (b, kv + h)),
        pl.BlockSpec((t, LANES), lambda b, h: (b, kf)),
        pl.BlockSpec((GLA_GATE_RANK, GLA_DK), lambda b, h: (0, h)),
        pl.BlockSpec((1, GLA_DK), lambda b, h: (0, h)),
        pl.BlockSpec((None, None, GLA_DK, GLA_DV), lambda b, h: (b, h, 0, 0)),
    ]
    out_specs = [
        pl.BlockSpec((t, GLA_DV), lambda b, h: (b, h)),
        pl.BlockSpec((None, None, GLA_DK, GLA_DV), lambda b, h: (b, h, 0, 0)),
    ]
    out_shape = [
        jax.ShapeDtypeStruct((nb * t, GLA_VW), F32),
        jax.ShapeDtypeStruct((nb, GLA_HEADS, GLA_DK, GLA_DV), F32),
    ]
    if small:
        body, scratch = _gla_small_body, []
    else:
        body = functools.partial(_gla_body, chunk=GLA_CHUNK)
        n = t // GLA_CHUNK
        scratch = [
            pltpu.VMEM((t, GLA_DK), BF16),
            pltpu.VMEM((n, GLA_DV, GLA_DK), F32),
            pltpu.VMEM((n, SUBLANES, GLA_DK), F32),
            pltpu.VMEM((n, GLA_DV, GLA_DK), BF16),
        ]
    return pl.pallas_call(
        body,
        grid=(nb, GLA_HEADS),
        in_specs=in_specs,
        out_specs=out_specs,
        out_shape=out_shape,
        scratch_shapes=scratch,
        compiler_params=_params("parallel", "parallel"),
        name="gla_small" if small else "gla",
    )(p, p, p, p, w_f2, b_f.reshape(1, -1), s0)


def _moba_prompt_body(q_ref, k_ref, v_ref, o_ref, *, nblk):
    blk = MOBA_BLOCK
    t = nblk * blk
    shift = blk.bit_length() - 1
    assert 1 << shift == blk and nblk <= LANES and nblk % MOBA_PAIR == 0
    scale = MOBA_DH ** -0.5
    q = q_ref[...]
    k = k_ref[...]
    qb = (q * scale).astype(BF16)
    kb = k.astype(BF16)
    vb = v_ref[...].astype(BF16)

    km = jnp.concatenate([jnp.mean(k[n * blk:(n + 1) * blk, :], axis=0, keepdims=True) for n in range(nblk)],
                         axis=0)
    gate = lax.dot_general(km, q, (((1,), (1,)), ((), ())),
                           precision=lax.Precision.HIGHEST, preferred_element_type=F32)
    nrow = lax.broadcasted_iota(I32, (nblk, t), 0)
    qblk = jnp.right_shift(lax.broadcasted_iota(I32, (nblk, t), 1), shift)
    valid = nrow < qblk
    gm = jnp.where(valid, gate, NEG_INF)
    rank = jnp.zeros((nblk, t), I32)
    for m in range(nblk):
        g_m = gm[m:m + 1, :]
        beats = jnp.logical_or(g_m > gm, jnp.logical_and(g_m == gm, m < nrow))
        rank = rank + beats.astype(I32)
    sel = jnp.logical_and(valid, rank < MOBA_TOPK).astype(F32)
    sel_c = jnp.concatenate([sel, jnp.zeros((LANES - nblk, t), F32)], axis=0).T
    bias_c = jnp.where(sel_c > 0.0, 0.0, NEG_INF)

    causal = lax.broadcasted_iota(I32, (blk, blk), 1) <= lax.broadcasted_iota(I32, (blk, blk), 0)
    masked = jnp.full((blk, blk), NEG_INF, F32)
    pair = MOBA_PAIR
    for v in range(nblk // pair):
        lo = pair * v * blk
        nbi = pair * (v + 1)
        s = _nt(qb[lo:lo + pair * blk, :], kb[0:nbi * blk, :])
        bias = bias_c[lo:lo + pair * blk, :]
        pieces = []
        for n in range(nbi):
            sn = s[:, n * blk:(n + 1) * blk]
            if n < pair * v:
                pieces.append(sn + bias[:, n:n + 1])
                continue
            j = n - pair * v
            parts = []
            for i in range(pair):
                sni = sn[i * blk:(i + 1) * blk, :]
                if i == j:
                    parts.append(jnp.where(causal, sni, NEG_INF))
                elif i > j:
                    parts.append(sni + bias[i * blk:(i + 1) * blk, n:n + 1])
                else:
                    parts.append(masked)
            pieces.append(jnp.concatenate(parts, axis=0))
        s = jnp.concatenate(pieces, axis=1)
        m = jnp.max(s, axis=-1, keepdims=True)
        p = jnp.exp(s - m)
        l = jnp.sum(p, axis=-1, keepdims=True)
        acc = jnp.dot(p.astype(BF16), vb[0:nbi * blk, :], preferred_element_type=F32)
        o_ref[lo:lo + pair * blk, :] = acc / l


def _moba_prompt(p, *, nb, t):
    nblk = t // MOBA_BLOCK
    kq, kk, kv = COL_MQ // MOBA_DH, COL_MK // MOBA_DH, COL_MV // MOBA_DH
    return pl.pallas_call(
        functools.partial(_moba_prompt_body, nblk=nblk),
        grid=(nb, MOBA_HEADS),
        in_specs=[
            pl.BlockSpec((t, MOBA_DH), lambda b, h: (b, kq + h)),
            pl.BlockSpec((t, MOBA_DH), lambda b, h: (b, kk + h)),
            pl.BlockSpec((t, MOBA_DH), lambda b, h: (b, kv + h)),
        ],
        out_specs=pl.BlockSpec((t, MOBA_DH), lambda b, h: (b, h)),
        out_shape=jax.ShapeDtypeStruct((nb * t, MOBA_W), F32),
        compiler_params=_params("parallel", "parallel"),
        name="moba_prompt",
    )(p, p, p)


def _bmean_body(pt_ref, *refs, ppb):
    pages, o_ref = refs[:-1], refs[-1]
    for j in range(len(pages) // ppb):
        tot = jnp.sum(pages[j * ppb][...], axis=0)
        for r in pages[j * ppb + 1:(j + 1) * ppb]:
            tot = tot + jnp.sum(r[...], axis=0)
        o_ref[j] = tot * (1.0 / MOBA_BLOCK)


def _block_means(cache_k, page_table, layer):
    nb, n_pages = page_table.shape
    ppb = MOBA_BLOCK // PAGE_SIZE
    n_full = n_pages // ppb
    bps = BMEAN_BLOCKS_PER_STEP
    assert n_full % bps == 0
    pps = bps * ppb
    page_block = (None, None, PAGE_SIZE, MOBA_HEADS, MOBA_DH)
    page_spec = lambda j: pl.BlockSpec(page_block, lambda b, n, pt: (layer, pt[b, pps * n + j], 0, 0, 0))
    return pl.pallas_call(
        functools.partial(_bmean_body, ppb=ppb),
        grid_spec=pltpu.PrefetchScalarGridSpec(
            num_scalar_prefetch=1,
            grid=(nb, n_full // bps),
            in_specs=[page_spec(j) for j in range(pps)],
            out_specs=pl.BlockSpec((None, bps, MOBA_HEADS, MOBA_DH), lambda b, n, pt: (b, n, 0, 0)),
        ),
        out_shape=jax.ShapeDtypeStruct((nb, n_full, MOBA_HEADS, MOBA_DH), F32),
        compiler_params=_params("parallel", "arbitrary"),
        name="moba_block_means",
    )(page_table, *([cache_k] * pps))


def _gate_topk_body(q_ref, bm_ref, o_ref, *, td):
    n_full = bm_ref.shape[1]
    lane = lax.broadcasted_iota(I32, (n_full, LANES), 1)
    row = lax.broadcasted_iota(I32, (n_full, LANES), 0)
    orow = lax.broadcasted_iota(I32, (SUBLANES, LANES), 0)
    for h in range(MOBA_HEADS):
        km = bm_ref[h]
        g = jnp.full((n_full, LANES), NEG_INF, F32)
        for t in range(td):
            qv = q_ref[t:t + 1, h * MOBA_DH:(h + 1) * MOBA_DH]
            g = jnp.where(lane == t, jnp.sum(km * qv, axis=-1, keepdims=True), g)
        outv = jnp.zeros((SUBLANES, LANES), I32)
        for j in range(MOBA_TOPK):
            mx = jnp.max(g, axis=0, keepdims=True)
            idx = jnp.min(jnp.where(g == mx, row, n_full), axis=0, keepdims=True)
            g = jnp.where(row == idx, NEG_INF, g)
            outv = jnp.where(orow == j, idx, outv)
        o_ref[h] = outv


def _gate_topk(p, bmean_t, *, nb, td):
    n_full = bmean_t.shape[2]
    assert td <= SUBLANES and MOBA_TOPK <= SUBLANES
    kq = COL_MQ // MOBA_W
    return pl.pallas_call(
        functools.partial(_gate_topk_body, td=td),
        grid=(nb,),
        in_specs=[
            pl.BlockSpec((td, MOBA_W), lambda b: (b, kq)),
            pl.BlockSpec((None, MOBA_HEADS, n_full, MOBA_DH), lambda b: (b, 0, 0, 0)),
        ],
        out_specs=pl.BlockSpec((None, MOBA_HEADS, SUBLANES, LANES), lambda b: (b, 0, 0, 0)),
        out_shape=jax.ShapeDtypeStruct((nb, MOBA_HEADS, SUBLANES, LANES), I32),
        compiler_params=_params("parallel"),
        name="moba_gate_topk",
    )(p, bmean_t)


def _moba_sample_body(idx_ref, pt_ref, q_ref, k_ref, v_ref, ck_ref, cv_ref, o_ref, kbuf, vbuf, sem, *, layer, td):
    step = pl.program_id(0)
    nsteps = pl.num_programs(0)
    slot = lax.rem(step, 2)
    ppb = MOBA_BLOCK // PAGE_SIZE
    per_q = MOBA_TOPK * ppb
    nsel = MOBA_TOPK * MOBA_BLOCK
    scale = MOBA_DH ** -0.5
    slots = [(t, j, pg) for t in range(td) for j in range(MOBA_TOPK) for pg in range(ppb)]

    def copies(st, sl, t, j, pg):
        b = lax.div(st, MOBA_HEADS)
        h = lax.rem(st, MOBA_HEADS)
        blk = idx_ref[(st * td + t) * MOBA_TOPK + j]
        page = pt_ref[b, blk * ppb + pg]
        dst = pl.ds((t * per_q + j * ppb + pg) * PAGE_SIZE, PAGE_SIZE)
        return (pltpu.make_async_copy(ck_ref.at[layer, page, :, h, :], kbuf.at[sl, dst, :], sem.at[sl, 0]),
                pltpu.make_async_copy(cv_ref.at[layer, page, :, h, :], vbuf.at[sl, dst, :], sem.at[sl, 1]))

    def issue(st, sl):
        for c in slots:
            ck, cv = copies(st, sl, *c)
            ck.start()
            cv.start()

    @pl.when(step == 0)
    def _():
        issue(step, slot)

    @pl.when(step + 1 < nsteps)
    def _():
        issue(step + 1, 1 - slot)

    for c in slots:
        ck, cv = copies(step, slot, *c)
        ck.wait()
        cv.wait()

    rows = 2 * SUBLANES
    q = q_ref[...]
    qp = jnp.concatenate([q, jnp.zeros((rows - td, MOBA_DH), F32)], axis=0).astype(BF16)
    s_sel = _nt(qp, kbuf[slot].astype(BF16)) * scale
    r = lax.broadcasted_iota(I32, (rows, td * nsel), 0)
    c = lax.broadcasted_iota(I32, (rows, td * nsel), 1)
    mine = jnp.logical_and(c >= r * nsel, c < (r + 1) * nsel)
    s_sel = jnp.where(mine, s_sel, NEG_INF)

    k_new = k_ref[...]
    v_new = v_ref[...]
    rn = lax.broadcasted_iota(I32, (td, LANES), 0)
    cn = lax.broadcasted_iota(I32, (td, LANES), 1)
    s_new = jnp.full((td, LANES), NEG_INF, F32)
    for t in range(td):
        col = jnp.sum(q * k_new[t:t + 1, :], axis=-1, keepdims=True) * scale
        s_new = jnp.where(jnp.logical_and(cn == t, rn >= t), col, s_new)

    s_sel = s_sel[:td, :]
    m = jnp.maximum(jnp.max(s_sel, axis=-1, keepdims=True), jnp.max(s_new, axis=-1, keepdims=True))
    p_sel = jnp.exp(s_sel - m)
    p_new = jnp.exp(s_new - m)
    l = jnp.sum(p_sel, axis=-1, keepdims=True) + jnp.sum(p_new, axis=-1, keepdims=True)
    pp = jnp.concatenate([p_sel, jnp.zeros((rows - td, td * nsel), F32)], axis=0).astype(BF16)
    acc = jnp.dot(pp, vbuf[slot].astype(BF16), preferred_element_type=F32)[:td, :]
    for t in range(td):
        acc = acc + p_new[:, t:t + 1] * v_new[t:t + 1, :]
    o_ref[...] = acc / l


def _moba_sample(p, cache_k, cache_v, page_table, idx_flat, *, layer, nb, td):
    kq, kk, kv = COL_MQ // MOBA_DH, COL_MK // MOBA_DH, COL_MV // MOBA_DH
    nrows = td * MOBA_TOPK * MOBA_BLOCK
    row_spec = lambda off: pl.BlockSpec(
        (td, MOBA_DH), lambda s, idx, pt: (lax.div(s, MOBA_HEADS), off + lax.rem(s, MOBA_HEADS)))
    return pl.pallas_call(
        functools.partial(_moba_sample_body, layer=layer, td=td),
        grid_spec=pltpu.PrefetchScalarGridSpec(
            num_scalar_prefetch=2,
            grid=(nb * MOBA_HEADS,),
            in_specs=[
                row_spec(kq), row_spec(kk), row_spec(kv),
                pl.BlockSpec(memory_space=pl.ANY),
                pl.BlockSpec(memory_space=pl.ANY),
            ],
            out_specs=row_spec(0),
            scratch_shapes=[
                pltpu.VMEM((2, nrows, MOBA_DH), F32),
                pltpu.VMEM((2, nrows, MOBA_DH), F32),
                pltpu.SemaphoreType.DMA((2, 2)),
            ],
        ),
        out_shape=jax.ShapeDtypeStruct((nb * td, MOBA_W), F32),
        compiler_params=_params("arbitrary"),
        name="moba_sample",
    )(idx_flat, page_table, p, p, p, cache_k, cache_v)


def _merge_body(go_ref, gg_ref, mo_ref, mg_ref, on_ref, a_ref):
    on = on_ref[...]
    for h in range(GLA_HEADS):
        sl = slice(h * GLA_DV, (h + 1) * GLA_DV)
        g = _rms_rows(go_ref[:, sl], on) * _silu(gg_ref[:, sl])
        a_ref[:, sl] = g.astype(a_ref.dtype)
    a_ref[:, GLA_VW:] = (mo_ref[...] * _silu(mg_ref[...])).astype(a_ref.dtype)


def _merge(gla_o, moba_o, p, o_norm, *, tm):
    m = gla_o.shape[0]
    kg, km = COL_GG // GLA_VW, COL_MG // MOBA_W
    return pl.pallas_call(
        _merge_body,
        grid=(m // tm,),
        in_specs=[
            pl.BlockSpec((tm, GLA_VW), lambda i: (i, 0)),
            pl.BlockSpec((tm, GLA_VW), lambda i: (i, kg)),
            pl.BlockSpec((tm, MOBA_W), lambda i: (i, 0)),
            pl.BlockSpec((tm, MOBA_W), lambda i: (i, km)),
            pl.BlockSpec((1, GLA_DV), lambda i: (0, 0)),
        ],
        out_specs=pl.BlockSpec((tm, GLA_VW + MOBA_W), lambda i: (i, 0)),
        out_shape=jax.ShapeDtypeStruct((m, GLA_VW + MOBA_W), BF16),
        compiler_params=_params("parallel"),
        name="a_merge",
    )(gla_o, p, moba_o, p, o_norm.reshape(1, -1))


def _s5_coef_body(lr_ref, li_ref, ldt_ref, br_ref, bi_ref, bbr_ref, bbi_ref, abr_ref, abi_ref):
    lr = lr_ref[...]
    li = li_ref[...]
    dt = jnp.exp(ldt_ref[...])
    mag = jnp.exp(lr * dt)
    ab_re = mag * jnp.cos(li * dt)
    ab_im = mag * jnp.sin(li * dt)
    den = lr * lr + li * li
    nr = ab_re - 1.0
    cr = (nr * lr + ab_im * li) / den
    ci = (ab_im * lr - nr * li) / den
    br = br_ref[...]
    bi = bi_ref[...]
    bbr_ref[...] = cr * br - ci * bi
    bbi_ref[...] = cr * bi + ci * br
    abr_ref[...] = ab_re
    abi_ref[...] = ab_im


def _s5_coef(lam_re, lam_im, log_dt, b_re, b_im):
    g, p = lam_re.shape
    w = p * S5_GROUP
    rep = lambda a: jnp.repeat(a, S5_GROUP, axis=1)
    full = pl.BlockSpec((g, w), lambda: (0, 0))
    outs = pl.pallas_call(
        _s5_coef_body,
        in_specs=[full] * 5,
        out_specs=[full] * 4,
        out_shape=[jax.ShapeDtypeStruct((g, w), F32)] * 4,
        name="s5_coef",
    )(rep(lam_re), rep(lam_im), jnp.broadcast_to(log_dt[:, None], (g, w)),
      b_re.reshape(g, w), b_im.reshape(g, w))
    bbr, bbi, abr, abi = outs
    return (bbr.reshape(g, p, S5_GROUP), bbi.reshape(g, p, S5_GROUP),
            abr[:, ::S5_GROUP], abi[:, ::S5_GROUP])


def _s5_expand(w2):
    per_tile = LANES // S5_P
    rg = jnp.right_shift(lax.broadcasted_iota(I32, (S5_SLAB, LANES), 0), S5_GROUP.bit_length() - 1)
    lg = jnp.right_shift(lax.broadcasted_iota(I32, (S5_SLAB, LANES), 1), S5_P.bit_length() - 1)
    pieces = []
    for part in range(2):
        w = w2[:, part * LANES:(part + 1) * LANES]
        for kk in range(S5_SLAB_STATES // LANES):
            pieces.append(jnp.where(rg == per_tile * kk + lg, w, 0.0))
    return jnp.concatenate(pieces, axis=1).astype(BF16)


def _s5_body(u_ref, bb_ref, cc_ref, ar_ref, ai_ref, d_ref, x0r_ref, x0i_ref,
             y_ref, xr_ref, xi_ref, lhs, bu, ybuf, cr, ci, bd_in, bd_out, *, ns, nt):
    ti = pl.program_id(1)
    tpv = SUBLANES // ns
    nst = S5_SLAB_STATES
    lo_rows = SUBLANES - ns
    nlb = S5_SLAB // LANES

    @pl.when(ti == 0)
    def _():
        cr[...] = jnp.zeros_like(cr)
        ci[...] = jnp.zeros_like(ci)
        cr[lo_rows:, :] = x0r_ref[...]
        ci[lo_rows:, :] = x0i_ref[...]
        bd_in[...] = _s5_expand(bb_ref[...])
        bd_out[...] = _s5_expand(cc_ref[...])

    for b in range(ns):
        for j in range(nlb):
            lhs[j, pl.ds(b, nt, stride=ns), :] = u_ref[b, :, j * LANES:(j + 1) * LANES]
    u_rows = jnp.concatenate([lhs[j] for j in range(nlb)], axis=1)
    bu[...] = jnp.dot(u_rows.astype(BF16), bd_in[...], preferred_element_type=F32)

    ar = ar_ref[...]
    ai = ai_ref[...]
    if tpv == 1:
        def step(v, carry):
            xr, xi = carry
            sl = pl.ds(pl.multiple_of(v * SUBLANES, SUBLANES), SUBLANES)
            nxr = ar * xr - ai * xi + bu[sl, :nst]
            nxi = ar * xi + ai * xr + bu[sl, nst:]
            bu[sl, :nst] = nxr
            bu[sl, nst:] = nxi
            return nxr, nxi
    else:
        assert tpv == 2
        lo = lax.broadcasted_iota(I32, (SUBLANES, nst), 0) < ns
        a2r = ar * ar - ai * ai
        a2i = 2.0 * ar * ai
        c1r = jnp.where(lo, 0.0, ar)
        c1i = jnp.where(lo, 0.0, ai)
        c2r = jnp.where(lo, ar, a2r)
        c2i = jnp.where(lo, ai, a2i)

        def step(v, carry):
            xr, xi = carry
            sl = pl.ds(pl.multiple_of(v * SUBLANES, SUBLANES), SUBLANES)
            br = bu[sl, :nst]
            bi = bu[sl, nst:]
            sbr = pltpu.roll(br, ns, 0)
            sbi = pltpu.roll(bi, ns, 0)
            pr = jnp.where(lo, pltpu.roll(xr, ns, 0), xr)
            pi = jnp.where(lo, pltpu.roll(xi, ns, 0), xi)
            nxr = br + c1r * sbr - c1i * sbi + c2r * pr - c2i * pi
            nxi = bi + c1r * sbi + c1i * sbr + c2r * pi + c2i * pr
            bu[sl, :nst] = nxr
            bu[sl, nst:] = nxi
            return nxr, nxi

    nv = (nt * ns) // SUBLANES
    xr, xi = lax.fori_loop(0, nv, step, (cr[...], ci[...]), unroll=2 if nv % 2 == 0 else 1)
    cr[...] = xr
    ci[...] = xi

    yv = _nt(bu[...].astype(BF16), bd_out[...]) + d_ref[...] * u_rows
    yv = 0.5 * yv * (1.0 + lax.erf(yv * math.sqrt(0.5)))
    for j in range(nlb):
        ybuf[j] = yv[:, j * LANES:(j + 1) * LANES]
    for b in range(ns):
        for j in range(nlb):
            y_ref[b, :, j * LANES:(j + 1) * LANES] = ybuf[j, pl.ds(b, nt, stride=ns), :]

    @pl.when(ti == pl.num_programs(1) - 1)
    def _():
        xr_ref[...] = cr[lo_rows:, :]
        xi_ref[...] = ci[lo_rows:, :]


def _s5_core(uz, bb, cc, ab_re, ab_im, d, x0_re, x0_im, *, ns, t, nt):
    w = d.shape[0]
    nslab = w // S5_SLAB
    nst = S5_SLAB_STATES
    assert SUBLANES % ns == 0 and t % nt == 0 and (nt * ns) % SUBLANES == 0
    u3 = uz.reshape(ns, t, uz.shape[1])
    to_slab = lambda a: jnp.transpose(a.reshape(ns, nslab, nst), (1, 0, 2))
    st_spec = pl.BlockSpec((None, ns, nst), lambda s, i: (s, 0, 0))
    y, xr, xi = pl.pallas_call(
        functools.partial(_s5_body, ns=ns, nt=nt),
        grid=(nslab, t // nt),
        in_specs=[
            pl.BlockSpec((ns, nt, S5_SLAB), lambda s, i: (0, i, s)),
            pl.BlockSpec((None, S5_SLAB, 2 * LANES), lambda s, i: (s, 0, 0)),
            pl.BlockSpec((None, S5_SLAB, 2 * LANES), lambda s, i: (s, 0, 0)),
            pl.BlockSpec((None, 1, nst), lambda s, i: (s, 0, 0)),
            pl.BlockSpec((None, 1, nst), lambda s, i: (s, 0, 0)),
            pl.BlockSpec((1, S5_SLAB), lambda s, i: (0, s)),
            st_spec, st_spec,
        ],
        out_specs=[
            pl.BlockSpec((ns, nt, S5_SLAB), lambda s, i: (0, i, s)),
            st_spec, st_spec,
        ],
        out_shape=[
            jax.ShapeDtypeStruct((ns, t, w), F32),
            jax.ShapeDtypeStruct((nslab, ns, nst), F32),
            jax.ShapeDtypeStruct((nslab, ns, nst), F32),
        ],
        scratch_shapes=[
            pltpu.VMEM((S5_SLAB // LANES, nt * ns, LANES), F32),
            pltpu.VMEM((nt * ns, 2 * nst), F32),
            pltpu.VMEM((S5_SLAB // LANES, nt * ns, LANES), F32),
            pltpu.VMEM((SUBLANES, nst), F32),
            pltpu.VMEM((SUBLANES, nst), F32),
            pltpu.VMEM((S5_SLAB, 2 * nst), BF16),
            pltpu.VMEM((S5_SLAB, 2 * nst), BF16),
        ],
        compiler_params=_params("parallel", "arbitrary"),
        name="s5_core",
    )(u3, bb, cc, ab_re.reshape(nslab, 1, nst), ab_im.reshape(nslab, 1, nst), d.reshape(1, w),
      to_slab(x0_re), to_slab(x0_im))
    g = w // S5_GROUP
    from_slab = lambda a: jnp.transpose(a, (1, 0, 2)).reshape(ns, g, S5_P)
    return y.reshape(ns * t, w), from_slab(xr), from_slab(xi)


def _s5_compact(re, im):
    reps = LANES // S5_P
    flat = lambda a: jnp.tile(a.reshape(-1, S5_SLAB, S5_P), (1, 1, reps))
    return jnp.concatenate([flat(re), flat(im)], axis=2)


def _rope_tables(pos, reps):
    half = MOBA_DH // 2
    inv_freq = ROPE_THETA ** (-jnp.arange(half, dtype=F32) / half)
    ang = pos.astype(F32)[:, None] * inv_freq[None, :]
    cos = jnp.cos(ang)
    sin = jnp.sin(ang)
    cos2 = jnp.concatenate([cos, cos], axis=-1)
    sin2 = jnp.concatenate([-sin, sin], axis=-1)
    return jnp.tile(cos2, (reps, 1)), jnp.tile(sin2, (reps, 1))


def _layer_a(x, pos, nb, t, s0, weights, sample_ctx):
    norm, wcat, w_f2, b_f, o_norm, q_norm, k_norm, w_out = weights
    m = nb * t
    tm = min(m, 512)
    cos, sin = _rope_tables(pos, nb)
    p = _aproj(x, norm, wcat, q_norm, k_norm, cos, sin, tm=tm)
    gla_o, state = _gla(p, w_f2, b_f, s0, nb=nb, t=t)
    if sample_ctx is None:
        moba_o = _moba_prompt(p, nb=nb, t=t)
    else:
        cache_k, cache_v, page_table, layer = sample_ctx
        bmean = _block_means(cache_k, page_table, layer)
        ids = _gate_topk(p, jnp.transpose(bmean, (0, 2, 1, 3)), nb=nb, td=t)
        idx_flat = jnp.transpose(ids[:, :, :MOBA_TOPK, :t], (0, 1, 3, 2)).reshape(-1)
        moba_o = _moba_sample(p, cache_k, cache_v, page_table, idx_flat, layer=layer, nb=nb, td=t)
    a = _merge(gla_o, moba_o, p, o_norm, tm=min(m, 256))
    x_new = _gemm_res(a, w_out, x, tm=min(m, 1024), tn=512)
    mk = p[:, COL_MK:COL_MK + MOBA_W].reshape(nb, t, MOBA_HEADS, MOBA_DH)
    mv = p[:, COL_MV:COL_MV + MOBA_W].reshape(nb, t, MOBA_HEADS, MOBA_DH)
    return x_new, mk, mv, state


def _layer_c(x, ns, t, x0_re, x0_im, weights):
    norm, w_in, bb, cc, ab_re, ab_im, d, w_glu, b_glu, w_out = weights
    m = ns * t
    uz = _gemm_norm(x, norm, w_in, tm=min(m, 512), tn=1024)
    y, xr, xi = _s5_core(uz, bb, cc, ab_re, ab_im, d, x0_re, x0_im, ns=ns, t=t, nt=min(t, 256))
    v = _gemm_glu(y, w_glu, uz, b_glu, tm=min(m, 512), tn=512)
    x_new = _gemm_res(v, w_out, x, tm=min(m, 1024), tn=512)
    return x_new, xr, xi


def kernel(x_prompt, x_sample, cache_k, cache_v, state_gla, state_s5_re, state_s5_im, page_table, norm_a, w_in_a, w_gla_f2, b_gla_f, gla_out_norm, moba_q_norm, moba_k_norm, w_out_a, norm_c, w_in_c, s5_lambda_re, s5_lambda_im, s5_log_dt, s5_b_re, s5_b_im, s5_c_re, s5_c_im, s5_d, w_glu, b_glu, w_out_c):
    nbp, tp, d = x_prompt.shape
    nbs, ts, _ = x_sample.shape
    depth = norm_a.shape[0] + norm_c.shape[0]
    past_len = page_table.shape[1] * PAGE_SIZE
    assert past_len % MOBA_BLOCK == 0 and past_len // MOBA_BLOCK >= MOBA_TOPK
    assert tp % MOBA_BLOCK == 0 and tp % GLA_CHUNK == 0
    pos_p = jnp.arange(tp)
    pos_s = past_len + jnp.arange(ts)
    xp = x_prompt.reshape(nbp * tp, d)
    xs = x_sample.reshape(nbs * ts, d)
    g = d // S5_GROUP
    nslab = d // S5_SLAB
    gl = S5_SLAB // S5_GROUP
    outs = [[] for _ in range(10)]
    for layer in range(depth):
        i = layer // 2
        if layer % 2 == 0:
            w = w_in_a[i]
            gf_cols = w[:, COL_GG + GLA_VW:COL_GG + GLA_VW + GLA_GATE_RANK]
            wcat = jnp.concatenate(
                [w[:, :COL_GG + GLA_VW], w[:, COL_GG + GLA_VW + GLA_GATE_RANK:],
                 jnp.pad(gf_cols, ((0, 0), (0, A_TN - GLA_GATE_RANK)))], axis=1).astype(BF16)
            weights = (norm_a[i], wcat, w_gla_f2[i], b_gla_f[i], gla_out_norm[i], moba_q_norm[i],
                       moba_k_norm[i], w_out_a[i].astype(BF16))
            zero_state = jnp.zeros((nbp, GLA_HEADS, GLA_DK, GLA_DV), F32)
            xp, mk, mv, sp = _layer_a(xp, pos_p, nbp, tp, zero_state, weights, None)
            outs[0].append(mk)
            outs[1].append(mv)
            outs[4].append(sp)
            xs, mk, mv, ss = _layer_a(xs, pos_s, nbs, ts, state_gla[i], weights,
                                      (cache_k, cache_v, page_table, i))
            outs[2].append(mk)
            outs[3].append(mv)
            outs[5].append(ss)
        else:
            bb_re, bb_im, ab_re, ab_im = _s5_coef(s5_lambda_re[i], s5_lambda_im[i], s5_log_dt[i],
                                                  s5_b_re[i], s5_b_im[i])
            bb = _s5_compact(jnp.swapaxes(bb_re, 1, 2), jnp.swapaxes(bb_im, 1, 2))
            cc = _s5_compact(s5_c_re[i], -s5_c_im[i])
            weights = (norm_c[i], w_in_c[i].astype(BF16), bb, cc, ab_re, ab_im, s5_d[i],
                       w_glu[i].astype(BF16), b_glu[i], w_out_c[i].astype(BF16))
            zeros = jnp.zeros((nbp, g, S5_P), F32)
            xp, xr, xi = _layer_c(xp, nbp, tp, zeros, zeros, weights)
            outs[6].append(xr)
            outs[7].append(xi)
            xs, xr, xi = _layer_c(xs, nbs, ts, state_s5_re[i], state_s5_im[i], weights)
            outs[8].append(xr)
            outs[9].append(xi)
    pk, pv, sk, sv, pg, sg, psr, psi, ssr, ssi = [jnp.stack(o) for o in outs]
    return (xp.reshape(nbp, tp, d), xs.reshape(nbs, ts, d), pk, pv, sk, sv, pg, sg, psr, psi, ssr, ssi)
```

```python
import functools
import math

import jax
import jax.numpy as jnp
from jax import lax
from jax.experimental import pallas as pl
from jax.experimental.pallas import tpu as pltpu

F32 = jnp.float32
BF16 = jnp.bfloat16
I32 = jnp.int32

RMS_EPS = 1e-6
GLA_HEADS = 8
GLA_DK = 128
GLA_DV = 256
GLA_KW = GLA_HEADS * GLA_DK
GLA_VW = GLA_HEADS * GLA_DV
GLA_GATE_RANK = 16
GLA_GATE_TAU = 16.0
GLA_CHUNK = 64
GLA_GROUP_ROWS = 256
GLA_GROUP_UNROLL = 2
MOBA_HEADS = 16
MOBA_DH = 128
MOBA_W = MOBA_HEADS * MOBA_DH
MOBA_BLOCK = 256
MOBA_TOPK = 3
MOBA_PAIR = 2
ROPE_THETA = 10000.0
PAGE_SIZE = 128
BMEAN_BLOCKS_PER_STEP = 4
S5_GROUP = 16
S5_P = 64
S5_SLAB = 256
S5_SLAB_STATES = (S5_SLAB // S5_GROUP) * S5_P

LANES = 128
SUBLANES = 8
VMEM_LIMIT = 52 * 1024 * 1024
NEG_INF = float("-inf")

COL_GQ = 0
COL_GK = COL_GQ + GLA_KW
COL_GV = COL_GK + GLA_KW
COL_GG = COL_GV + GLA_VW
COL_MQ = COL_GG + GLA_VW
COL_MK = COL_MQ + MOBA_W
COL_MV = COL_MK + MOBA_W
COL_MG = COL_MV + MOBA_W
A_COLS = COL_MG + MOBA_W
WPREP_ROWS = 128
GEMM_TM = 1024
EPILOGUE_ROWS = 1024
GEMM_SUB = 256
LHS_DOUBLE_BUFFER_MAX_BYTES = 8 * 1024 * 1024


def _params(*sem):
    return pltpu.CompilerParams(dimension_semantics=sem, vmem_limit_bytes=VMEM_LIMIT)


def _nt(a, b):
    return lax.dot_general(a, b, (((1,), (1,)), ((), ())), preferred_element_type=F32)


def _tn(a, b):
    return lax.dot_general(a, b, (((0,), (0,)), ((), ())), preferred_element_type=F32)


def _silu(x):
    return x / (1.0 + jnp.exp(-x))


def _log_sigmoid(x):
    return jnp.minimum(x, 0.0) - jnp.log(1.0 + jnp.exp(-jnp.abs(x)))


def _rms_rows(x, g):
    ms = jnp.mean(x * x, axis=-1, keepdims=True)
    return x * lax.rsqrt(ms + RMS_EPS) * g


def _lhs_spec(tm, k, itemsize):
    if tm * k * itemsize > LHS_DOUBLE_BUFFER_MAX_BYTES:
        return pl.BlockSpec((tm, k), lambda i, j: (i, 0), pipeline_mode=pl.Buffered(1))
    return pl.BlockSpec((tm, k), lambda i, j: (i, 0))


def _wprep_body(w_ref, om_ref, of_ref):
    lo = COL_GG + GLA_VW
    hi = lo + GLA_GATE_RANK
    om_ref[:, :lo] = w_ref[:, :lo].astype(BF16)
    om_ref[:, lo:] = w_ref[:, hi:].astype(BF16)
    pad = jnp.zeros((w_ref.shape[0], LANES - GLA_GATE_RANK), F32)
    of_ref[...] = jnp.concatenate([w_ref[:, lo:hi], pad], axis=1).astype(BF16)


def _wprep(w):
    d, n = w.shape
    tk = WPREP_ROWS
    return pl.pallas_call(
        _wprep_body,
        grid=(d // tk,),
        in_specs=[pl.BlockSpec((tk, n), lambda i: (i, 0))],
        out_specs=[pl.BlockSpec((tk, A_COLS), lambda i: (i, 0)), pl.BlockSpec((tk, LANES), lambda i: (i, 0))],
        out_shape=[jax.ShapeDtypeStruct((d, A_COLS), BF16), jax.ShapeDtypeStruct((d, LANES), BF16)],
        compiler_params=_params("parallel"),
        name="a_weight_prep",
    )(w)


def _aproj_body(x_ref, g_ref, w_ref, wf_ref, qn_ref, kn_ref, cos_ref, sin_ref, o_ref, gf_ref, h_ref, *, tn):
    j = pl.program_id(1)

    @pl.when(j == 0)
    def _():
        h_ref[...] = _rms_rows(x_ref[...], g_ref[...]).astype(BF16)
        gf_ref[...] = jnp.dot(h_ref[...], wf_ref[...], preferred_element_type=F32)

    o_ref[...] = jnp.dot(h_ref[...], w_ref[...], preferred_element_type=F32)

    q_lo, k_lo, k_hi = COL_MQ // tn, COL_MK // tn, COL_MV // tn

    @pl.when(jnp.logical_and(j >= q_lo, j < k_hi))
    def _():
        gain = jnp.where(j < k_lo, qn_ref[...], kn_ref[...])
        rb = min(o_ref.shape[0], EPILOGUE_ROWS)

        def rows_block(r, carry):
            rows = pl.ds(pl.multiple_of(r * rb, rb), rb)
            cos = cos_ref[rows, :]
            sin = sin_ref[rows, :]
            for hh in range(tn // MOBA_DH):
                cols = slice(hh * MOBA_DH, (hh + 1) * MOBA_DH)
                y = _rms_rows(o_ref[rows, cols], gain)
                o_ref[rows, cols] = y * cos + pltpu.roll(y, MOBA_DH // 2, 1) * sin
            return carry

        lax.fori_loop(0, o_ref.shape[0] // rb, rows_block, 0)


def _aproj(x, g, w_main, w_gf, qn, kn, cos, sin, *, tm, tn):
    m, d = x.shape
    n = w_main.shape[1]
    assert n == A_COLS and COL_MQ % tn == 0 and COL_MK % tn == 0 and COL_MV % tn == 0
    return pl.pallas_call(
        functools.partial(_aproj_body, tn=tn),
        grid=(m // tm, n // tn),
        in_specs=[
            _lhs_spec(tm, d, 4),
            pl.BlockSpec((1, d), lambda i, j: (0, 0)),
            pl.BlockSpec((d, tn), lambda i, j: (0, j)),
            pl.BlockSpec((d, LANES), lambda i, j: (0, 0)),
            pl.BlockSpec((1, MOBA_DH), lambda i, j: (0, 0)),
            pl.BlockSpec((1, MOBA_DH), lambda i, j: (0, 0)),
            pl.BlockSpec((tm, MOBA_DH), lambda i, j: (i, 0)),
            pl.BlockSpec((tm, MOBA_DH), lambda i, j: (i, 0)),
        ],
        out_specs=[
            pl.BlockSpec((tm, tn), lambda i, j: (i, j)),
            pl.BlockSpec((tm, LANES), lambda i, j: (i, 0)),
        ],
        out_shape=[jax.ShapeDtypeStruct((m, n), F32), jax.ShapeDtypeStruct((m, LANES), F32)],
        scratch_shapes=[pltpu.VMEM((tm, d), BF16)],
        compiler_params=_params("parallel", "arbitrary"),
        name="a_proj",
    )(x, g.reshape(1, d), w_main, w_gf, qn.reshape(1, -1), kn.reshape(1, -1), cos, sin)


def _gemm_norm_body(x_ref, g_ref, w_ref, o_ref, h_ref):
    @pl.when(pl.program_id(1) == 0)
    def _():
        h_ref[...] = _rms_rows(x_ref[...], g_ref[...]).astype(BF16)

    o_ref[...] = jnp.dot(h_ref[...], w_ref[...], preferred_element_type=F32)


def _gemm_norm(x, g, w, *, tm, tn):
    m, d = x.shape
    n = w.shape[1]
    return pl.pallas_call(
        _gemm_norm_body,
        grid=(m // tm, n // tn),
        in_specs=[
            _lhs_spec(tm, d, 4),
            pl.BlockSpec((1, d), lambda i, j: (0, 0)),
            pl.BlockSpec((d, tn), lambda i, j: (0, j)),
        ],
        out_specs=pl.BlockSpec((tm, tn), lambda i, j: (i, j)),
        out_shape=jax.ShapeDtypeStruct((m, n), F32),
        scratch_shapes=[pltpu.VMEM((tm, d), BF16)],
        compiler_params=_params("parallel", "arbitrary"),
        name="c_proj",
    )(x, g.reshape(1, d), w)


def _gemm_res_body(a_ref, w_ref, r_ref, o_ref):
    o_ref[...] = r_ref[...] + jnp.dot(a_ref[...], w_ref[...], preferred_element_type=F32)


def _gemm_res(a, w, res, *, tm, tn):
    m, k = a.shape
    n = w.shape[1]
    return pl.pallas_call(
        _gemm_res_body,
        grid=(m // tm, n // tn),
        in_specs=[
            pl.BlockSpec((tm, k), lambda i, j: (i, 0)),
            pl.BlockSpec((k, tn), lambda i, j: (0, j)),
            pl.BlockSpec((tm, tn), lambda i, j: (i, j)),
        ],
        out_specs=pl.BlockSpec((tm, tn), lambda i, j: (i, j)),
        out_shape=jax.ShapeDtypeStruct((m, n), F32),
        compiler_params=_params("parallel", "arbitrary"),
        name="out_proj",
    )(a, w, res)


def _gemm_glu_body(y_ref, w_ref, z_ref, b_ref, o_ref, h_ref, *, tn):
    j = pl.program_id(1)

    @pl.when(j == 0)
    def _():
        h_ref[...] = y_ref[...].astype(BF16)

    sub = min(tn, GEMM_SUB)
    for c in range(tn // sub):
        cols = slice(c * sub, (c + 1) * sub)
        t = jnp.dot(h_ref[...], w_ref[:, cols], preferred_element_type=F32) + b_ref[:, cols]
        y = y_ref[:, pl.ds(pl.multiple_of(j * tn + c * sub, sub), sub)]
        z = z_ref[:, cols]
        o_ref[:, cols] = (y * z / ((1.0 + jnp.exp(-t)) * (1.0 + jnp.exp(-z)))).astype(o_ref.dtype)


def _gemm_glu(y, w, uz, bias, *, tm, tn):
    m, k = y.shape
    n = w.shape[1]
    zoff = n // tn
    return pl.pallas_call(
        functools.partial(_gemm_glu_body, tn=tn),
        grid=(m // tm, n // tn),
        in_specs=[
            _lhs_spec(tm, k, 4),
            pl.BlockSpec((k, tn), lambda i, j: (0, j)),
            pl.BlockSpec((tm, tn), lambda i, j: (i, zoff + j)),
            pl.BlockSpec((1, tn), lambda i, j: (0, j)),
        ],
        out_specs=pl.BlockSpec((tm, tn), lambda i, j: (i, j)),
        out_shape=jax.ShapeDtypeStruct((m, n), BF16),
        scratch_shapes=[pltpu.VMEM((tm, k), BF16)],
        compiler_params=_params("parallel", "arbitrary"),
        name="glu_proj",
    )(y, w, uz, bias.reshape(1, n))


def _gla_body(q_ref, k_ref, v_ref, gf_ref, wf_ref, bf_ref, s0_ref, o_ref, s_ref,
              qd_s, u_s, dec_s, st_s, *, chunk):
    c = chunk
    t = q_ref.shape[0]
    n = t // c
    grp = GLA_GROUP_ROWS
    cpg = grp // c
    shift = c.bit_length() - 1
    assert 1 << shift == c and t % grp == 0
    scale = GLA_DK ** -0.5
    r = lax.broadcasted_iota(I32, (grp, grp), 0)
    cc = lax.broadcasted_iota(I32, (grp, grp), 1)
    same = jnp.right_shift(r, shift) == jnp.right_shift(cc, shift)
    causal = jnp.logical_and(same, cc <= r)
    causal_b = causal.astype(BF16)
    wf = wf_ref[...].astype(BF16)
    bias = bf_ref[...]

    def group(i, carry):
        rows = pl.ds(pl.multiple_of(i * grp, grp), grp)
        gf = gf_ref[rows, :][:, :GLA_GATE_RANK]
        pre = jnp.dot(gf.astype(BF16), wf, preferred_element_type=F32) + bias
        g = _log_sigmoid(pre) / GLA_GATE_TAU
        g_hi = g.astype(BF16)
        g_md = (g - g_hi.astype(F32)).astype(BF16)
        g_lo = (g - g_hi.astype(F32) - g_md.astype(F32)).astype(BF16)
        b3 = jnp.dot(causal_b, jnp.concatenate([g_hi, g_md, g_lo], axis=1), preferred_element_type=F32)
        b = b3[:, :GLA_DK] + b3[:, GLA_DK:2 * GLA_DK] + b3[:, 2 * GLA_DK:]
        row_of = lambda r0: jnp.concatenate(
            [jnp.broadcast_to(b[j * c + r0:j * c + r0 + 1, :], (c, GLA_DK)) for j in range(cpg)], axis=0)
        bm = row_of(c // 2 - 1)
        be = row_of(c - 1)
        q = q_ref[rows, :] * scale
        k = k_ref[rows, :]
        v = v_ref[rows, :].astype(BF16)
        qe = (q * jnp.exp(b - bm)).astype(BF16)
        ke = (k * jnp.exp(bm - b)).astype(BF16)
        att = jnp.where(causal, _nt(qe, ke), 0.0)
        o_ref[rows, :] = jnp.dot(att.astype(BF16), v, preferred_element_type=F32)
        qd_s[rows, :] = (q * jnp.exp(b)).astype(BF16)
        kd = (k * jnp.exp(be - b)).astype(BF16)
        dec = jnp.exp(be)
        for j in range(cpg):
            cj = i * cpg + j
            u_s[cj] = _tn(v[j * c:(j + 1) * c, :], kd[j * c:(j + 1) * c, :])
            dec_s[cj] = dec[j * c:j * c + SUBLANES, :]
        return carry

    lax.fori_loop(0, t // grp, group, 0, unroll=GLA_GROUP_UNROLL)

    def advance(cj, st):
        st_s[cj] = st.astype(BF16)
        return st * dec_s[cj][0:1, :] + u_s[cj]

    st = lax.fori_loop(0, n, advance, s0_ref[...].T, unroll=4)
    s_ref[...] = st.T

    def inter(cj, carry):
        rows = pl.ds(pl.multiple_of(cj * c, c), c)
        o_ref[rows, :] = o_ref[rows, :] + _nt(qd_s[rows, :], st_s[cj])
        return carry

    lax.fori_loop(0, n, inter, 0, unroll=4)


def _gla_small_body(q_ref, k_ref, v_ref, gf_ref, wf_ref, bf_ref, s0_ref, o_ref, s_ref):
    c = q_ref.shape[0]
    scale = GLA_DK ** -0.5
    q = q_ref[...] * scale
    k = k_ref[...]
    v = v_ref[...]
    pre = bf_ref[...] + jnp.zeros((c, GLA_DK), F32)
    for r in range(GLA_GATE_RANK):
        pre = pre + gf_ref[:, r:r + 1] * wf_ref[r:r + 1, :]
    g = _log_sigmoid(pre) / GLA_GATE_TAU
    row = lax.broadcasted_iota(I32, (c, GLA_DK), 0)
    b = jnp.zeros((c, GLA_DK), F32)
    for s in range(c):
        b = b + jnp.where(row >= s, g[s:s + 1, :], 0.0)
    be = b[c - 1:c, :]
    s0 = s0_ref[...]
    o = jnp.dot(q * jnp.exp(b), s0, preferred_element_type=F32)
    for s in range(c):
        e = jnp.exp(jnp.where(row >= s, b - b[s:s + 1, :], NEG_INF))
        a_col = jnp.sum(q * k[s:s + 1, :] * e, axis=-1, keepdims=True)
        o = o + a_col * v[s:s + 1, :]
    o_ref[...] = o
    kd = k * jnp.exp(be - b)
    pad = jnp.concatenate([kd, jnp.exp(be), jnp.zeros((LANES - c - 1, GLA_DK), F32)], axis=0)
    padt = pad.T
    s_new = s0 * padt[:, c:c + 1]
    for s in range(c):
        s_new = s_new + padt[:, s:s + 1] * v[s:s + 1, :]
    s_ref[...] = s_new


def _gla(p, gf, w_f2, b_f, s0, *, nb, t):
    small = t % GLA_CHUNK != 0
    kq, kk, kv = COL_GQ // GLA_DK, COL_GK // GLA_DK, COL_GV // GLA_DV
    in_specs = [
        pl.BlockSpec((t, GLA_DK), lambda b, h: (b, kq + h)),
        pl.BlockSpec((t, GLA_DK), lambda b, h: (b, kk + h)),
        pl.BlockSpec((t, GLA_DV), lambda b, h: (b, kv + h)),
        pl.BlockSpec((t, LANES), lambda b, h: (b, 0)),
        pl.BlockSpec((GLA_GATE_RANK, GLA_DK), lambda b, h: (0, h)),
        pl.BlockSpec((1, GLA_DK), lambda b, h: (0, h)),
        pl.BlockSpec((None, None, GLA_DK, GLA_DV), lambda b, h: (b, h, 0, 0)),
    ]
    out_specs = [
        pl.BlockSpec((t, GLA_DV), lambda b, h: (b, h)),
        pl.BlockSpec((None, None, GLA_DK, GLA_DV), lambda b, h: (b, h, 0, 0)),
    ]
    out_shape = [
        jax.ShapeDtypeStruct((nb * t, GLA_VW), F32),
        jax.ShapeDtypeStruct((nb, GLA_HEADS, GLA_DK, GLA_DV), F32),
    ]
    if small:
        body, scratch = _gla_small_body, []
    else:
        body = functools.partial(_gla_body, chunk=GLA_CHUNK)
        n = t // GLA_CHUNK
        scratch = [
            pltpu.VMEM((t, GLA_DK), BF16),
            pltpu.VMEM((n, GLA_DV, GLA_DK), F32),
            pltpu.VMEM((n, SUBLANES, GLA_DK), F32),
            pltpu.VMEM((n, GLA_DV, GLA_DK), BF16),
        ]
    return pl.pallas_call(
        body,
        grid=(nb, GLA_HEADS),
        in_specs=in_specs,
        out_specs=out_specs,
        out_shape=out_shape,
        scratch_shapes=scratch,
        compiler_params=_params("parallel", "parallel"),
        name="gla_small" if small else "gla",
    )(p, p, p, gf, w_f2, b_f.reshape(1, -1), s0)


def _moba_prompt_body(q_ref, k_ref, v_ref, o_ref, *, nblk):
    blk = MOBA_BLOCK
    t = nblk * blk
    shift = blk.bit_length() - 1
    assert 1 << shift == blk and nblk <= LANES and nblk % MOBA_PAIR == 0
    scale = MOBA_DH ** -0.5
    q = q_ref[...]
    k = k_ref[...]
    qb = (q * scale).astype(BF16)
    kb = k.astype(BF16)
    vb = v_ref[...].astype(BF16)

    km = jnp.concatenate([jnp.mean(k[n * blk:(n + 1) * blk, :], axis=0, keepdims=True) for n in range(nblk)],
                         axis=0)
    gate = lax.dot_general(km, q, (((1,), (1,)), ((), ())),
                           precision=lax.Precision.HIGHEST, preferred_element_type=F32)
    nrow = lax.broadcasted_iota(I32, (nblk, t), 0)
    qblk = jnp.right_shift(lax.broadcasted_iota(I32, (nblk, t), 1), shift)
    valid = nrow < qblk
    gm = jnp.where(valid, gate, NEG_INF)
    rank = jnp.zeros((nblk, t), I32)
    for m in range(nblk):
        g_m = gm[m:m + 1, :]
        beats = jnp.logical_or(g_m > gm, jnp.logical_and(g_m == gm, m < nrow))
        rank = rank + beats.astype(I32)
    sel = jnp.logical_and(valid, rank < MOBA_TOPK).astype(F32)
    sel_c = jnp.concatenate([sel, jnp.zeros((LANES - nblk, t), F32)], axis=0).T
    bias_c = jnp.where(sel_c > 0.0, 0.0, NEG_INF)

    causal = lax.broadcasted_iota(I32, (blk, blk), 1) <= lax.broadcasted_iota(I32, (blk, blk), 0)
    masked = jnp.full((blk, blk), NEG_INF, F32)
    pair = MOBA_PAIR
    for v in range(nblk // pair):
        lo = pair * v * blk
        nbi = pair * (v + 1)
        s = _nt(qb[lo:lo + pair * blk, :], kb[0:nbi * blk, :])
        bias = bias_c[lo:lo + pair * blk, :]
        pieces = []
        for n in range(nbi):
            sn = s[:, n * blk:(n + 1) * blk]
            if n < pair * v:
                pieces.append(sn + bias[:, n:n + 1])
                continue
            j = n - pair * v
            parts = []
            for i in range(pair):
                sni = sn[i * blk:(i + 1) * blk, :]
                if i == j:
                    parts.append(jnp.where(causal, sni, NEG_INF))
                elif i > j:
                    parts.append(sni + bias[i * blk:(i + 1) * blk, n:n + 1])
                else:
                    parts.append(masked)
            pieces.append(jnp.concatenate(parts, axis=0))
        s = jnp.concatenate(pieces, axis=1)
        m = jnp.max(s, axis=-1, keepdims=True)
        p = jnp.exp(s - m)
        l = jnp.sum(p, axis=-1, keepdims=True)
        acc = jnp.dot(p.astype(BF16), vb[0:nbi * blk, :], preferred_element_type=F32)
        o_ref[lo:lo + pair * blk, :] = acc / l


def _moba_prompt(p, *, nb, t):
    nblk = t // MOBA_BLOCK
    kq, kk, kv = COL_MQ // MOBA_DH, COL_MK // MOBA_DH, COL_MV // MOBA_DH
    return pl.pallas_call(
        functools.partial(_moba_prompt_body, nblk=nblk),
        grid=(nb, MOBA_HEADS),
        in_specs=[
            pl.BlockSpec((t, MOBA_DH), lambda b, h: (b, kq + h)),
            pl.BlockSpec((t, MOBA_DH), lambda b, h: (b, kk + h)),
            pl.BlockSpec((t, MOBA_DH), lambda b, h: (b, kv + h)),
        ],
        out_specs=pl.BlockSpec((t, MOBA_DH), lambda b, h: (b, h)),
        out_shape=jax.ShapeDtypeStruct((nb * t, MOBA_W), F32),
        compiler_params=_params("parallel", "parallel"),
        name="moba_prompt",
    )(p, p, p)


def _bmean_body(pt_ref, *refs, ppb):
    pages, o_ref = refs[:-1], refs[-1]
    for j in range(len(pages) // ppb):
        tot = jnp.sum(pages[j * ppb][...], axis=0)
        for r in pages[j * ppb + 1:(j + 1) * ppb]:
            tot = tot + jnp.sum(r[...], axis=0)
        o_ref[j] = tot * (1.0 / MOBA_BLOCK)


def _block_means(cache_k, page_table, layer):
    nb, n_pages = page_table.shape
    ppb = MOBA_BLOCK // PAGE_SIZE
    n_full = n_pages // ppb
    bps = BMEAN_BLOCKS_PER_STEP
    assert n_full % bps == 0
    pps = bps * ppb
    page_block = (None, None, PAGE_SIZE, MOBA_HEADS, MOBA_DH)
    page_spec = lambda j: pl.BlockSpec(page_block, lambda b, n, pt: (layer, pt[b, pps * n + j], 0, 0, 0))
    return pl.pallas_call(
        functools.partial(_bmean_body, ppb=ppb),
        grid_spec=pltpu.PrefetchScalarGridSpec(
            num_scalar_prefetch=1,
            grid=(nb, n_full // bps),
            in_specs=[page_spec(j) for j in range(pps)],
            out_specs=pl.BlockSpec((None, bps, MOBA_HEADS, MOBA_DH), lambda b, n, pt: (b, n, 0, 0)),
        ),
        out_shape=jax.ShapeDtypeStruct((nb, n_full, MOBA_HEADS, MOBA_DH), F32),
        compiler_params=_params("parallel", "arbitrary"),
        name="moba_block_means",
    )(page_table, *([cache_k] * pps))


def _gate_topk_body(q_ref, bm_ref, o_ref, *, td):
    n_full = bm_ref.shape[1]
    lane = lax.broadcasted_iota(I32, (n_full, LANES), 1)
    row = lax.broadcasted_iota(I32, (n_full, LANES), 0)
    orow = lax.broadcasted_iota(I32, (SUBLANES, LANES), 0)
    for h in range(MOBA_HEADS):
        km = bm_ref[h]
        g = jnp.full((n_full, LANES), NEG_INF, F32)
        for t in range(td):
            qv = q_ref[t:t + 1, h * MOBA_DH:(h + 1) * MOBA_DH]
            g = jnp.where(lane == t, jnp.sum(km * qv, axis=-1, keepdims=True), g)
        outv = jnp.zeros((SUBLANES, LANES), I32)
        for j in range(MOBA_TOPK):
            mx = jnp.max(g, axis=0, keepdims=True)
            idx = jnp.min(jnp.where(g == mx, row, n_full), axis=0, keepdims=True)
            g = jnp.where(row == idx, NEG_INF, g)
            outv = jnp.where(orow == j, idx, outv)
        o_ref[h] = outv


def _gate_topk(p, bmean_t, *, nb, td):
    n_full = bmean_t.shape[2]
    assert td <= SUBLANES and MOBA_TOPK <= SUBLANES
    kq = COL_MQ // MOBA_W
    return pl.pallas_call(
        functools.partial(_gate_topk_body, td=td),
        grid=(nb,),
        in_specs=[
            pl.BlockSpec((td, MOBA_W), lambda b: (b, kq)),
            pl.BlockSpec((None, MOBA_HEADS, n_full, MOBA_DH), lambda b: (b, 0, 0, 0)),
        ],
        out_specs=pl.BlockSpec((None, MOBA_HEADS, SUBLANES, LANES), lambda b: (b, 0, 0, 0)),
        out_shape=jax.ShapeDtypeStruct((nb, MOBA_HEADS, SUBLANES, LANES), I32),
        compiler_params=_params("parallel"),
        name="moba_gate_topk",
    )(p, bmean_t)


def _moba_sample_body(idx_ref, pt_ref, q_ref, k_ref, v_ref, ck_ref, cv_ref, o_ref, kbuf, vbuf, sem, *, layer, td):
    step = pl.program_id(0)
    nsteps = pl.num_programs(0)
    slot = lax.rem(step, 2)
    ppb = MOBA_BLOCK // PAGE_SIZE
    per_q = MOBA_TOPK * ppb
    nsel = MOBA_TOPK * MOBA_BLOCK
    scale = MOBA_DH ** -0.5
    slots = [(t, j, pg) for t in range(td) for j in range(MOBA_TOPK) for pg in range(ppb)]

    def copies(st, sl, t, j, pg):
        b = lax.div(st, MOBA_HEADS)
        h = lax.rem(st, MOBA_HEADS)
        blk = idx_ref[(st * td + t) * MOBA_TOPK + j]
        page = pt_ref[b, blk * ppb + pg]
        dst = pl.ds((t * per_q + j * ppb + pg) * PAGE_SIZE, PAGE_SIZE)
        return (pltpu.make_async_copy(ck_ref.at[layer, page, :, h, :], kbuf.at[sl, dst, :], sem.at[sl, 0]),
                pltpu.make_async_copy(cv_ref.at[layer, page, :, h, :], vbuf.at[sl, dst, :], sem.at[sl, 1]))

    def issue(st, sl):
        for c in slots:
            ck, cv = copies(st, sl, *c)
            ck.start()
            cv.start()

    @pl.when(step == 0)
    def _():
        issue(step, slot)

    @pl.when(step + 1 < nsteps)
    def _():
        issue(step + 1, 1 - slot)

    for c in slots:
        ck, cv = copies(step, slot, *c)
        ck.wait()
        cv.wait()

    rows = 2 * SUBLANES
    q = q_ref[...]
    qp = jnp.concatenate([q, jnp.zeros((rows - td, MOBA_DH), F32)], axis=0).astype(BF16)
    s_sel = _nt(qp, kbuf[slot].astype(BF16)) * scale
    r = lax.broadcasted_iota(I32, (rows, td * nsel), 0)
    c = lax.broadcasted_iota(I32, (rows, td * nsel), 1)
    mine = jnp.logical_and(c >= r * nsel, c < (r + 1) * nsel)
    s_sel = jnp.where(mine, s_sel, NEG_INF)

    k_new = k_ref[...]
    v_new = v_ref[...]
    rn = lax.broadcasted_iota(I32, (td, LANES), 0)
    cn = lax.broadcasted_iota(I32, (td, LANES), 1)
    s_new = jnp.full((td, LANES), NEG_INF, F32)
    for t in range(td):
        col = jnp.sum(q * k_new[t:t + 1, :], axis=-1, keepdims=True) * scale
        s_new = jnp.where(jnp.logical_and(cn == t, rn >= t), col, s_new)

    s_sel = s_sel[:td, :]
    m = jnp.maximum(jnp.max(s_sel, axis=-1, keepdims=True), jnp.max(s_new, axis=-1, keepdims=True))
    p_sel = jnp.exp(s_sel - m)
    p_new = jnp.exp(s_new - m)
    l = jnp.sum(p_sel, axis=-1, keepdims=True) + jnp.sum(p_new, axis=-1, keepdims=True)
    pp = jnp.concatenate([p_sel, jnp.zeros((rows - td, td * nsel), F32)], axis=0).astype(BF16)
    acc = jnp.dot(pp, vbuf[slot].astype(BF16), preferred_element_type=F32)[:td, :]
    for t in range(td):
        acc = acc + p_new[:, t:t + 1] * v_new[t:t + 1, :]
    o_ref[...] = acc / l


def _moba_sample(p, cache_k, cache_v, page_table, idx_flat, *, layer, nb, td):
    kq, kk, kv = COL_MQ // MOBA_DH, COL_MK // MOBA_DH, COL_MV // MOBA_DH
    nrows = td * MOBA_TOPK * MOBA_BLOCK
    row_spec = lambda off: pl.BlockSpec(
        (td, MOBA_DH), lambda s, idx, pt: (lax.div(s, MOBA_HEADS), off + lax.rem(s, MOBA_HEADS)))
    return pl.pallas_call(
        functools.partial(_moba_sample_body, layer=layer, td=td),
        grid_spec=pltpu.PrefetchScalarGridSpec(
            num_scalar_prefetch=2,
            grid=(nb * MOBA_HEADS,),
            in_specs=[
                row_spec(kq), row_spec(kk), row_spec(kv),
                pl.BlockSpec(memory_space=pl.ANY),
                pl.BlockSpec(memory_space=pl.ANY),
            ],
            out_specs=row_spec(0),
            scratch_shapes=[
                pltpu.VMEM((2, nrows, MOBA_DH), F32),
                pltpu.VMEM((2, nrows, MOBA_DH), F32),
                pltpu.SemaphoreType.DMA((2, 2)),
            ],
        ),
        out_shape=jax.ShapeDtypeStruct((nb * td, MOBA_W), F32),
        compiler_params=_params("arbitrary"),
        name="moba_sample",
    )(idx_flat, page_table, p, p, p, cache_k, cache_v)


def _merge_body(go_ref, gg_ref, mo_ref, mg_ref, on_ref, a_ref):
    on = on_ref[...]
    for h in range(GLA_HEADS):
        sl = slice(h * GLA_DV, (h + 1) * GLA_DV)
        g = _rms_rows(go_ref[:, sl], on) * _silu(gg_ref[:, sl])
        a_ref[:, sl] = g.astype(a_ref.dtype)
    a_ref[:, GLA_VW:] = (mo_ref[...] * _silu(mg_ref[...])).astype(a_ref.dtype)


def _merge(gla_o, moba_o, p, o_norm, *, tm):
    m = gla_o.shape[0]
    kg, km = COL_GG // GLA_VW, COL_MG // MOBA_W
    return pl.pallas_call(
        _merge_body,
        grid=(m // tm,),
        in_specs=[
            pl.BlockSpec((tm, GLA_VW), lambda i: (i, 0)),
            pl.BlockSpec((tm, GLA_VW), lambda i: (i, kg)),
            pl.BlockSpec((tm, MOBA_W), lambda i: (i, 0)),
            pl.BlockSpec((tm, MOBA_W), lambda i: (i, km)),
            pl.BlockSpec((1, GLA_DV), lambda i: (0, 0)),
        ],
        out_specs=pl.BlockSpec((tm, GLA_VW + MOBA_W), lambda i: (i, 0)),
        out_shape=jax.ShapeDtypeStruct((m, GLA_VW + MOBA_W), BF16),
        compiler_params=_params("parallel"),
        name="a_merge",
    )(gla_o, p, moba_o, p, o_norm.reshape(1, -1))


def _s5_coef_body(lr_ref, li_ref, ldt_ref, br_ref, bi_ref, bbr_ref, bbi_ref, abr_ref, abi_ref):
    lr = lr_ref[...]
    li = li_ref[...]
    dt = jnp.exp(ldt_ref[...])
    mag = jnp.exp(lr * dt)
    ab_re = mag * jnp.cos(li * dt)
    ab_im = mag * jnp.sin(li * dt)
    den = lr * lr + li * li
    nr = ab_re - 1.0
    cr = (nr * lr + ab_im * li) / den
    ci = (ab_im * lr - nr * li) / den
    br = br_ref[...]
    bi = bi_ref[...]
    bbr_ref[...] = cr * br - ci * bi
    bbi_ref[...] = cr * bi + ci * br
    abr_ref[...] = ab_re
    abi_ref[...] = ab_im


def _s5_coef(lam_re, lam_im, log_dt, b_re, b_im):
    g, p = lam_re.shape
    w = p * S5_GROUP
    rep = lambda a: jnp.repeat(a, S5_GROUP, axis=1)
    full = pl.BlockSpec((g, w), lambda: (0, 0))
    outs = pl.pallas_call(
        _s5_coef_body,
        in_specs=[full] * 5,
        out_specs=[full] * 4,
        out_shape=[jax.ShapeDtypeStruct((g, w), F32)] * 4,
        name="s5_coef",
    )(rep(lam_re), rep(lam_im), jnp.broadcast_to(log_dt[:, None], (g, w)),
      b_re.reshape(g, w), b_im.reshape(g, w))
    bbr, bbi, abr, abi = outs
    return (bbr.reshape(g, p, S5_GROUP), bbi.reshape(g, p, S5_GROUP),
            abr[:, ::S5_GROUP], abi[:, ::S5_GROUP])


def _s5_expand(w2):
    per_tile = LANES // S5_P
    rg = jnp.right_shift(lax.broadcasted_iota(I32, (S5_SLAB, LANES), 0), S5_GROUP.bit_length() - 1)
    lg = jnp.right_shift(lax.broadcasted_iota(I32, (S5_SLAB, LANES), 1), S5_P.bit_length() - 1)
    pieces = []
    for part in range(2):
        w = w2[:, part * LANES:(part + 1) * LANES]
        for kk in range(S5_SLAB_STATES // LANES):
            pieces.append(jnp.where(rg == per_tile * kk + lg, w, 0.0))
    return jnp.concatenate(pieces, axis=1)


def _s5_body(u_ref, bb_ref, cc_ref, ar_ref, ai_ref, d_ref, x0r_ref, x0i_ref,
             y_ref, xr_ref, xi_ref, lhs, bu, ybuf, cr, ci, bd_in, bd_out, *, ns, nt):
    ti = pl.program_id(1)
    tpv = SUBLANES // ns
    nst = S5_SLAB_STATES
    lo_rows = SUBLANES - ns
    nlb = S5_SLAB // LANES

    @pl.when(ti == 0)
    def _():
        cr[...] = jnp.zeros_like(cr)
        ci[...] = jnp.zeros_like(ci)
        cr[lo_rows:, :] = x0r_ref[...]
        ci[lo_rows:, :] = x0i_ref[...]
        e_in = _s5_expand(bb_ref[...])
        bd_in[:S5_SLAB, :] = e_in.astype(BF16)
        if tpv == 2:
            e_re, e_im = e_in[:, :nst], e_in[:, nst:]
            a_re, a_im = ar_ref[...], ai_ref[...]
            bd_in[S5_SLAB:, :] = jnp.concatenate([a_re * e_re - a_im * e_im, a_re * e_im + a_im * e_re],
                                                 axis=1).astype(BF16)
        bd_out[...] = _s5_expand(cc_ref[...]).astype(BF16)

    for b in range(ns):
        for j in range(nlb):
            lhs[j, pl.ds(b, nt, stride=ns), :] = u_ref[b, :, j * LANES:(j + 1) * LANES]
    u_rows = jnp.concatenate([lhs[j] for j in range(nlb)], axis=1)
    if tpv == 2:
        tbit = jnp.right_shift(lax.broadcasted_iota(I32, u_rows.shape, 0), ns.bit_length() - 1)
        u_prev = jnp.where(jnp.bitwise_and(tbit, 1) == 1, pltpu.roll(u_rows, ns, 0), 0.0)
        lhs_rows = jnp.concatenate([u_rows, u_prev], axis=1)
    else:
        lhs_rows = u_rows
    bu[...] = jnp.dot(lhs_rows.astype(BF16), bd_in[...], preferred_element_type=F32)

    ar = ar_ref[...]
    ai = ai_ref[...]
    if tpv == 1:
        def step(v, carry):
            xr, xi = carry
            sl = pl.ds(pl.multiple_of(v * SUBLANES, SUBLANES), SUBLANES)
            nxr = ar * xr - ai * xi + bu[sl, :nst]
            nxi = ar * xi + ai * xr + bu[sl, nst:]
            bu[sl, :nst] = nxr
            bu[sl, nst:] = nxi
            return nxr, nxi
    else:
        assert tpv == 2
        lo = lax.broadcasted_iota(I32, (SUBLANES, nst), 0) < ns
        c2r = jnp.where(lo, ar, ar * ar - ai * ai)
        c2i = jnp.where(lo, ai, 2.0 * ar * ai)

        def step(v, carry):
            xr, xi = carry
            sl = pl.ds(pl.multiple_of(v * SUBLANES, SUBLANES), SUBLANES)
            pr = jnp.where(lo, pltpu.roll(xr, ns, 0), xr)
            pi = jnp.where(lo, pltpu.roll(xi, ns, 0), xi)
            nxr = bu[sl, :nst] + c2r * pr - c2i * pi
            nxi = bu[sl, nst:] + c2r * pi + c2i * pr
            bu[sl, :nst] = nxr
            bu[sl, nst:] = nxi
            return nxr, nxi

    nv = (nt * ns) // SUBLANES
    xr, xi = lax.fori_loop(0, nv, step, (cr[...], ci[...]), unroll=2 if nv % 2 == 0 else 1)
    cr[...] = xr
    ci[...] = xi

    yv = _nt(bu[...].astype(BF16), bd_out[...]) + d_ref[...] * u_rows
    yv = 0.5 * yv * (1.0 + lax.erf(yv * math.sqrt(0.5)))
    for j in range(nlb):
        ybuf[j] = yv[:, j * LANES:(j + 1) * LANES]
    for b in range(ns):
        for j in range(nlb):
            y_ref[b, :, j * LANES:(j + 1) * LANES] = ybuf[j, pl.ds(b, nt, stride=ns), :]

    @pl.when(ti == pl.num_programs(1) - 1)
    def _():
        xr_ref[...] = cr[lo_rows:, :]
        xi_ref[...] = ci[lo_rows:, :]


def _s5_core(uz, bb, cc, ab_re, ab_im, d, x0_re, x0_im, *, ns, t, nt):
    w = d.shape[0]
    nslab = w // S5_SLAB
    nst = S5_SLAB_STATES
    assert SUBLANES % ns == 0 and t % nt == 0 and (nt * ns) % SUBLANES == 0
    u3 = uz.reshape(ns, t, uz.shape[1])
    to_slab = lambda a: jnp.transpose(a.reshape(ns, nslab, nst), (1, 0, 2))
    st_spec = pl.BlockSpec((None, ns, nst), lambda s, i: (s, 0, 0))
    y, xr, xi = pl.pallas_call(
        functools.partial(_s5_body, ns=ns, nt=nt),
        grid=(nslab, t // nt),
        in_specs=[
            pl.BlockSpec((ns, nt, S5_SLAB), lambda s, i: (0, i, s)),
            pl.BlockSpec((None, S5_SLAB, 2 * LANES), lambda s, i: (s, 0, 0)),
            pl.BlockSpec((None, S5_SLAB, 2 * LANES), lambda s, i: (s, 0, 0)),
            pl.BlockSpec((None, 1, nst), lambda s, i: (s, 0, 0)),
            pl.BlockSpec((None, 1, nst), lambda s, i: (s, 0, 0)),
            pl.BlockSpec((1, S5_SLAB), lambda s, i: (0, s)),
            st_spec, st_spec,
        ],
        out_specs=[
            pl.BlockSpec((ns, nt, S5_SLAB), lambda s, i: (0, i, s)),
            st_spec, st_spec,
        ],
        out_shape=[
            jax.ShapeDtypeStruct((ns, t, w), F32),
            jax.ShapeDtypeStruct((nslab, ns, nst), F32),
            jax.ShapeDtypeStruct((nslab, ns, nst), F32),
        ],
        scratch_shapes=[
            pltpu.VMEM((S5_SLAB // LANES, nt * ns, LANES), F32),
            pltpu.VMEM((nt * ns, 2 * nst), F32),
            pltpu.VMEM((S5_SLAB // LANES, nt * ns, LANES), F32),
            pltpu.VMEM((SUBLANES, nst), F32),
            pltpu.VMEM((SUBLANES, nst), F32),
            pltpu.VMEM((S5_SLAB * (SUBLANES // ns), 2 * nst), BF16),
            pltpu.VMEM((S5_SLAB, 2 * nst), BF16),
        ],
        compiler_params=_params("parallel", "arbitrary"),
        name="s5_core",
    )(u3, bb, cc, ab_re.reshape(nslab, 1, nst), ab_im.reshape(nslab, 1, nst), d.reshape(1, w),
      to_slab(x0_re), to_slab(x0_im))
    g = w // S5_GROUP
    from_slab = lambda a: jnp.transpose(a, (1, 0, 2)).reshape(ns, g, S5_P)
    return y.reshape(ns * t, w), from_slab(xr), from_slab(xi)


def _s5_compact(re, im):
    reps = LANES // S5_P
    flat = lambda a: jnp.tile(a.reshape(-1, S5_SLAB, S5_P), (1, 1, reps))
    return jnp.concatenate([flat(re), flat(im)], axis=2)


def _rope_tables(pos, reps):
    half = MOBA_DH // 2
    inv_freq = ROPE_THETA ** (-jnp.arange(half, dtype=F32) / half)
    ang = pos.astype(F32)[:, None] * inv_freq[None, :]
    cos = jnp.cos(ang)
    sin = jnp.sin(ang)
    cos2 = jnp.concatenate([cos, cos], axis=-1)
    sin2 = jnp.concatenate([-sin, sin], axis=-1)
    return jnp.tile(cos2, (reps, 1)), jnp.tile(sin2, (reps, 1))


def _layer_a(x, pos, nb, t, s0, weights, sample_ctx):
    norm, w_main, w_gf, w_f2, b_f, o_norm, q_norm, k_norm, w_out = weights
    m = nb * t
    tm = min(m, GEMM_TM)
    cos, sin = _rope_tables(pos, nb)
    p, gf = _aproj(x, norm, w_main, w_gf, q_norm, k_norm, cos, sin, tm=tm, tn=512)
    gla_o, state = _gla(p, gf, w_f2, b_f, s0, nb=nb, t=t)
    if sample_ctx is None:
        moba_o = _moba_prompt(p, nb=nb, t=t)
    else:
        cache_k, cache_v, page_table, layer = sample_ctx
        bmean = _block_means(cache_k, page_table, layer)
        ids = _gate_topk(p, jnp.transpose(bmean, (0, 2, 1, 3)), nb=nb, td=t)
        idx_flat = jnp.transpose(ids[:, :, :MOBA_TOPK, :t], (0, 1, 3, 2)).reshape(-1)
        moba_o = _moba_sample(p, cache_k, cache_v, page_table, idx_flat, layer=layer, nb=nb, td=t)
    a = _merge(gla_o, moba_o, p, o_norm, tm=min(m, 256))
    x_new = _gemm_res(a, w_out, x, tm=tm, tn=512)
    mk = p[:, COL_MK:COL_MK + MOBA_W].reshape(nb, t, MOBA_HEADS, MOBA_DH)
    mv = p[:, COL_MV:COL_MV + MOBA_W].reshape(nb, t, MOBA_HEADS, MOBA_DH)
    return x_new, mk, mv, state


def _layer_c(x, ns, t, x0_re, x0_im, weights):
    norm, w_in, bb, cc, ab_re, ab_im, d, w_glu, b_glu, w_out = weights
    m = ns * t
    tm = min(m, GEMM_TM)
    uz = _gemm_norm(x, norm, w_in, tm=tm, tn=512)
    y, xr, xi = _s5_core(uz, bb, cc, ab_re, ab_im, d, x0_re, x0_im, ns=ns, t=t, nt=min(t, 256))
    v = _gemm_glu(y, w_glu, uz, b_glu, tm=tm, tn=512)
    x_new = _gemm_res(v, w_out, x, tm=tm, tn=512)
    return x_new, xr, xi


def kernel(x_prompt, x_sample, cache_k, cache_v, state_gla, state_s5_re, state_s5_im, page_table, norm_a, w_in_a, w_gla_f2, b_gla_f, gla_out_norm, moba_q_norm, moba_k_norm, w_out_a, norm_c, w_in_c, s5_lambda_re, s5_lambda_im, s5_log_dt, s5_b_re, s5_b_im, s5_c_re, s5_c_im, s5_d, w_glu, b_glu, w_out_c):
    nbp, tp, d = x_prompt.shape
    nbs, ts, _ = x_sample.shape
    depth = norm_a.shape[0] + norm_c.shape[0]
    past_len = page_table.shape[1] * PAGE_SIZE
    assert past_len % MOBA_BLOCK == 0 and past_len // MOBA_BLOCK >= MOBA_TOPK
    assert tp % MOBA_BLOCK == 0 and tp % GLA_CHUNK == 0
    pos_p = jnp.arange(tp)
    pos_s = past_len + jnp.arange(ts)
    xp = x_prompt.reshape(nbp * tp, d)
    xs = x_sample.reshape(nbs * ts, d)
    g = d // S5_GROUP
    outs = [[] for _ in range(10)]
    for layer in range(depth):
        i = layer // 2
        if layer % 2 == 0:
            w_main, w_gf = _wprep(w_in_a[i])
            weights = (norm_a[i], w_main, w_gf, w_gla_f2[i], b_gla_f[i], gla_out_norm[i], moba_q_norm[i],
                       moba_k_norm[i], w_out_a[i].astype(BF16))
            zero_state = jnp.zeros((nbp, GLA_HEADS, GLA_DK, GLA_DV), F32)
            xp, mk, mv, sp = _layer_a(xp, pos_p, nbp, tp, zero_state, weights, None)
            outs[0].append(mk)
            outs[1].append(mv)
            outs[4].append(sp)
            xs, mk, mv, ss = _layer_a(xs, pos_s, nbs, ts, state_gla[i], weights,
                                      (cache_k, cache_v, page_table, i))
            outs[2].append(mk)
            outs[3].append(mv)
            outs[5].append(ss)
        else:
            bb_re, bb_im, ab_re, ab_im = _s5_coef(s5_lambda_re[i], s5_lambda_im[i], s5_log_dt[i],
                                                  s5_b_re[i], s5_b_im[i])
            bb = _s5_compact(jnp.swapaxes(bb_re, 1, 2), jnp.swapaxes(bb_im, 1, 2))
            cc = _s5_compact(s5_c_re[i], -s5_c_im[i])
            weights = (norm_c[i], w_in_c[i].astype(BF16), bb, cc, ab_re, ab_im, s5_d[i],
                       w_glu[i].astype(BF16), b_glu[i], w_out_c[i].astype(BF16))
            zeros = jnp.zeros((nbp, g, S5_P), F32)
            xp, xr, xi = _layer_c(xp, nbp, tp, zeros, zeros, weights)
            outs[6].append(xr)
            outs[7].append(xi)
            xs, xr, xi = _layer_c(xs, nbs, ts, state_s5_re[i], state_s5_im[i], weights)
            outs[8].append(xr)
            outs[9].append(xi)
    pk, pv, sk, sv, pg, sg, psr, psi, ssr, ssi = [jnp.stack(o) for o in outs]
    return (xp.reshape(nbp, tp, d), xs.reshape(nbs, ts, d), pk, pv, sk, sv, pg, sg, psr, psi, ssr, ssi)
```

```python
import functools
import math

import jax
import jax.numpy as jnp
from jax import lax
from jax.experimental import pallas as pl
from jax.experimental.pallas import tpu as pltpu

F32 = jnp.float32
BF16 = jnp.bfloat16
I32 = jnp.int32

RMS_EPS = 1e-6
GLA_HEADS = 8
GLA_DK = 128
GLA_DV = 256
GLA_KW = GLA_HEADS * GLA_DK
GLA_VW = GLA_HEADS * GLA_DV
GLA_GATE_RANK = 16
GLA_GATE_TAU = 16.0
GLA_CHUNK = 64
GLA_GROUP_ROWS = 256
MOBA_HEADS = 16
MOBA_DH = 128
MOBA_W = MOBA_HEADS * MOBA_DH
MOBA_BLOCK = 256
MOBA_TOPK = 3
MOBA_PAIR = 2
ROPE_THETA = 10000.0
PAGE_SIZE = 128
BMEAN_BLOCKS_PER_STEP = 4
S5_GROUP = 16
S5_P = 64
S5_SLAB = 128
S5_SLAB_STATES = (S5_SLAB // S5_GROUP) * S5_P
S5_TIME_TILE = 512

LANES = 128
SUBLANES = 8
VMEM_LIMIT = 52 * 1024 * 1024
NEG_INF = float("-inf")

COL_GQ = 0
COL_GK = COL_GQ + GLA_KW
COL_GV = COL_GK + GLA_KW
COL_GG = COL_GV + GLA_VW
COL_MQ = COL_GG + GLA_VW
COL_MK = COL_MQ + MOBA_W
COL_MV = COL_MK + MOBA_W
COL_MG = COL_MV + MOBA_W
A_COLS = COL_MG + MOBA_W
WPREP_ROWS = 256
GEMM_TM = 1024
EPILOGUE_ROWS = 1024
GEMM_SUB = 256
LHS_DOUBLE_BUFFER_MAX_BYTES = 8 * 1024 * 1024


def _params(*sem):
    return pltpu.CompilerParams(dimension_semantics=sem, vmem_limit_bytes=VMEM_LIMIT)


def _nt(a, b):
    return lax.dot_general(a, b, (((1,), (1,)), ((), ())), preferred_element_type=F32)


def _tn(a, b):
    return lax.dot_general(a, b, (((0,), (0,)), ((), ())), preferred_element_type=F32)


def _silu(x):
    return x / (1.0 + jnp.exp(-x))


def _log_sigmoid(x):
    return jnp.minimum(x, 0.0) - jnp.log(1.0 + jnp.exp(-jnp.abs(x)))


def _rms_rows(x, g):
    ms = jnp.mean(x * x, axis=-1, keepdims=True)
    return x * lax.rsqrt(ms + RMS_EPS) * g


def _lhs_spec(tm, k, itemsize):
    if tm * k * itemsize > LHS_DOUBLE_BUFFER_MAX_BYTES:
        return pl.BlockSpec((tm, k), lambda i, j: (i, 0), pipeline_mode=pl.Buffered(1))
    return pl.BlockSpec((tm, k), lambda i, j: (i, 0))


def _wprep_body(a_ref, b_ref, om_ref, of_ref, *, gate_blk):
    i = pl.program_id(0)
    gr = GLA_GATE_RANK

    @pl.when(i == 0)
    def _():
        of_ref[...] = jnp.zeros_like(of_ref)

    @pl.when(i < gate_blk)
    def _():
        om_ref[...] = a_ref[...].astype(BF16)

    @pl.when(i == gate_blk)
    def _():
        of_ref[:gr, :] = a_ref[:gr, :].astype(BF16)

    @pl.when(i >= gate_blk)
    def _():
        om_ref[:-gr, :] = a_ref[gr:, :].astype(BF16)
        om_ref[-gr:, :] = b_ref[...].astype(BF16)


def _wprep(wt):
    n, d = wt.shape
    tk = WPREP_ROWS
    gr = GLA_GATE_RANK
    gate_lo = COL_GG + GLA_VW
    assert n == A_COLS + gr and gate_lo % tk == 0 and A_COLS % tk == 0 and tk % gr == 0
    last_tail = (n - gr) // gr
    return pl.pallas_call(
        functools.partial(_wprep_body, gate_blk=gate_lo // tk),
        grid=(A_COLS // tk,),
        in_specs=[
            pl.BlockSpec((tk, d), lambda i: (i, 0)),
            pl.BlockSpec((gr, d), lambda i: (jnp.minimum((i + 1) * (tk // gr), last_tail), 0)),
        ],
        out_specs=[pl.BlockSpec((tk, d), lambda i: (i, 0)), pl.BlockSpec((LANES, d), lambda i: (0, 0))],
        out_shape=[jax.ShapeDtypeStruct((A_COLS, d), BF16), jax.ShapeDtypeStruct((LANES, d), BF16)],
        compiler_params=_params("arbitrary"),
        name="a_weight_prep",
    )(wt, wt)


def _aproj_body(x_ref, g_ref, w_ref, wf_ref, qn_ref, kn_ref, cos_ref, sin_ref, o_ref, gf_ref, h_ref, *, tn):
    j = pl.program_id(1)

    @pl.when(j == 0)
    def _():
        h_ref[...] = _rms_rows(x_ref[...], g_ref[...]).astype(BF16)
        gf_ref[...] = _nt(h_ref[...], wf_ref[...])

    o_ref[...] = _nt(h_ref[...], w_ref[...])

    q_lo, k_lo, k_hi = COL_MQ // tn, COL_MK // tn, COL_MV // tn

    @pl.when(jnp.logical_and(j >= q_lo, j < k_hi))
    def _():
        gain = jnp.where(j < k_lo, qn_ref[...], kn_ref[...])
        rb = min(o_ref.shape[0], EPILOGUE_ROWS)

        def rows_block(r, carry):
            rows = pl.ds(pl.multiple_of(r * rb, rb), rb)
            cos = cos_ref[rows, :]
            sin = sin_ref[rows, :]
            for hh in range(tn // MOBA_DH):
                cols = slice(hh * MOBA_DH, (hh + 1) * MOBA_DH)
                y = _rms_rows(o_ref[rows, cols], gain)
                o_ref[rows, cols] = y * cos + pltpu.roll(y, MOBA_DH // 2, 1) * sin
            return carry

        lax.fori_loop(0, o_ref.shape[0] // rb, rows_block, 0)


def _aproj(x, g, w_main, w_gf, qn, kn, cos, sin, *, tm, tn):
    m, d = x.shape
    n = w_main.shape[0]
    assert n == A_COLS and COL_MQ % tn == 0 and COL_MK % tn == 0 and COL_MV % tn == 0
    return pl.pallas_call(
        functools.partial(_aproj_body, tn=tn),
        grid=(m // tm, n // tn),
        in_specs=[
            _lhs_spec(tm, d, 4),
            pl.BlockSpec((1, d), lambda i, j: (0, 0)),
            pl.BlockSpec((tn, d), lambda i, j: (j, 0)),
            pl.BlockSpec((LANES, d), lambda i, j: (0, 0)),
            pl.BlockSpec((1, MOBA_DH), lambda i, j: (0, 0)),
            pl.BlockSpec((1, MOBA_DH), lambda i, j: (0, 0)),
            pl.BlockSpec((tm, MOBA_DH), lambda i, j: (i, 0)),
            pl.BlockSpec((tm, MOBA_DH), lambda i, j: (i, 0)),
        ],
        out_specs=[
            pl.BlockSpec((tm, tn), lambda i, j: (i, j)),
            pl.BlockSpec((tm, LANES), lambda i, j: (i, 0)),
        ],
        out_shape=[jax.ShapeDtypeStruct((m, n), F32), jax.ShapeDtypeStruct((m, LANES), F32)],
        scratch_shapes=[pltpu.VMEM((tm, d), BF16)],
        compiler_params=_params("parallel", "arbitrary"),
        name="a_proj",
    )(x, g.reshape(1, d), w_main, w_gf, qn.reshape(1, -1), kn.reshape(1, -1), cos, sin)


def _gemm_norm_body(x_ref, g_ref, w_ref, o_ref, h_ref):
    @pl.when(pl.program_id(1) == 0)
    def _():
        h_ref[...] = _rms_rows(x_ref[...], g_ref[...]).astype(BF16)

    o_ref[...] = jnp.dot(h_ref[...], w_ref[...], preferred_element_type=F32)


def _gemm_norm(x, g, w, *, tm, tn):
    m, d = x.shape
    n = w.shape[1]
    return pl.pallas_call(
        _gemm_norm_body,
        grid=(m // tm, n // tn),
        in_specs=[
            _lhs_spec(tm, d, 4),
            pl.BlockSpec((1, d), lambda i, j: (0, 0)),
            pl.BlockSpec((d, tn), lambda i, j: (0, j)),
        ],
        out_specs=pl.BlockSpec((tm, tn), lambda i, j: (i, j)),
        out_shape=jax.ShapeDtypeStruct((m, n), F32),
        scratch_shapes=[pltpu.VMEM((tm, d), BF16)],
        compiler_params=_params("parallel", "arbitrary"),
        name="c_proj",
    )(x, g.reshape(1, d), w)


def _gemm_res_body(a_ref, w_ref, r_ref, o_ref):
    o_ref[...] = r_ref[...] + jnp.dot(a_ref[...], w_ref[...], preferred_element_type=F32)


def _gemm_res(a, w, res, *, tm, tn):
    m, k = a.shape
    n = w.shape[1]
    return pl.pallas_call(
        _gemm_res_body,
        grid=(m // tm, n // tn),
        in_specs=[
            pl.BlockSpec((tm, k), lambda i, j: (i, 0)),
            pl.BlockSpec((k, tn), lambda i, j: (0, j)),
            pl.BlockSpec((tm, tn), lambda i, j: (i, j)),
        ],
        out_specs=pl.BlockSpec((tm, tn), lambda i, j: (i, j)),
        out_shape=jax.ShapeDtypeStruct((m, n), F32),
        compiler_params=_params("parallel", "arbitrary"),
        name="out_proj",
    )(a, w, res)


def _gemm_glu_body(y_ref, w_ref, z_ref, b_ref, o_ref, h_ref, *, tn):
    j = pl.program_id(1)

    @pl.when(j == 0)
    def _():
        h_ref[...] = y_ref[...].astype(BF16)

    sub = min(tn, GEMM_SUB)
    for c in range(tn // sub):
        cols = slice(c * sub, (c + 1) * sub)
        t = jnp.dot(h_ref[...], w_ref[:, cols], preferred_element_type=F32) + b_ref[:, cols]
        y = y_ref[:, pl.ds(pl.multiple_of(j * tn + c * sub, sub), sub)]
        z = z_ref[:, cols]
        o_ref[:, cols] = (y * z / ((1.0 + jnp.exp(-t)) * (1.0 + jnp.exp(-z)))).astype(o_ref.dtype)


def _gemm_glu(y, w, uz, bias, *, tm, tn):
    m, k = y.shape
    n = w.shape[1]
    zoff = n // tn
    return pl.pallas_call(
        functools.partial(_gemm_glu_body, tn=tn),
        grid=(m // tm, n // tn),
        in_specs=[
            _lhs_spec(tm, k, 4),
            pl.BlockSpec((k, tn), lambda i, j: (0, j)),
            pl.BlockSpec((tm, tn), lambda i, j: (i, zoff + j)),
            pl.BlockSpec((1, tn), lambda i, j: (0, j)),
        ],
        out_specs=pl.BlockSpec((tm, tn), lambda i, j: (i, j)),
        out_shape=jax.ShapeDtypeStruct((m, n), BF16),
        scratch_shapes=[pltpu.VMEM((tm, k), BF16)],
        compiler_params=_params("parallel", "arbitrary"),
        name="glu_proj",
    )(y, w, uz, bias.reshape(1, n))


def _gla_body(q_ref, k_ref, v_ref, gf_ref, wf_ref, bf_ref, s0_ref, o_ref, s_ref,
              qd_s, u_s, dec_s, st_s, *, chunk):
    c = chunk
    t = q_ref.shape[0]
    n = t // c
    grp = GLA_GROUP_ROWS
    cpg = grp // c
    shift = c.bit_length() - 1
    assert 1 << shift == c and t % grp == 0
    scale = GLA_DK ** -0.5
    r = lax.broadcasted_iota(I32, (grp, grp), 0)
    cc = lax.broadcasted_iota(I32, (grp, grp), 1)
    same = jnp.right_shift(r, shift) == jnp.right_shift(cc, shift)
    causal = jnp.logical_and(same, cc <= r)
    causal_b = causal.astype(BF16)
    wf = wf_ref[...].astype(BF16)
    bias = bf_ref[...]

    def group(i, carry):
        rows = pl.ds(pl.multiple_of(i * grp, grp), grp)
        gf = gf_ref[rows, :][:, :GLA_GATE_RANK]
        pre = jnp.dot(gf.astype(BF16), wf, preferred_element_type=F32) + bias
        g = _log_sigmoid(pre) / GLA_GATE_TAU
        g_hi = g.astype(BF16)
        g_md = (g - g_hi.astype(F32)).astype(BF16)
        g_lo = (g - g_hi.astype(F32) - g_md.astype(F32)).astype(BF16)
        b3 = jnp.dot(causal_b, jnp.concatenate([g_hi, g_md, g_lo], axis=1), preferred_element_type=F32)
        b = b3[:, :GLA_DK] + b3[:, GLA_DK:2 * GLA_DK] + b3[:, 2 * GLA_DK:]
        row_of = lambda r0: jnp.concatenate(
            [jnp.broadcast_to(b[j * c + r0:j * c + r0 + 1, :], (c, GLA_DK)) for j in range(cpg)], axis=0)
        bm = row_of(c // 2 - 1)
        be = row_of(c - 1)
        q = q_ref[rows, :] * scale
        k = k_ref[rows, :]
        v = v_ref[rows, :].astype(BF16)
        qe = (q * jnp.exp(b - bm)).astype(BF16)
        ke = (k * jnp.exp(bm - b)).astype(BF16)
        att = jnp.where(causal, _nt(qe, ke), 0.0)
        o_ref[rows, :] = jnp.dot(att.astype(BF16), v, preferred_element_type=F32)
        qd_s[rows, :] = (q * jnp.exp(b)).astype(BF16)
        kd = (k * jnp.exp(be - b)).astype(BF16)
        dec = jnp.exp(be)
        for j in range(cpg):
            cj = i * cpg + j
            u_s[cj] = _tn(v[j * c:(j + 1) * c, :], kd[j * c:(j + 1) * c, :])
            dec_s[cj] = dec[j * c:j * c + SUBLANES, :]
        return carry

    lax.fori_loop(0, t // grp, group, 0, unroll=True)

    def advance(cj, st):
        st_s[cj] = st.astype(BF16)
        return st * dec_s[cj][0:1, :] + u_s[cj]

    st = lax.fori_loop(0, n, advance, s0_ref[...].T, unroll=True)
    s_ref[...] = st.T

    def inter(cj, carry):
        rows = pl.ds(pl.multiple_of(cj * c, c), c)
        o_ref[rows, :] = o_ref[rows, :] + _nt(qd_s[rows, :], st_s[cj])
        return carry

    lax.fori_loop(0, n, inter, 0, unroll=True)


def _gla_small_body(q_ref, k_ref, v_ref, gf_ref, wf_ref, bf_ref, s0_ref, o_ref, s_ref):
    c = q_ref.shape[0]
    scale = GLA_DK ** -0.5
    q = q_ref[...] * scale
    k = k_ref[...]
    v = v_ref[...]
    pre = bf_ref[...] + jnp.zeros((c, GLA_DK), F32)
    for r in range(GLA_GATE_RANK):
        pre = pre + gf_ref[:, r:r + 1] * wf_ref[r:r + 1, :]
    g = _log_sigmoid(pre) / GLA_GATE_TAU
    row = lax.broadcasted_iota(I32, (c, GLA_DK), 0)
    b = jnp.zeros((c, GLA_DK), F32)
    for s in range(c):
        b = b + jnp.where(row >= s, g[s:s + 1, :], 0.0)
    be = b[c - 1:c, :]
    s0 = s0_ref[...]
    o = jnp.dot(q * jnp.exp(b), s0, preferred_element_type=F32)
    for s in range(c):
        e = jnp.exp(jnp.where(row >= s, b - b[s:s + 1, :], NEG_INF))
        a_col = jnp.sum(q * k[s:s + 1, :] * e, axis=-1, keepdims=True)
        o = o + a_col * v[s:s + 1, :]
    o_ref[...] = o
    kd = k * jnp.exp(be - b)
    pad = jnp.concatenate([kd, jnp.exp(be), jnp.zeros((LANES - c - 1, GLA_DK), F32)], axis=0)
    padt = pad.T
    s_new = s0 * padt[:, c:c + 1]
    for s in range(c):
        s_new = s_new + padt[:, s:s + 1] * v[s:s + 1, :]
    s_ref[...] = s_new


def _gla(p, gf, w_f2, b_f, s0, *, nb, t):
    small = t % GLA_CHUNK != 0
    kq, kk, kv = COL_GQ // GLA_DK, COL_GK // GLA_DK, COL_GV // GLA_DV
    in_specs = [
        pl.BlockSpec((t, GLA_DK), lambda b, h: (b, kq + h)),
        pl.BlockSpec((t, GLA_DK), lambda b, h: (b, kk + h)),
        pl.BlockSpec((t, GLA_DV), lambda b, h: (b, kv + h)),
        pl.BlockSpec((t, LANES), lambda b, h: (b, 0)),
        pl.BlockSpec((GLA_GATE_RANK, GLA_DK), lambda b, h: (0, h)),
        pl.BlockSpec((1, GLA_DK), lambda b, h: (0, h)),
        pl.BlockSpec((None, None, GLA_DK, GLA_DV), lambda b, h: (b, h, 0, 0)),
    ]
    out_specs = [
        pl.BlockSpec((t, GLA_DV), lambda b, h: (b, h)),
        pl.BlockSpec((None, None, GLA_DK, GLA_DV), lambda b, h: (b, h, 0, 0)),
    ]
    out_shape = [
        jax.ShapeDtypeStruct((nb * t, GLA_VW), F32),
        jax.ShapeDtypeStruct((nb, GLA_HEADS, GLA_DK, GLA_DV), F32),
    ]
    if small:
        body, scratch = _gla_small_body, []
    else:
        body = functools.partial(_gla_body, chunk=GLA_CHUNK)
        n = t // GLA_CHUNK
        scratch = [
            pltpu.VMEM((t, GLA_DK), BF16),
            pltpu.VMEM((n, GLA_DV, GLA_DK), F32),
            pltpu.VMEM((n, SUBLANES, GLA_DK), F32),
            pltpu.VMEM((n, GLA_DV, GLA_DK), BF16),
        ]
    return pl.pallas_call(
        body,
        grid=(nb, GLA_HEADS),
        in_specs=in_specs,
        out_specs=out_specs,
        out_shape=out_shape,
        scratch_shapes=scratch,
        compiler_params=_params("parallel", "parallel"),
        name="gla_small" if small else "gla",
    )(p, p, p, gf, w_f2, b_f.reshape(1, -1), s0)


def _moba_prompt_body(q_ref, k_ref, v_ref, o_ref, *, nblk):
    blk = MOBA_BLOCK
    t = nblk * blk
    shift = blk.bit_length() - 1
    assert 1 << shift == blk and nblk <= LANES and nblk % MOBA_PAIR == 0
    scale = MOBA_DH ** -0.5
    q = q_ref[...]
    k = k_ref[...]
    qb = (q * scale).astype(BF16)
    kb = k.astype(BF16)
    vb = v_ref[...].astype(BF16)

    km = jnp.concatenate([jnp.mean(k[n * blk:(n + 1) * blk, :], axis=0, keepdims=True) for n in range(nblk)],
                         axis=0)
    gate = lax.dot_general(km, q, (((1,), (1,)), ((), ())),
                           precision=lax.Precision.HIGHEST, preferred_element_type=F32)
    nrow = lax.broadcasted_iota(I32, (nblk, t), 0)
    qblk = jnp.right_shift(lax.broadcasted_iota(I32, (nblk, t), 1), shift)
    valid = nrow < qblk
    gm = jnp.where(valid, gate, NEG_INF)
    rank = jnp.zeros((nblk, t), I32)
    for m in range(nblk):
        g_m = gm[m:m + 1, :]
        beats = jnp.logical_or(g_m > gm, jnp.logical_and(g_m == gm, m < nrow))
        rank = rank + beats.astype(I32)
    sel = jnp.logical_and(valid, rank < MOBA_TOPK).astype(F32)
    sel_c = jnp.concatenate([sel, jnp.zeros((LANES - nblk, t), F32)], axis=0).T
    bias_c = jnp.where(sel_c > 0.0, 0.0, NEG_INF)

    causal = lax.broadcasted_iota(I32, (blk, blk), 1) <= lax.broadcasted_iota(I32, (blk, blk), 0)
    masked = jnp.full((blk, blk), NEG_INF, F32)
    pair = MOBA_PAIR
    for v in range(nblk // pair):
        lo = pair * v * blk
        nbi = pair * (v + 1)
        s = _nt(qb[lo:lo + pair * blk, :], kb[0:nbi * blk, :])
        bias = bias_c[lo:lo + pair * blk, :]
        pieces = []
        for n in range(nbi):
            sn = s[:, n * blk:(n + 1) * blk]
            if n < pair * v:
                pieces.append(sn + bias[:, n:n + 1])
                continue
            j = n - pair * v
            parts = []
            for i in range(pair):
                sni = sn[i * blk:(i + 1) * blk, :]
                if i == j:
                    parts.append(jnp.where(causal, sni, NEG_INF))
                elif i > j:
                    parts.append(sni + bias[i * blk:(i + 1) * blk, n:n + 1])
                else:
                    parts.append(masked)
            pieces.append(jnp.concatenate(parts, axis=0))
        s = jnp.concatenate(pieces, axis=1)
        m = jnp.max(s, axis=-1, keepdims=True)
        p = jnp.exp(s - m)
        l = jnp.sum(p, axis=-1, keepdims=True)
        acc = jnp.dot(p.astype(BF16), vb[0:nbi * blk, :], preferred_element_type=F32)
        o_ref[lo:lo + pair * blk, :] = acc / l


def _moba_prompt(p, *, nb, t):
    nblk = t // MOBA_BLOCK
    kq, kk, kv = COL_MQ // MOBA_DH, COL_MK // MOBA_DH, COL_MV // MOBA_DH
    return pl.pallas_call(
        functools.partial(_moba_prompt_body, nblk=nblk),
        grid=(nb, MOBA_HEADS),
        in_specs=[
            pl.BlockSpec((t, MOBA_DH), lambda b, h: (b, kq + h)),
            pl.BlockSpec((t, MOBA_DH), lambda b, h: (b, kk + h)),
            pl.BlockSpec((t, MOBA_DH), lambda b, h: (b, kv + h)),
        ],
        out_specs=pl.BlockSpec((t, MOBA_DH), lambda b, h: (b, h)),
        out_shape=jax.ShapeDtypeStruct((nb * t, MOBA_W), F32),
        compiler_params=_params("parallel", "parallel"),
        name="moba_prompt",
    )(p, p, p)


def _bmean_body(pt_ref, *refs, ppb):
    pages, o_ref = refs[:-1], refs[-1]
    for j in range(len(pages) // ppb):
        tot = jnp.sum(pages[j * ppb][...], axis=0)
        for r in pages[j * ppb + 1:(j + 1) * ppb]:
            tot = tot + jnp.sum(r[...], axis=0)
        o_ref[j] = tot * (1.0 / MOBA_BLOCK)


def _block_means(cache_k, page_table, layer):
    nb, n_pages = page_table.shape
    ppb = MOBA_BLOCK // PAGE_SIZE
    n_full = n_pages // ppb
    bps = BMEAN_BLOCKS_PER_STEP
    assert n_full % bps == 0
    pps = bps * ppb
    page_block = (None, None, PAGE_SIZE, MOBA_HEADS, MOBA_DH)
    page_spec = lambda j: pl.BlockSpec(page_block, lambda b, n, pt: (layer, pt[b, pps * n + j], 0, 0, 0))
    return pl.pallas_call(
        functools.partial(_bmean_body, ppb=ppb),
        grid_spec=pltpu.PrefetchScalarGridSpec(
            num_scalar_prefetch=1,
            grid=(nb, n_full // bps),
            in_specs=[page_spec(j) for j in range(pps)],
            out_specs=pl.BlockSpec((None, bps, MOBA_HEADS, MOBA_DH), lambda b, n, pt: (b, n, 0, 0)),
        ),
        out_shape=jax.ShapeDtypeStruct((nb, n_full, MOBA_HEADS, MOBA_DH), F32),
        compiler_params=_params("parallel", "arbitrary"),
        name="moba_block_means",
    )(page_table, *([cache_k] * pps))


def _gate_topk_body(q_ref, bm_ref, o_ref, *, td):
    n_full = bm_ref.shape[1]
    lane = lax.broadcasted_iota(I32, (n_full, LANES), 1)
    row = lax.broadcasted_iota(I32, (n_full, LANES), 0)
    orow = lax.broadcasted_iota(I32, (SUBLANES, LANES), 0)
    for h in range(MOBA_HEADS):
        km = bm_ref[h]
        g = jnp.full((n_full, LANES), NEG_INF, F32)
        for t in range(td):
            qv = q_ref[t:t + 1, h * MOBA_DH:(h + 1) * MOBA_DH]
            g = jnp.where(lane == t, jnp.sum(km * qv, axis=-1, keepdims=True), g)
        outv = jnp.zeros((SUBLANES, LANES), I32)
        for j in range(MOBA_TOPK):
            mx = jnp.max(g, axis=0, keepdims=True)
            idx = jnp.min(jnp.where(g == mx, row, n_full), axis=0, keepdims=True)
            g = jnp.where(row == idx, NEG_INF, g)
            outv = jnp.where(orow == j, idx, outv)
        o_ref[h] = outv


def _gate_topk(p, bmean_t, *, nb, td):
    n_full = bmean_t.shape[2]
    assert td <= SUBLANES and MOBA_TOPK <= SUBLANES
    kq = COL_MQ // MOBA_W
    return pl.pallas_call(
        functools.partial(_gate_topk_body, td=td),
        grid=(nb,),
        in_specs=[
            pl.BlockSpec((td, MOBA_W), lambda b: (b, kq)),
            pl.BlockSpec((None, MOBA_HEADS, n_full, MOBA_DH), lambda b: (b, 0, 0, 0)),
        ],
        out_specs=pl.BlockSpec((None, MOBA_HEADS, SUBLANES, LANES), lambda b: (b, 0, 0, 0)),
        out_shape=jax.ShapeDtypeStruct((nb, MOBA_HEADS, SUBLANES, LANES), I32),
        compiler_params=_params("parallel"),
        name="moba_gate_topk",
    )(p, bmean_t)


def _moba_sample_body(idx_ref, pt_ref, q_ref, k_ref, v_ref, ck_ref, cv_ref, o_ref, kbuf, vbuf, sem, *, layer, td):
    step = pl.program_id(0)
    nsteps = pl.num_programs(0)
    slot = lax.rem(step, 2)
    ppb = MOBA_BLOCK // PAGE_SIZE
    per_q = MOBA_TOPK * ppb
    nsel = MOBA_TOPK * MOBA_BLOCK
    scale = MOBA_DH ** -0.5
    slots = [(t, j, pg) for t in range(td) for j in range(MOBA_TOPK) for pg in range(ppb)]

    def copies(st, sl, t, j, pg):
        b = lax.div(st, MOBA_HEADS)
        h = lax.rem(st, MOBA_HEADS)
        blk = idx_ref[(st * td + t) * MOBA_TOPK + j]
        page = pt_ref[b, blk * ppb + pg]
        dst = pl.ds((t * per_q + j * ppb + pg) * PAGE_SIZE, PAGE_SIZE)
        return (pltpu.make_async_copy(ck_ref.at[layer, page, :, h, :], kbuf.at[sl, dst, :], sem.at[sl, 0]),
                pltpu.make_async_copy(cv_ref.at[layer, page, :, h, :], vbuf.at[sl, dst, :], sem.at[sl, 1]))

    def issue(st, sl):
        for c in slots:
            ck, cv = copies(st, sl, *c)
            ck.start()
            cv.start()

    @pl.when(step == 0)
    def _():
        issue(step, slot)

    @pl.when(step + 1 < nsteps)
    def _():
        issue(step + 1, 1 - slot)

    for c in slots:
        ck, cv = copies(step, slot, *c)
        ck.wait()
        cv.wait()

    rows = 2 * SUBLANES
    q = q_ref[...]
    qp = jnp.concatenate([q, jnp.zeros((rows - td, MOBA_DH), F32)], axis=0).astype(BF16)
    s_sel = _nt(qp, kbuf[slot].astype(BF16)) * scale
    r = lax.broadcasted_iota(I32, (rows, td * nsel), 0)
    c = lax.broadcasted_iota(I32, (rows, td * nsel), 1)
    mine = jnp.logical_and(c >= r * nsel, c < (r + 1) * nsel)
    s_sel = jnp.where(mine, s_sel, NEG_INF)

    k_new = k_ref[...]
    v_new = v_ref[...]
    rn = lax.broadcasted_iota(I32, (td, LANES), 0)
    cn = lax.broadcasted_iota(I32, (td, LANES), 1)
    s_new = jnp.full((td, LANES), NEG_INF, F32)
    for t in range(td):
        col = jnp.sum(q * k_new[t:t + 1, :], axis=-1, keepdims=True) * scale
        s_new = jnp.where(jnp.logical_and(cn == t, rn >= t), col, s_new)

    s_sel = s_sel[:td, :]
    m = jnp.maximum(jnp.max(s_sel, axis=-1, keepdims=True), jnp.max(s_new, axis=-1, keepdims=True))
    p_sel = jnp.exp(s_sel - m)
    p_new = jnp.exp(s_new - m)
    l = jnp.sum(p_sel, axis=-1, keepdims=True) + jnp.sum(p_new, axis=-1, keepdims=True)
    pp = jnp.concatenate([p_sel, jnp.zeros((rows - td, td * nsel), F32)], axis=0).astype(BF16)
    acc = jnp.dot(pp, vbuf[slot].astype(BF16), preferred_element_type=F32)[:td, :]
    for t in range(td):
        acc = acc + p_new[:, t:t + 1] * v_new[t:t + 1, :]
    o_ref[...] = acc / l


def _moba_sample(p, cache_k, cache_v, page_table, idx_flat, *, layer, nb, td):
    kq, kk, kv = COL_MQ // MOBA_DH, COL_MK // MOBA_DH, COL_MV // MOBA_DH
    nrows = td * MOBA_TOPK * MOBA_BLOCK
    row_spec = lambda off: pl.BlockSpec(
        (td, MOBA_DH), lambda s, idx, pt: (lax.div(s, MOBA_HEADS), off + lax.rem(s, MOBA_HEADS)))
    return pl.pallas_call(
        functools.partial(_moba_sample_body, layer=layer, td=td),
        grid_spec=pltpu.PrefetchScalarGridSpec(
            num_scalar_prefetch=2,
            grid=(nb * MOBA_HEADS,),
            in_specs=[
                row_spec(kq), row_spec(kk), row_spec(kv),
                pl.BlockSpec(memory_space=pl.ANY),
                pl.BlockSpec(memory_space=pl.ANY),
            ],
            out_specs=row_spec(0),
            scratch_shapes=[
                pltpu.VMEM((2, nrows, MOBA_DH), F32),
                pltpu.VMEM((2, nrows, MOBA_DH), F32),
                pltpu.SemaphoreType.DMA((2, 2)),
            ],
        ),
        out_shape=jax.ShapeDtypeStruct((nb * td, MOBA_W), F32),
        compiler_params=_params("arbitrary"),
        name="moba_sample",
    )(idx_flat, page_table, p, p, p, cache_k, cache_v)


def _merge_body(go_ref, gg_ref, mo_ref, mg_ref, on_ref, a_ref):
    on = on_ref[...]
    for h in range(GLA_HEADS):
        sl = slice(h * GLA_DV, (h + 1) * GLA_DV)
        g = _rms_rows(go_ref[:, sl], on) * _silu(gg_ref[:, sl])
        a_ref[:, sl] = g.astype(a_ref.dtype)
    a_ref[:, GLA_VW:] = (mo_ref[...] * _silu(mg_ref[...])).astype(a_ref.dtype)


def _merge(gla_o, moba_o, p, o_norm, *, tm):
    m = gla_o.shape[0]
    kg, km = COL_GG // GLA_VW, COL_MG // MOBA_W
    return pl.pallas_call(
        _merge_body,
        grid=(m // tm,),
        in_specs=[
            pl.BlockSpec((tm, GLA_VW), lambda i: (i, 0)),
            pl.BlockSpec((tm, GLA_VW), lambda i: (i, kg)),
            pl.BlockSpec((tm, MOBA_W), lambda i: (i, 0)),
            pl.BlockSpec((tm, MOBA_W), lambda i: (i, km)),
            pl.BlockSpec((1, GLA_DV), lambda i: (0, 0)),
        ],
        out_specs=pl.BlockSpec((tm, GLA_VW + MOBA_W), lambda i: (i, 0)),
        out_shape=jax.ShapeDtypeStruct((m, GLA_VW + MOBA_W), BF16),
        compiler_params=_params("parallel"),
        name="a_merge",
    )(gla_o, p, moba_o, p, o_norm.reshape(1, -1))


def _s5_coef_body(lr_ref, li_ref, ldt_ref, br_ref, bi_ref, bbr_ref, bbi_ref, abr_ref, abi_ref):
    lr = lr_ref[...]
    li = li_ref[...]
    dt = jnp.exp(ldt_ref[...])
    mag = jnp.exp(lr * dt)
    ab_re = mag * jnp.cos(li * dt)
    ab_im = mag * jnp.sin(li * dt)
    den = lr * lr + li * li
    nr = ab_re - 1.0
    cr = (nr * lr + ab_im * li) / den
    ci = (ab_im * lr - nr * li) / den
    br = br_ref[...]
    bi = bi_ref[...]
    bbr_ref[...] = cr * br - ci * bi
    bbi_ref[...] = cr * bi + ci * br
    abr_ref[...] = ab_re
    abi_ref[...] = ab_im


def _s5_coef(lam_re, lam_im, log_dt, b_re, b_im):
    g, p = lam_re.shape
    w = p * S5_GROUP
    rep = lambda a: jnp.repeat(a, S5_GROUP, axis=1)
    full = pl.BlockSpec((g, w), lambda: (0, 0))
    outs = pl.pallas_call(
        _s5_coef_body,
        in_specs=[full] * 5,
        out_specs=[full] * 4,
        out_shape=[jax.ShapeDtypeStruct((g, w), F32)] * 4,
        name="s5_coef",
    )(rep(lam_re), rep(lam_im), jnp.broadcast_to(log_dt[:, None], (g, w)),
      b_re.reshape(g, w), b_im.reshape(g, w))
    bbr, bbi, abr, abi = outs
    return (bbr.reshape(g, p, S5_GROUP), bbi.reshape(g, p, S5_GROUP),
            abr[:, ::S5_GROUP], abi[:, ::S5_GROUP])


def _s5_expand(w2):
    per_tile = LANES // S5_P
    rg = jnp.right_shift(lax.broadcasted_iota(I32, (S5_SLAB, LANES), 0), S5_GROUP.bit_length() - 1)
    lg = jnp.right_shift(lax.broadcasted_iota(I32, (S5_SLAB, LANES), 1), S5_P.bit_length() - 1)
    pieces = []
    for part in range(2):
        w = w2[:, part * LANES:(part + 1) * LANES]
        for kk in range(S5_SLAB_STATES // LANES):
            pieces.append(jnp.where(rg == per_tile * kk + lg, w, 0.0))
    return jnp.concatenate(pieces, axis=1)


def _s5_body(u_ref, bb_ref, cc_ref, ar_ref, ai_ref, d_ref, x0r_ref, x0i_ref,
             y_ref, xr_ref, xi_ref, lhs, bu, ybuf, cr, ci, bd_in, bd_out, *, ns, nt):
    ti = pl.program_id(1)
    tpv = SUBLANES // ns
    nst = S5_SLAB_STATES
    lo_rows = SUBLANES - ns
    nlb = S5_SLAB // LANES

    @pl.when(ti == 0)
    def _():
        cr[...] = jnp.zeros_like(cr)
        ci[...] = jnp.zeros_like(ci)
        cr[lo_rows:, :] = x0r_ref[...]
        ci[lo_rows:, :] = x0i_ref[...]
        e_in = _s5_expand(bb_ref[...])
        bd_in[:S5_SLAB, :] = e_in.astype(BF16)
        if tpv == 2:
            e_re, e_im = e_in[:, :nst], e_in[:, nst:]
            a_re, a_im = ar_ref[...], ai_ref[...]
            bd_in[S5_SLAB:, :] = jnp.concatenate([a_re * e_re - a_im * e_im, a_re * e_im + a_im * e_re],
                                                 axis=1).astype(BF16)
        bd_out[...] = _s5_expand(cc_ref[...]).astype(BF16)

    for b in range(ns):
        for j in range(nlb):
            lhs[j, pl.ds(b, nt, stride=ns), :] = u_ref[b, :, j * LANES:(j + 1) * LANES]
    u_rows = jnp.concatenate([lhs[j] for j in range(nlb)], axis=1)
    if tpv == 2:
        tbit = jnp.right_shift(lax.broadcasted_iota(I32, u_rows.shape, 0), ns.bit_length() - 1)
        u_prev = jnp.where(jnp.bitwise_and(tbit, 1) == 1, pltpu.roll(u_rows, ns, 0), 0.0)
        lhs_rows = jnp.concatenate([u_rows, u_prev], axis=1)
    else:
        lhs_rows = u_rows
    bu[...] = jnp.dot(lhs_rows.astype(BF16), bd_in[...], preferred_element_type=F32)

    ar = ar_ref[...]
    ai = ai_ref[...]
    if tpv == 1:
        def step(v, carry):
            xr, xi = carry
            sl = pl.ds(pl.multiple_of(v * SUBLANES, SUBLANES), SUBLANES)
            nxr = ar * xr - ai * xi + bu[sl, :nst]
            nxi = ar * xi + ai * xr + bu[sl, nst:]
            bu[sl, :nst] = nxr
            bu[sl, nst:] = nxi
            return nxr, nxi
    else:
        assert tpv == 2
        lo = lax.broadcasted_iota(I32, (SUBLANES, nst), 0) < ns
        c2r = jnp.where(lo, ar, ar * ar - ai * ai)
        c2i = jnp.where(lo, ai, 2.0 * ar * ai)

        def step(v, carry):
            xr, xi = carry
            sl = pl.ds(pl.multiple_of(v * SUBLANES, SUBLANES), SUBLANES)
            pr = jnp.where(lo, pltpu.roll(xr, ns, 0), xr)
            pi = jnp.where(lo, pltpu.roll(xi, ns, 0), xi)
            nxr = bu[sl, :nst] + c2r * pr - c2i * pi
            nxi = bu[sl, nst:] + c2r * pi + c2i * pr
            bu[sl, :nst] = nxr
            bu[sl, nst:] = nxi
            return nxr, nxi

    nv = (nt * ns) // SUBLANES
    xr, xi = lax.fori_loop(0, nv, step, (cr[...], ci[...]), unroll=True)
    cr[...] = xr
    ci[...] = xi

    yv = _nt(bu[...].astype(BF16), bd_out[...]) + d_ref[...] * u_rows
    yv = 0.5 * yv * (1.0 + lax.erf(yv * math.sqrt(0.5)))
    for j in range(nlb):
        ybuf[j] = yv[:, j * LANES:(j + 1) * LANES]
    for b in range(ns):
        for j in range(nlb):
            y_ref[b, :, j * LANES:(j + 1) * LANES] = ybuf[j, pl.ds(b, nt, stride=ns), :]

    @pl.when(ti == pl.num_programs(1) - 1)
    def _():
        xr_ref[...] = cr[lo_rows:, :]
        xi_ref[...] = ci[lo_rows:, :]


def _s5_core(uz, bb, cc, ab_re, ab_im, d, x0_re, x0_im, *, ns, t, nt):
    w = d.shape[0]
    nslab = w // S5_SLAB
    nst = S5_SLAB_STATES
    assert SUBLANES % ns == 0 and t % nt == 0 and (nt * ns) % SUBLANES == 0
    u3 = uz.reshape(ns, t, uz.shape[1])
    to_slab = lambda a: jnp.transpose(a.reshape(ns, nslab, nst), (1, 0, 2))
    st_spec = pl.BlockSpec((None, ns, nst), lambda s, i: (s, 0, 0))
    y, xr, xi = pl.pallas_call(
        functools.partial(_s5_body, ns=ns, nt=nt),
        grid=(nslab, t // nt),
        in_specs=[
            pl.BlockSpec((ns, nt, S5_SLAB), lambda s, i: (0, i, s)),
            pl.BlockSpec((None, S5_SLAB, 2 * LANES), lambda s, i: (s, 0, 0)),
            pl.BlockSpec((None, S5_SLAB, 2 * LANES), lambda s, i: (s, 0, 0)),
            pl.BlockSpec((None, 1, nst), lambda s, i: (s, 0, 0)),
            pl.BlockSpec((None, 1, nst), lambda s, i: (s, 0, 0)),
            pl.BlockSpec((1, S5_SLAB), lambda s, i: (0, s)),
            st_spec, st_spec,
        ],
        out_specs=[
            pl.BlockSpec((ns, nt, S5_SLAB), lambda s, i: (0, i, s)),
            st_spec, st_spec,
        ],
        out_shape=[
            jax.ShapeDtypeStruct((ns, t, w), F32),
            jax.ShapeDtypeStruct((nslab, ns, nst), F32),
            jax.ShapeDtypeStruct((nslab, ns, nst), F32),
        ],
        scratch_shapes=[
            pltpu.VMEM((S5_SLAB // LANES, nt * ns, LANES), F32),
            pltpu.VMEM((nt * ns, 2 * nst), F32),
            pltpu.VMEM((S5_SLAB // LANES, nt * ns, LANES), F32),
            pltpu.VMEM((SUBLANES, nst), F32),
            pltpu.VMEM((SUBLANES, nst), F32),
            pltpu.VMEM((S5_SLAB * (SUBLANES // ns), 2 * nst), BF16),
            pltpu.VMEM((S5_SLAB, 2 * nst), BF16),
        ],
        compiler_params=_params("parallel", "arbitrary"),
        name="s5_core",
    )(u3, bb, cc, ab_re.reshape(nslab, 1, nst), ab_im.reshape(nslab, 1, nst), d.reshape(1, w),
      to_slab(x0_re), to_slab(x0_im))
    g = w // S5_GROUP
    from_slab = lambda a: jnp.transpose(a, (1, 0, 2)).reshape(ns, g, S5_P)
    return y.reshape(ns * t, w), from_slab(xr), from_slab(xi)


def _s5_compact(re, im):
    reps = LANES // S5_P
    flat = lambda a: jnp.tile(a.reshape(-1, S5_SLAB, S5_P), (1, 1, reps))
    return jnp.concatenate([flat(re), flat(im)], axis=2)


def _rope_tables(pos, reps):
    half = MOBA_DH // 2
    inv_freq = ROPE_THETA ** (-jnp.arange(half, dtype=F32) / half)
    ang = pos.astype(F32)[:, None] * inv_freq[None, :]
    cos = jnp.cos(ang)
    sin = jnp.sin(ang)
    cos2 = jnp.concatenate([cos, cos], axis=-1)
    sin2 = jnp.concatenate([-sin, sin], axis=-1)
    return jnp.tile(cos2, (reps, 1)), jnp.tile(sin2, (reps, 1))


def _layer_a(x, pos, nb, t, s0, weights, sample_ctx):
    norm, w_main, w_gf, w_f2, b_f, o_norm, q_norm, k_norm, w_out = weights
    m = nb * t
    tm = min(m, GEMM_TM)
    cos, sin = _rope_tables(pos, nb)
    p, gf = _aproj(x, norm, w_main, w_gf, q_norm, k_norm, cos, sin, tm=tm, tn=512)
    gla_o, state = _gla(p, gf, w_f2, b_f, s0, nb=nb, t=t)
    if sample_ctx is None:
        moba_o = _moba_prompt(p, nb=nb, t=t)
    else:
        cache_k, cache_v, page_table, layer = sample_ctx
        bmean = _block_means(cache_k, page_table, layer)
        ids = _gate_topk(p, jnp.transpose(bmean, (0, 2, 1, 3)), nb=nb, td=t)
        idx_flat = jnp.transpose(ids[:, :, :MOBA_TOPK, :t], (0, 1, 3, 2)).reshape(-1)
        moba_o = _moba_sample(p, cache_k, cache_v, page_table, idx_flat, layer=layer, nb=nb, td=t)
    a = _merge(gla_o, moba_o, p, o_norm, tm=min(m, 256))
    x_new = _gemm_res(a, w_out, x, tm=tm, tn=512)
    mk = p[:, COL_MK:COL_MK + MOBA_W].reshape(nb, t, MOBA_HEADS, MOBA_DH)
    mv = p[:, COL_MV:COL_MV + MOBA_W].reshape(nb, t, MOBA_HEADS, MOBA_DH)
    return x_new, mk, mv, state


def _layer_c(x, ns, t, x0_re, x0_im, weights):
    norm, w_in, bb, cc, ab_re, ab_im, d, w_glu, b_glu, w_out = weights
    m = ns * t
    tm = min(m, GEMM_TM)
    uz = _gemm_norm(x, norm, w_in, tm=tm, tn=512)
    y, xr, xi = _s5_core(uz, bb, cc, ab_re, ab_im, d, x0_re, x0_im, ns=ns, t=t, nt=min(t, S5_TIME_TILE))
    v = _gemm_glu(y, w_glu, uz, b_glu, tm=tm, tn=512)
    x_new = _gemm_res(v, w_out, x, tm=tm, tn=512)
    return x_new, xr, xi


def kernel(x_prompt, x_sample, cache_k, cache_v, state_gla, state_s5_re, state_s5_im, page_table, norm_a, w_in_a, w_gla_f2, b_gla_f, gla_out_norm, moba_q_norm, moba_k_norm, w_out_a, norm_c, w_in_c, s5_lambda_re, s5_lambda_im, s5_log_dt, s5_b_re, s5_b_im, s5_c_re, s5_c_im, s5_d, w_glu, b_glu, w_out_c):
    nbp, tp, d = x_prompt.shape
    nbs, ts, _ = x_sample.shape
    depth = norm_a.shape[0] + norm_c.shape[0]
    past_len = page_table.shape[1] * PAGE_SIZE
    assert past_len % MOBA_BLOCK == 0 and past_len // MOBA_BLOCK >= MOBA_TOPK
    assert tp % MOBA_BLOCK == 0 and tp % GLA_CHUNK == 0
    pos_p = jnp.arange(tp)
    pos_s = past_len + jnp.arange(ts)
    xp = x_prompt.reshape(nbp * tp, d)
    xs = x_sample.reshape(nbs * ts, d)
    g = d // S5_GROUP
    outs = [[] for _ in range(10)]
    for layer in range(depth):
        i = layer // 2
        if layer % 2 == 0:
            w_main, w_gf = _wprep(jnp.swapaxes(w_in_a[i], 0, 1))
            weights = (norm_a[i], w_main, w_gf, w_gla_f2[i], b_gla_f[i], gla_out_norm[i], moba_q_norm[i],
                       moba_k_norm[i], w_out_a[i].astype(BF16))
            zero_state = jnp.zeros((nbp, GLA_HEADS, GLA_DK, GLA_DV), F32)
            xp, mk, mv, sp = _layer_a(xp, pos_p, nbp, tp, zero_state, weights, None)
            outs[0].append(mk)
            outs[1].append(mv)
            outs[4].append(sp)
            xs, mk, mv, ss = _layer_a(xs, pos_s, nbs, ts, state_gla[i], weights,
                                      (cache_k, cache_v, page_table, i))
            outs[2].append(mk)
            outs[3].append(mv)
            outs[5].append(ss)
        else:
            bb_re, bb_im, ab_re, ab_im = _s5_coef(s5_lambda_re[i], s5_lambda_im[i], s5_log_dt[i],
                                                  s5_b_re[i], s5_b_im[i])
            bb = _s5_compact(jnp.swapaxes(bb_re, 1, 2), jnp.swapaxes(bb_im, 1, 2))
            cc = _s5_compact(s5_c_re[i], -s5_c_im[i])
            weights = (norm_c[i], w_in_c[i].astype(BF16), bb, cc, ab_re, ab_im, s5_d[i],
                       w_glu[i].astype(BF16), b_glu[i], w_out_c[i].astype(BF16))
            zeros = jnp.zeros((nbp, g, S5_P), F32)
            xp, xr, xi = _layer_c(xp, nbp, tp, zeros, zeros, weights)
            outs[6].append(xr)
            outs[7].append(xi)
            xs, xr, xi = _layer_c(xs, nbs, ts, state_s5_re[i], state_s5_im[i], weights)
            outs[8].append(xr)
            outs[9].append(xi)
    pk, pv, sk, sv, pg, sg, psr, psi, ssr, ssi = [jnp.stack(o) for o in outs]
    return (xp.reshape(nbp, tp, d), xs.reshape(nbs, ts, d), pk, pv, sk, sv, pg, sg, psr, psi, ssr, ssi)
```

```python
import functools
import math

import jax
import jax.numpy as jnp
from jax import lax
from jax.experimental import pallas as pl
from jax.experimental.pallas import tpu as pltpu

F32 = jnp.float32
BF16 = jnp.bfloat16
I32 = jnp.int32

RMS_EPS = 1e-6
GLA_HEADS = 8
GLA_DK = 128
GLA_DV = 256
GLA_KW = GLA_HEADS * GLA_DK
GLA_VW = GLA_HEADS * GLA_DV
GLA_GATE_RANK = 16
GLA_GATE_TAU = 16.0
GLA_CHUNK = 64
GLA_GROUP_ROWS = 256
MOBA_HEADS = 16
MOBA_DH = 128
MOBA_W = MOBA_HEADS * MOBA_DH
MOBA_BLOCK = 256
MOBA_TOPK = 3
MOBA_PAIR = 1
ROPE_THETA = 10000.0
PAGE_SIZE = 128
BMEAN_BLOCKS_PER_STEP = 4
S5_GROUP = 16
S5_P = 64
S5_SLAB = 128
S5_SLAB_STATES = (S5_SLAB // S5_GROUP) * S5_P
S5_TIME_TILE = 1024

LANES = 128
SUBLANES = 8
VMEM_LIMIT = 52 * 1024 * 1024
NEG_INF = float("-inf")

COL_GQ = 0
COL_GK = COL_GQ + GLA_KW
COL_GV = COL_GK + GLA_KW
COL_GG = COL_GV + GLA_VW
COL_MQ = COL_GG + GLA_VW
COL_MK = COL_MQ + MOBA_W
COL_MV = COL_MK + MOBA_W
COL_MG = COL_MV + MOBA_W
A_COLS = COL_MG + MOBA_W
WPREP_ROWS = 256
GEMM_TM = 1024
EPILOGUE_ROWS = 1024
GEMM_SUB = 256
LHS_DOUBLE_BUFFER_MAX_BYTES = 8 * 1024 * 1024


def _params(*sem):
    return pltpu.CompilerParams(dimension_semantics=sem, vmem_limit_bytes=VMEM_LIMIT)


def _nt(a, b):
    return lax.dot_general(a, b, (((1,), (1,)), ((), ())), preferred_element_type=F32)


def _tn(a, b):
    return lax.dot_general(a, b, (((0,), (0,)), ((), ())), preferred_element_type=F32)


def _silu(x):
    return x / (1.0 + jnp.exp(-x))


def _log_sigmoid(x):
    return jnp.minimum(x, 0.0) - jnp.log(1.0 + jnp.exp(-jnp.abs(x)))


def _rms_rows(x, g):
    ms = jnp.mean(x * x, axis=-1, keepdims=True)
    return x * lax.rsqrt(ms + RMS_EPS) * g


def _lhs_spec(tm, k, itemsize):
    if tm * k * itemsize > LHS_DOUBLE_BUFFER_MAX_BYTES:
        return pl.BlockSpec((tm, k), lambda i, j: (i, 0), pipeline_mode=pl.Buffered(1))
    return pl.BlockSpec((tm, k), lambda i, j: (i, 0))


def _wprep_body(a_ref, b_ref, om_ref, of_ref, *, gate_blk):
    i = pl.program_id(0)
    gr = GLA_GATE_RANK

    @pl.when(i == 0)
    def _():
        of_ref[...] = jnp.zeros_like(of_ref)

    @pl.when(i < gate_blk)
    def _():
        om_ref[...] = a_ref[...].astype(BF16)

    @pl.when(i == gate_blk)
    def _():
        of_ref[:gr, :] = a_ref[:gr, :].astype(BF16)

    @pl.when(i >= gate_blk)
    def _():
        om_ref[:-gr, :] = a_ref[gr:, :].astype(BF16)
        om_ref[-gr:, :] = b_ref[...].astype(BF16)


def _wprep(wt):
    n, d = wt.shape
    tk = WPREP_ROWS
    gr = GLA_GATE_RANK
    gate_lo = COL_GG + GLA_VW
    assert n == A_COLS + gr and gate_lo % tk == 0 and A_COLS % tk == 0 and tk % gr == 0
    last_tail = (n - gr) // gr
    return pl.pallas_call(
        functools.partial(_wprep_body, gate_blk=gate_lo // tk),
        grid=(A_COLS // tk,),
        in_specs=[
            pl.BlockSpec((tk, d), lambda i: (i, 0)),
            pl.BlockSpec((gr, d), lambda i: (jnp.minimum((i + 1) * (tk // gr), last_tail), 0)),
        ],
        out_specs=[pl.BlockSpec((tk, d), lambda i: (i, 0)), pl.BlockSpec((LANES, d), lambda i: (0, 0))],
        out_shape=[jax.ShapeDtypeStruct((A_COLS, d), BF16), jax.ShapeDtypeStruct((LANES, d), BF16)],
        compiler_params=_params("arbitrary"),
        name="a_weight_prep",
    )(wt, wt)


def _aproj_body(x_ref, g_ref, w_ref, wf_ref, qn_ref, kn_ref, cos_ref, sin_ref, o_ref, gf_ref, h_ref, *, tn):
    j = pl.program_id(1)

    @pl.when(j == 0)
    def _():
        h_ref[...] = _rms_rows(x_ref[...], g_ref[...]).astype(BF16)
        gf_ref[...] = _nt(h_ref[...], wf_ref[...])

    o_ref[...] = _nt(h_ref[...], w_ref[...])

    q_lo, k_lo, k_hi = COL_MQ // tn, COL_MK // tn, COL_MV // tn

    @pl.when(jnp.logical_and(j >= q_lo, j < k_hi))
    def _():
        gain = jnp.where(j < k_lo, qn_ref[...], kn_ref[...])
        rb = min(o_ref.shape[0], EPILOGUE_ROWS)

        def rows_block(r, carry):
            rows = pl.ds(pl.multiple_of(r * rb, rb), rb)
            cos = cos_ref[rows, :]
            sin = sin_ref[rows, :]
            for hh in range(tn // MOBA_DH):
                cols = slice(hh * MOBA_DH, (hh + 1) * MOBA_DH)
                y = _rms_rows(o_ref[rows, cols], gain)
                o_ref[rows, cols] = y * cos + pltpu.roll(y, MOBA_DH // 2, 1) * sin
            return carry

        lax.fori_loop(0, o_ref.shape[0] // rb, rows_block, 0)


def _aproj(x, g, w_main, w_gf, qn, kn, cos, sin, *, tm, tn):
    m, d = x.shape
    n = w_main.shape[0]
    assert n == A_COLS and COL_MQ % tn == 0 and COL_MK % tn == 0 and COL_MV % tn == 0
    return pl.pallas_call(
        functools.partial(_aproj_body, tn=tn),
        grid=(m // tm, n // tn),
        in_specs=[
            _lhs_spec(tm, d, 4),
            pl.BlockSpec((1, d), lambda i, j: (0, 0)),
            pl.BlockSpec((tn, d), lambda i, j: (j, 0)),
            pl.BlockSpec((LANES, d), lambda i, j: (0, 0)),
            pl.BlockSpec((1, MOBA_DH), lambda i, j: (0, 0)),
            pl.BlockSpec((1, MOBA_DH), lambda i, j: (0, 0)),
            pl.BlockSpec((tm, MOBA_DH), lambda i, j: (i, 0)),
            pl.BlockSpec((tm, MOBA_DH), lambda i, j: (i, 0)),
        ],
        out_specs=[
            pl.BlockSpec((tm, tn), lambda i, j: (i, j)),
            pl.BlockSpec((tm, LANES), lambda i, j: (i, 0)),
        ],
        out_shape=[jax.ShapeDtypeStruct((m, n), F32), jax.ShapeDtypeStruct((m, LANES), F32)],
        scratch_shapes=[pltpu.VMEM((tm, d), BF16)],
        compiler_params=_params("parallel", "arbitrary"),
        name="a_proj",
    )(x, g.reshape(1, d), w_main, w_gf, qn.reshape(1, -1), kn.reshape(1, -1), cos, sin)


def _gemm_norm_body(x_ref, g_ref, w_ref, o_ref, h_ref):
    @pl.when(pl.program_id(1) == 0)
    def _():
        h_ref[...] = _rms_rows(x_ref[...], g_ref[...]).astype(BF16)

    o_ref[...] = jnp.dot(h_ref[...], w_ref[...], preferred_element_type=F32)


def _gemm_norm(x, g, w, *, tm, tn):
    m, d = x.shape
    n = w.shape[1]
    return pl.pallas_call(
        _gemm_norm_body,
        grid=(m // tm, n // tn),
        in_specs=[
            _lhs_spec(tm, d, 4),
            pl.BlockSpec((1, d), lambda i, j: (0, 0)),
            pl.BlockSpec((d, tn), lambda i, j: (0, j)),
        ],
        out_specs=pl.BlockSpec((tm, tn), lambda i, j: (i, j)),
        out_shape=jax.ShapeDtypeStruct((m, n), F32),
        scratch_shapes=[pltpu.VMEM((tm, d), BF16)],
        compiler_params=_params("parallel", "arbitrary"),
        name="c_proj",
    )(x, g.reshape(1, d), w)


def _gemm_res_body(a_ref, w_ref, r_ref, o_ref):
    o_ref[...] = r_ref[...] + jnp.dot(a_ref[...], w_ref[...], preferred_element_type=F32)


def _gemm_res(a, w, res, *, tm, tn):
    m, k = a.shape
    n = w.shape[1]
    return pl.pallas_call(
        _gemm_res_body,
        grid=(m // tm, n // tn),
        in_specs=[
            pl.BlockSpec((tm, k), lambda i, j: (i, 0)),
            pl.BlockSpec((k, tn), lambda i, j: (0, j)),
            pl.BlockSpec((tm, tn), lambda i, j: (i, j)),
        ],
        out_specs=pl.BlockSpec((tm, tn), lambda i, j: (i, j)),
        out_shape=jax.ShapeDtypeStruct((m, n), F32),
        compiler_params=_params("parallel", "arbitrary"),
        name="out_proj",
    )(a, w, res)


def _gemm_glu_body(y_ref, w_ref, z_ref, b_ref, o_ref, h_ref, *, tn):
    j = pl.program_id(1)

    @pl.when(j == 0)
    def _():
        h_ref[...] = y_ref[...].astype(BF16)

    sub = min(tn, GEMM_SUB)
    for c in range(tn // sub):
        cols = slice(c * sub, (c + 1) * sub)
        t = jnp.dot(h_ref[...], w_ref[:, cols], preferred_element_type=F32) + b_ref[:, cols]
        y = y_ref[:, pl.ds(pl.multiple_of(j * tn + c * sub, sub), sub)]
        z = z_ref[:, cols]
        o_ref[:, cols] = (y * z / ((1.0 + jnp.exp(-t)) * (1.0 + jnp.exp(-z)))).astype(o_ref.dtype)


def _gemm_glu(y, w, uz, bias, *, tm, tn):
    m, k = y.shape
    n = w.shape[1]
    zoff = n // tn
    return pl.pallas_call(
        functools.partial(_gemm_glu_body, tn=tn),
        grid=(m // tm, n // tn),
        in_specs=[
            _lhs_spec(tm, k, 4),
            pl.BlockSpec((k, tn), lambda i, j: (0, j)),
            pl.BlockSpec((tm, tn), lambda i, j: (i, zoff + j)),
            pl.BlockSpec((1, tn), lambda i, j: (0, j)),
        ],
        out_specs=pl.BlockSpec((tm, tn), lambda i, j: (i, j)),
        out_shape=jax.ShapeDtypeStruct((m, n), BF16),
        scratch_shapes=[pltpu.VMEM((tm, k), BF16)],
        compiler_params=_params("parallel", "arbitrary"),
        name="glu_proj",
    )(y, w, uz, bias.reshape(1, n))


def _gla_body(q_ref, k_ref, v_ref, gf_ref, wf_ref, bf_ref, s0_ref, o_ref, s_ref,
              qd_s, u_s, dec_s, st_s, *, chunk):
    c = chunk
    t = q_ref.shape[0]
    n = t // c
    grp = GLA_GROUP_ROWS
    cpg = grp // c
    shift = c.bit_length() - 1
    assert 1 << shift == c and t % grp == 0
    scale = GLA_DK ** -0.5
    r = lax.broadcasted_iota(I32, (grp, grp), 0)
    cc = lax.broadcasted_iota(I32, (grp, grp), 1)
    same = jnp.right_shift(r, shift) == jnp.right_shift(cc, shift)
    causal = jnp.logical_and(same, cc <= r)
    causal_b = causal.astype(BF16)
    wf = wf_ref[...].astype(BF16)
    bias = bf_ref[...]

    def group(i, carry):
        rows = pl.ds(pl.multiple_of(i * grp, grp), grp)
        gf = gf_ref[rows, :][:, :GLA_GATE_RANK]
        pre = jnp.dot(gf.astype(BF16), wf, preferred_element_type=F32) + bias
        g = _log_sigmoid(pre) / GLA_GATE_TAU
        g_hi = g.astype(BF16)
        g_md = (g - g_hi.astype(F32)).astype(BF16)
        g_lo = (g - g_hi.astype(F32) - g_md.astype(F32)).astype(BF16)
        b3 = jnp.dot(causal_b, jnp.concatenate([g_hi, g_md, g_lo], axis=1), preferred_element_type=F32)
        b = b3[:, :GLA_DK] + b3[:, GLA_DK:2 * GLA_DK] + b3[:, 2 * GLA_DK:]
        row_of = lambda r0: jnp.concatenate(
            [jnp.broadcast_to(b[j * c + r0:j * c + r0 + 1, :], (c, GLA_DK)) for j in range(cpg)], axis=0)
        bm = row_of(c // 2 - 1)
        be = row_of(c - 1)
        q = q_ref[rows, :] * scale
        k = k_ref[rows, :]
        v = v_ref[rows, :].astype(BF16)
        qe = (q * jnp.exp(b - bm)).astype(BF16)
        ke = (k * jnp.exp(bm - b)).astype(BF16)
        att = jnp.where(causal, _nt(qe, ke), 0.0)
        o_ref[rows, :] = jnp.dot(att.astype(BF16), v, preferred_element_type=F32)
        qd_s[rows, :] = (q * jnp.exp(b)).astype(BF16)
        kd = (k * jnp.exp(be - b)).astype(BF16)
        dec = jnp.exp(be)
        for j in range(cpg):
            cj = i * cpg + j
            u_s[cj] = _tn(v[j * c:(j + 1) * c, :], kd[j * c:(j + 1) * c, :])
            dec_s[cj] = dec[j * c:j * c + SUBLANES, :]
        return carry

    lax.fori_loop(0, t // grp, group, 0, unroll=True)

    def advance(cj, st):
        st_s[cj] = st.astype(BF16)
        return st * dec_s[cj][0:1, :] + u_s[cj]

    st = lax.fori_loop(0, n, advance, s0_ref[...].T, unroll=True)
    s_ref[...] = st.T

    def inter(cj, carry):
        rows = pl.ds(pl.multiple_of(cj * c, c), c)
        o_ref[rows, :] = o_ref[rows, :] + _nt(qd_s[rows, :], st_s[cj])
        return carry

    lax.fori_loop(0, n, inter, 0, unroll=True)


def _gla_small_body(q_ref, k_ref, v_ref, gf_ref, wf_ref, bf_ref, s0_ref, o_ref, s_ref):
    c = q_ref.shape[0]
    scale = GLA_DK ** -0.5
    q = q_ref[...] * scale
    k = k_ref[...]
    v = v_ref[...]
    pre = bf_ref[...] + jnp.zeros((c, GLA_DK), F32)
    for r in range(GLA_GATE_RANK):
        pre = pre + gf_ref[:, r:r + 1] * wf_ref[r:r + 1, :]
    g = _log_sigmoid(pre) / GLA_GATE_TAU
    row = lax.broadcasted_iota(I32, (c, GLA_DK), 0)
    b = jnp.zeros((c, GLA_DK), F32)
    for s in range(c):
        b = b + jnp.where(row >= s, g[s:s + 1, :], 0.0)
    be = b[c - 1:c, :]
    s0 = s0_ref[...]
    o = jnp.dot(q * jnp.exp(b), s0, preferred_element_type=F32)
    for s in range(c):
        e = jnp.exp(jnp.where(row >= s, b - b[s:s + 1, :], NEG_INF))
        a_col = jnp.sum(q * k[s:s + 1, :] * e, axis=-1, keepdims=True)
        o = o + a_col * v[s:s + 1, :]
    o_ref[...] = o
    kd = k * jnp.exp(be - b)
    pad = jnp.concatenate([kd, jnp.exp(be), jnp.zeros((LANES - c - 1, GLA_DK), F32)], axis=0)
    padt = pad.T
    s_new = s0 * padt[:, c:c + 1]
    for s in range(c):
        s_new = s_new + padt[:, s:s + 1] * v[s:s + 1, :]
    s_ref[...] = s_new


def _gla(p, gf, w_f2, b_f, s0, *, nb, t):
    small = t % GLA_CHUNK != 0
    kq, kk, kv = COL_GQ // GLA_DK, COL_GK // GLA_DK, COL_GV // GLA_DV
    in_specs = [
        pl.BlockSpec((t, GLA_DK), lambda b, h: (b, kq + h)),
        pl.BlockSpec((t, GLA_DK), lambda b, h: (b, kk + h)),
        pl.BlockSpec((t, GLA_DV), lambda b, h: (b, kv + h)),
        pl.BlockSpec((t, LANES), lambda b, h: (b, 0)),
        pl.BlockSpec((GLA_GATE_RANK, GLA_DK), lambda b, h: (0, h)),
        pl.BlockSpec((1, GLA_DK), lambda b, h: (0, h)),
        pl.BlockSpec((None, None, GLA_DK, GLA_DV), lambda b, h: (b, h, 0, 0)),
    ]
    out_specs = [
        pl.BlockSpec((t, GLA_DV), lambda b, h: (b, h)),
        pl.BlockSpec((None, None, GLA_DK, GLA_DV), lambda b, h: (b, h, 0, 0)),
    ]
    out_shape = [
        jax.ShapeDtypeStruct((nb * t, GLA_VW), F32),
        jax.ShapeDtypeStruct((nb, GLA_HEADS, GLA_DK, GLA_DV), F32),
    ]
    if small:
        body, scratch = _gla_small_body, []
    else:
        body = functools.partial(_gla_body, chunk=GLA_CHUNK)
        n = t // GLA_CHUNK
        scratch = [
            pltpu.VMEM((t, GLA_DK), BF16),
            pltpu.VMEM((n, GLA_DV, GLA_DK), F32),
            pltpu.VMEM((n, SUBLANES, GLA_DK), F32),
            pltpu.VMEM((n, GLA_DV, GLA_DK), BF16),
        ]
    return pl.pallas_call(
        body,
        grid=(nb, GLA_HEADS),
        in_specs=in_specs,
        out_specs=out_specs,
        out_shape=out_shape,
        scratch_shapes=scratch,
        compiler_params=_params("parallel", "parallel"),
        name="gla_small" if small else "gla",
    )(p, p, p, gf, w_f2, b_f.reshape(1, -1), s0)


def _moba_prompt_body(q_ref, k_ref, v_ref, *rest, nblk, ncast):
    cast_in, o_ref, cast_out = rest[:ncast], rest[ncast], rest[ncast + 1:]
    for wi, wo in zip(cast_in, cast_out):
        wo[...] = wi[...].astype(BF16)

    blk = MOBA_BLOCK
    t = nblk * blk
    shift = blk.bit_length() - 1
    assert 1 << shift == blk and nblk <= LANES and nblk % MOBA_PAIR == 0
    scale = MOBA_DH ** -0.5
    q = q_ref[...]
    k = k_ref[...]
    qb = (q * scale).astype(BF16)
    kb = k.astype(BF16)
    vb = v_ref[...].astype(BF16)

    km = jnp.concatenate([jnp.mean(k[n * blk:(n + 1) * blk, :], axis=0, keepdims=True) for n in range(nblk)],
                         axis=0)
    gate = lax.dot_general(km, q, (((1,), (1,)), ((), ())),
                           precision=lax.Precision.HIGHEST, preferred_element_type=F32)
    nrow = lax.broadcasted_iota(I32, (nblk, t), 0)
    qblk = jnp.right_shift(lax.broadcasted_iota(I32, (nblk, t), 1), shift)
    valid = nrow < qblk
    gm = jnp.where(valid, gate, NEG_INF)
    rank = jnp.zeros((nblk, t), I32)
    for m in range(nblk):
        g_m = gm[m:m + 1, :]
        beats = jnp.logical_or(g_m > gm, jnp.logical_and(g_m == gm, m < nrow))
        rank = rank + beats.astype(I32)
    sel = jnp.logical_and(valid, rank < MOBA_TOPK).astype(F32)
    sel_c = jnp.concatenate([sel, jnp.zeros((LANES - nblk, t), F32)], axis=0).T
    bias_c = jnp.where(sel_c > 0.0, 0.0, NEG_INF)

    causal = lax.broadcasted_iota(I32, (blk, blk), 1) <= lax.broadcasted_iota(I32, (blk, blk), 0)
    masked = jnp.full((blk, blk), NEG_INF, F32)
    pair = MOBA_PAIR
    for v in range(nblk // pair):
        lo = pair * v * blk
        nbi = pair * (v + 1)
        s = _nt(qb[lo:lo + pair * blk, :], kb[0:nbi * blk, :])
        bias = bias_c[lo:lo + pair * blk, :]
        pieces = []
        for n in range(nbi):
            sn = s[:, n * blk:(n + 1) * blk]
            if n < pair * v:
                pieces.append(sn + bias[:, n:n + 1])
                continue
            j = n - pair * v
            parts = []
            for i in range(pair):
                sni = sn[i * blk:(i + 1) * blk, :]
                if i == j:
                    parts.append(jnp.where(causal, sni, NEG_INF))
                elif i > j:
                    parts.append(sni + bias[i * blk:(i + 1) * blk, n:n + 1])
                else:
                    parts.append(masked)
            pieces.append(jnp.concatenate(parts, axis=0))
        s = jnp.concatenate(pieces, axis=1)
        m = jnp.max(s, axis=-1, keepdims=True)
        p = jnp.exp(s - m)
        l = jnp.sum(p, axis=-1, keepdims=True)
        acc = jnp.dot(p.astype(BF16), vb[0:nbi * blk, :], preferred_element_type=F32)
        o_ref[lo:lo + pair * blk, :] = acc / l


def _moba_prompt(p, to_cast, *, nb, t):
    nblk = t // MOBA_BLOCK
    kq, kk, kv = COL_MQ // MOBA_DH, COL_MK // MOBA_DH, COL_MV // MOBA_DH
    steps = nb * MOBA_HEADS
    bf16_rows = 2 * SUBLANES
    cast_specs = []
    for w in to_cast:
        assert w.shape[0] % (steps * bf16_rows) == 0
        cast_specs.append(pl.BlockSpec((w.shape[0] // steps, w.shape[1]), lambda b, h: (b * MOBA_HEADS + h, 0)))
    outs = pl.pallas_call(
        functools.partial(_moba_prompt_body, nblk=nblk, ncast=len(to_cast)),
        grid=(nb, MOBA_HEADS),
        in_specs=[
            pl.BlockSpec((t, MOBA_DH), lambda b, h: (b, kq + h)),
            pl.BlockSpec((t, MOBA_DH), lambda b, h: (b, kk + h)),
            pl.BlockSpec((t, MOBA_DH), lambda b, h: (b, kv + h)),
        ] + cast_specs,
        out_specs=[pl.BlockSpec((t, MOBA_DH), lambda b, h: (b, h))] + cast_specs,
        out_shape=[jax.ShapeDtypeStruct((nb * t, MOBA_W), F32)]
        + [jax.ShapeDtypeStruct(w.shape, BF16) for w in to_cast],
        compiler_params=_params("parallel", "parallel"),
        name="moba_prompt",
    )(p, p, p, *to_cast)
    return outs[0], list(outs[1:])


def _bmean_body(pt_ref, *refs, ppb):
    pages, o_ref = refs[:-1], refs[-1]
    for j in range(len(pages) // ppb):
        tot = jnp.sum(pages[j * ppb][...], axis=0)
        for r in pages[j * ppb + 1:(j + 1) * ppb]:
            tot = tot + jnp.sum(r[...], axis=0)
        o_ref[j] = tot * (1.0 / MOBA_BLOCK)


def _block_means(cache_k, page_table, layer):
    nb, n_pages = page_table.shape
    ppb = MOBA_BLOCK // PAGE_SIZE
    n_full = n_pages // ppb
    bps = BMEAN_BLOCKS_PER_STEP
    assert n_full % bps == 0
    pps = bps * ppb
    page_block = (None, None, PAGE_SIZE, MOBA_HEADS, MOBA_DH)
    page_spec = lambda j: pl.BlockSpec(page_block, lambda b, n, pt: (layer, pt[b, pps * n + j], 0, 0, 0))
    return pl.pallas_call(
        functools.partial(_bmean_body, ppb=ppb),
        grid_spec=pltpu.PrefetchScalarGridSpec(
            num_scalar_prefetch=1,
            grid=(nb, n_full // bps),
            in_specs=[page_spec(j) for j in range(pps)],
            out_specs=pl.BlockSpec((None, bps, MOBA_HEADS, MOBA_DH), lambda b, n, pt: (b, n, 0, 0)),
        ),
        out_shape=jax.ShapeDtypeStruct((nb, n_full, MOBA_HEADS, MOBA_DH), F32),
        compiler_params=_params("parallel", "arbitrary"),
        name="moba_block_means",
    )(page_table, *([cache_k] * pps))


def _gate_topk_body(q_ref, bm_ref, o_ref, *, td):
    n_full = bm_ref.shape[1]
    lane = lax.broadcasted_iota(I32, (n_full, LANES), 1)
    row = lax.broadcasted_iota(I32, (n_full, LANES), 0)
    orow = lax.broadcasted_iota(I32, (SUBLANES, LANES), 0)
    for h in range(MOBA_HEADS):
        km = bm_ref[h]
        g = jnp.full((n_full, LANES), NEG_INF, F32)
        for t in range(td):
            qv = q_ref[t:t + 1, h * MOBA_DH:(h + 1) * MOBA_DH]
            g = jnp.where(lane == t, jnp.sum(km * qv, axis=-1, keepdims=True), g)
        outv = jnp.zeros((SUBLANES, LANES), I32)
        for j in range(MOBA_TOPK):
            mx = jnp.max(g, axis=0, keepdims=True)
            idx = jnp.min(jnp.where(g == mx, row, n_full), axis=0, keepdims=True)
            g = jnp.where(row == idx, NEG_INF, g)
            outv = jnp.where(orow == j, idx, outv)
        o_ref[h] = outv


def _gate_topk(p, bmean_t, *, nb, td):
    n_full = bmean_t.shape[2]
    assert td <= SUBLANES and MOBA_TOPK <= SUBLANES
    kq = COL_MQ // MOBA_W
    return pl.pallas_call(
        functools.partial(_gate_topk_body, td=td),
        grid=(nb,),
        in_specs=[
            pl.BlockSpec((td, MOBA_W), lambda b: (b, kq)),
            pl.BlockSpec((None, MOBA_HEADS, n_full, MOBA_DH), lambda b: (b, 0, 0, 0)),
        ],
        out_specs=pl.BlockSpec((None, MOBA_HEADS, SUBLANES, LANES), lambda b: (b, 0, 0, 0)),
        out_shape=jax.ShapeDtypeStruct((nb, MOBA_HEADS, SUBLANES, LANES), I32),
        compiler_params=_params("parallel"),
        name="moba_gate_topk",
    )(p, bmean_t)


def _moba_sample_body(idx_ref, pt_ref, q_ref, k_ref, v_ref, ck_ref, cv_ref, o_ref, kbuf, vbuf, sem, *, layer, td):
    step = pl.program_id(0)
    nsteps = pl.num_programs(0)
    slot = lax.rem(step, 2)
    ppb = MOBA_BLOCK // PAGE_SIZE
    per_q = MOBA_TOPK * ppb
    nsel = MOBA_TOPK * MOBA_BLOCK
    scale = MOBA_DH ** -0.5
    slots = [(t, j, pg) for t in range(td) for j in range(MOBA_TOPK) for pg in range(ppb)]

    def copies(st, sl, t, j, pg):
        b = lax.div(st, MOBA_HEADS)
        h = lax.rem(st, MOBA_HEADS)
        blk = idx_ref[(st * td + t) * MOBA_TOPK + j]
        page = pt_ref[b, blk * ppb + pg]
        dst = pl.ds((t * per_q + j * ppb + pg) * PAGE_SIZE, PAGE_SIZE)
        return (pltpu.make_async_copy(ck_ref.at[layer, page, :, h, :], kbuf.at[sl, dst, :], sem.at[sl, 0]),
                pltpu.make_async_copy(cv_ref.at[layer, page, :, h, :], vbuf.at[sl, dst, :], sem.at[sl, 1]))

    def issue(st, sl):
        for c in slots:
            ck, cv = copies(st, sl, *c)
            ck.start()
            cv.start()

    @pl.when(step == 0)
    def _():
        issue(step, slot)

    @pl.when(step + 1 < nsteps)
    def _():
        issue(step + 1, 1 - slot)

    for c in slots:
        ck, cv = copies(step, slot, *c)
        ck.wait()
        cv.wait()

    rows = 2 * SUBLANES
    q = q_ref[...]
    qp = jnp.concatenate([q, jnp.zeros((rows - td, MOBA_DH), F32)], axis=0).astype(BF16)
    s_sel = _nt(qp, kbuf[slot].astype(BF16)) * scale
    r = lax.broadcasted_iota(I32, (rows, td * nsel), 0)
    c = lax.broadcasted_iota(I32, (rows, td * nsel), 1)
    mine = jnp.logical_and(c >= r * nsel, c < (r + 1) * nsel)
    s_sel = jnp.where(mine, s_sel, NEG_INF)

    k_new = k_ref[...]
    v_new = v_ref[...]
    rn = lax.broadcasted_iota(I32, (td, LANES), 0)
    cn = lax.broadcasted_iota(I32, (td, LANES), 1)
    s_new = jnp.full((td, LANES), NEG_INF, F32)
    for t in range(td):
        col = jnp.sum(q * k_new[t:t + 1, :], axis=-1, keepdims=True) * scale
        s_new = jnp.where(jnp.logical_and(cn == t, rn >= t), col, s_new)

    s_sel = s_sel[:td, :]
    m = jnp.maximum(jnp.max(s_sel, axis=-1, keepdims=True), jnp.max(s_new, axis=-1, keepdims=True))
    p_sel = jnp.exp(s_sel - m)
    p_new = jnp.exp(s_new - m)
    l = jnp.sum(p_sel, axis=-1, keepdims=True) + jnp.sum(p_new, axis=-1, keepdims=True)
    pp = jnp.concatenate([p_sel, jnp.zeros((rows - td, td * nsel), F32)], axis=0).astype(BF16)
    acc = jnp.dot(pp, vbuf[slot].astype(BF16), preferred_element_type=F32)[:td, :]
    for t in range(td):
        acc = acc + p_new[:, t:t + 1] * v_new[t:t + 1, :]
    o_ref[...] = acc / l


def _moba_sample(p, cache_k, cache_v, page_table, idx_flat, *, layer, nb, td):
    kq, kk, kv = COL_MQ // MOBA_DH, COL_MK // MOBA_DH, COL_MV // MOBA_DH
    nrows = td * MOBA_TOPK * MOBA_BLOCK
    row_spec = lambda off: pl.BlockSpec(
        (td, MOBA_DH), lambda s, idx, pt: (lax.div(s, MOBA_HEADS), off + lax.rem(s, MOBA_HEADS)))
    return pl.pallas_call(
        functools.partial(_moba_sample_body, layer=layer, td=td),
        grid_spec=pltpu.PrefetchScalarGridSpec(
            num_scalar_prefetch=2,
            grid=(nb * MOBA_HEADS,),
            in_specs=[
                row_spec(kq), row_spec(kk), row_spec(kv),
                pl.BlockSpec(memory_space=pl.ANY),
                pl.BlockSpec(memory_space=pl.ANY),
            ],
            out_specs=row_spec(0),
            scratch_shapes=[
                pltpu.VMEM((2, nrows, MOBA_DH), F32),
                pltpu.VMEM((2, nrows, MOBA_DH), F32),
                pltpu.SemaphoreType.DMA((2, 2)),
            ],
        ),
        out_shape=jax.ShapeDtypeStruct((nb * td, MOBA_W), F32),
        compiler_params=_params("arbitrary"),
        name="moba_sample",
    )(idx_flat, page_table, p, p, p, cache_k, cache_v)


def _merge_body(go_ref, gg_ref, mo_ref, mg_ref, on_ref, a_ref):
    on = on_ref[...]
    for h in range(GLA_HEADS):
        sl = slice(h * GLA_DV, (h + 1) * GLA_DV)
        g = _rms_rows(go_ref[:, sl], on) * _silu(gg_ref[:, sl])
        a_ref[:, sl] = g.astype(a_ref.dtype)
    a_ref[:, GLA_VW:] = (mo_ref[...] * _silu(mg_ref[...])).astype(a_ref.dtype)


def _merge(gla_o, moba_o, p, o_norm, *, tm):
    m = gla_o.shape[0]
    kg, km = COL_GG // GLA_VW, COL_MG // MOBA_W
    return pl.pallas_call(
        _merge_body,
        grid=(m // tm,),
        in_specs=[
            pl.BlockSpec((tm, GLA_VW), lambda i: (i, 0)),
            pl.BlockSpec((tm, GLA_VW), lambda i: (i, kg)),
            pl.BlockSpec((tm, MOBA_W), lambda i: (i, 0)),
            pl.BlockSpec((tm, MOBA_W), lambda i: (i, km)),
            pl.BlockSpec((1, GLA_DV), lambda i: (0, 0)),
        ],
        out_specs=pl.BlockSpec((tm, GLA_VW + MOBA_W), lambda i: (i, 0)),
        out_shape=jax.ShapeDtypeStruct((m, GLA_VW + MOBA_W), BF16),
        compiler_params=_params("parallel"),
        name="a_merge",
    )(gla_o, p, moba_o, p, o_norm.reshape(1, -1))


def _s5_coef_body(lr_ref, li_ref, ldt_ref, br_ref, bi_ref, bbr_ref, bbi_ref, abr_ref, abi_ref):
    lr = lr_ref[...]
    li = li_ref[...]
    dt = jnp.exp(ldt_ref[...])
    mag = jnp.exp(lr * dt)
    ab_re = mag * jnp.cos(li * dt)
    ab_im = mag * jnp.sin(li * dt)
    den = lr * lr + li * li
    nr = ab_re - 1.0
    cr = (nr * lr + ab_im * li) / den
    ci = (ab_im * lr - nr * li) / den
    br = br_ref[...]
    bi = bi_ref[...]
    bbr_ref[...] = cr * br - ci * bi
    bbi_ref[...] = cr * bi + ci * br
    abr_ref[...] = ab_re
    abi_ref[...] = ab_im


def _s5_coef(lam_re, lam_im, log_dt, b_re, b_im):
    g, p = lam_re.shape
    w = p * S5_GROUP
    rep = lambda a: jnp.repeat(a, S5_GROUP, axis=1)
    full = pl.BlockSpec((g, w), lambda: (0, 0))
    outs = pl.pallas_call(
        _s5_coef_body,
        in_specs=[full] * 5,
        out_specs=[full] * 4,
        out_shape=[jax.ShapeDtypeStruct((g, w), F32)] * 4,
        name="s5_coef",
    )(rep(lam_re), rep(lam_im), jnp.broadcast_to(log_dt[:, None], (g, w)),
      b_re.reshape(g, w), b_im.reshape(g, w))
    bbr, bbi, abr, abi = outs
    return (bbr.reshape(g, p, S5_GROUP), bbi.reshape(g, p, S5_GROUP),
            abr[:, ::S5_GROUP], abi[:, ::S5_GROUP])


def _s5_expand(w2):
    per_tile = LANES // S5_P
    rg = jnp.right_shift(lax.broadcasted_iota(I32, (S5_SLAB, LANES), 0), S5_GROUP.bit_length() - 1)
    lg = jnp.right_shift(lax.broadcasted_iota(I32, (S5_SLAB, LANES), 1), S5_P.bit_length() - 1)
    pieces = []
    for part in range(2):
        w = w2[:, part * LANES:(part + 1) * LANES]
        for kk in range(S5_SLAB_STATES // LANES):
            pieces.append(jnp.where(rg == per_tile * kk + lg, w, 0.0))
    return jnp.concatenate(pieces, axis=1)


def _s5_body(u_ref, bb_ref, cc_ref, ar_ref, ai_ref, d_ref, x0r_ref, x0i_ref,
             y_ref, xr_ref, xi_ref, lhs, bu, ybuf, cr, ci, bd_in, bd_out, *, ns, nt):
    ti = pl.program_id(1)
    tpv = SUBLANES // ns
    nst = S5_SLAB_STATES
    lo_rows = SUBLANES - ns
    nlb = S5_SLAB // LANES

    @pl.when(ti == 0)
    def _():
        cr[...] = jnp.zeros_like(cr)
        ci[...] = jnp.zeros_like(ci)
        cr[lo_rows:, :] = x0r_ref[...]
        ci[lo_rows:, :] = x0i_ref[...]
        e_in = _s5_expand(bb_ref[...])
        bd_in[:S5_SLAB, :] = e_in.astype(BF16)
        if tpv == 2:
            e_re, e_im = e_in[:, :nst], e_in[:, nst:]
            a_re, a_im = ar_ref[...], ai_ref[...]
            bd_in[S5_SLAB:, :] = jnp.concatenate([a_re * e_re - a_im * e_im, a_re * e_im + a_im * e_re],
                                                 axis=1).astype(BF16)
        bd_out[...] = _s5_expand(cc_ref[...]).astype(BF16)

    for b in range(ns):
        for j in range(nlb):
            lhs[j, pl.ds(b, nt, stride=ns), :] = u_ref[b, :, j * LANES:(j + 1) * LANES]
    u_rows = jnp.concatenate([lhs[j] for j in range(nlb)], axis=1)
    if tpv == 2:
        tbit = jnp.right_shift(lax.broadcasted_iota(I32, u_rows.shape, 0), ns.bit_length() - 1)
        u_prev = jnp.where(jnp.bitwise_and(tbit, 1) == 1, pltpu.roll(u_rows, ns, 0), 0.0)
        lhs_rows = jnp.concatenate([u_rows, u_prev], axis=1)
    else:
        lhs_rows = u_rows
    bu[...] = jnp.dot(lhs_rows.astype(BF16), bd_in[...], preferred_element_type=F32)

    ar = ar_ref[...]
    ai = ai_ref[...]
    if tpv == 1:
        def step(v, carry):
            xr, xi = carry
            sl = pl.ds(pl.multiple_of(v * SUBLANES, SUBLANES), SUBLANES)
            nxr = ar * xr - ai * xi + bu[sl, :nst]
            nxi = ar * xi + ai * xr + bu[sl, nst:]
            bu[sl, :nst] = nxr
            bu[sl, nst:] = nxi
            return nxr, nxi
    else:
        assert tpv == 2
        lo = lax.broadcasted_iota(I32, (SUBLANES, nst), 0) < ns
        c2r = jnp.where(lo, ar, ar * ar - ai * ai)
        c2i = jnp.where(lo, ai, 2.0 * ar * ai)

        def step(v, carry):
            xr, xi = carry
            sl = pl.ds(pl.multiple_of(v * SUBLANES, SUBLANES), SUBLANES)
            pr = jnp.where(lo, pltpu.roll(xr, ns, 0), xr)
            pi = jnp.where(lo, pltpu.roll(xi, ns, 0), xi)
            nxr = bu[sl, :nst] + c2r * pr - c2i * pi
            nxi = bu[sl, nst:] + c2r * pi + c2i * pr
            bu[sl, :nst] = nxr
            bu[sl, nst:] = nxi
            return nxr, nxi

    nv = (nt * ns) // SUBLANES
    xr, xi = lax.fori_loop(0, nv, step, (cr[...], ci[...]), unroll=True)
    cr[...] = xr
    ci[...] = xi

    yv = _nt(bu[...].astype(BF16), bd_out[...]) + d_ref[...] * u_rows
    yv = 0.5 * yv * (1.0 + lax.erf(yv * math.sqrt(0.5)))
    for j in range(nlb):
        ybuf[j] = yv[:, j * LANES:(j + 1) * LANES]
    for b in range(ns):
        for j in range(nlb):
            y_ref[b, :, j * LANES:(j + 1) * LANES] = ybuf[j, pl.ds(b, nt, stride=ns), :]

    @pl.when(ti == pl.num_programs(1) - 1)
    def _():
        xr_ref[...] = cr[lo_rows:, :]
        xi_ref[...] = ci[lo_rows:, :]


def _s5_core(uz, bb, cc, ab_re, ab_im, d, x0_re, x0_im, *, ns, t, nt):
    w = d.shape[0]
    nslab = w // S5_SLAB
    nst = S5_SLAB_STATES
    assert SUBLANES % ns == 0 and t % nt == 0 and (nt * ns) % SUBLANES == 0
    u3 = uz.reshape(ns, t, uz.shape[1])
    to_slab = lambda a: jnp.transpose(a.reshape(ns, nslab, nst), (1, 0, 2))
    st_spec = pl.BlockSpec((None, ns, nst), lambda s, i: (s, 0, 0))
    y, xr, xi = pl.pallas_call(
        functools.partial(_s5_body, ns=ns, nt=nt),
        grid=(nslab, t // nt),
        in_specs=[
            pl.BlockSpec((ns, nt, S5_SLAB), lambda s, i: (0, i, s)),
            pl.BlockSpec((None, S5_SLAB, 2 * LANES), lambda s, i: (s, 0, 0)),
            pl.BlockSpec((None, S5_SLAB, 2 * LANES), lambda s, i: (s, 0, 0)),
            pl.BlockSpec((None, 1, nst), lambda s, i: (s, 0, 0)),
            pl.BlockSpec((None, 1, nst), lambda s, i: (s, 0, 0)),
            pl.BlockSpec((1, S5_SLAB), lambda s, i: (0, s)),
            st_spec, st_spec,
        ],
        out_specs=[
            pl.BlockSpec((ns, nt, S5_SLAB), lambda s, i: (0, i, s)),
            st_spec, st_spec,
        ],
        out_shape=[
            jax.ShapeDtypeStruct((ns, t, w), F32),
            jax.ShapeDtypeStruct((nslab, ns, nst), F32),
            jax.ShapeDtypeStruct((nslab, ns, nst), F32),
        ],
        scratch_shapes=[
            pltpu.VMEM((S5_SLAB // LANES, nt * ns, LANES), F32),
            pltpu.VMEM((nt * ns, 2 * nst), F32),
            pltpu.VMEM((S5_SLAB // LANES, nt * ns, LANES), F32),
            pltpu.VMEM((SUBLANES, nst), F32),
            pltpu.VMEM((SUBLANES, nst), F32),
            pltpu.VMEM((S5_SLAB * (SUBLANES // ns), 2 * nst), BF16),
            pltpu.VMEM((S5_SLAB, 2 * nst), BF16),
        ],
        compiler_params=_params("parallel", "arbitrary"),
        name="s5_core",
    )(u3, bb, cc, ab_re.reshape(nslab, 1, nst), ab_im.reshape(nslab, 1, nst), d.reshape(1, w),
      to_slab(x0_re), to_slab(x0_im))
    g = w // S5_GROUP
    from_slab = lambda a: jnp.transpose(a, (1, 0, 2)).reshape(ns, g, S5_P)
    return y.reshape(ns * t, w), from_slab(xr), from_slab(xi)


def _s5_compact(re, im):
    reps = LANES // S5_P
    flat = lambda a: jnp.tile(a.reshape(-1, S5_SLAB, S5_P), (1, 1, reps))
    return jnp.concatenate([flat(re), flat(im)], axis=2)


def _rope_tables(pos, reps):
    half = MOBA_DH // 2
    inv_freq = ROPE_THETA ** (-jnp.arange(half, dtype=F32) / half)
    ang = pos.astype(F32)[:, None] * inv_freq[None, :]
    cos = jnp.cos(ang)
    sin = jnp.sin(ang)
    cos2 = jnp.concatenate([cos, cos], axis=-1)
    sin2 = jnp.concatenate([-sin, sin], axis=-1)
    return jnp.tile(cos2, (reps, 1)), jnp.tile(sin2, (reps, 1))


def _layer_a(x, pos, nb, t, s0, weights, w_out, to_cast, sample_ctx):
    norm, w_main, w_gf, w_f2, b_f, o_norm, q_norm, k_norm = weights
    cast = []
    m = nb * t
    tm = min(m, GEMM_TM)
    cos, sin = _rope_tables(pos, nb)
    p, gf = _aproj(x, norm, w_main, w_gf, q_norm, k_norm, cos, sin, tm=tm, tn=512)
    gla_o, state = _gla(p, gf, w_f2, b_f, s0, nb=nb, t=t)
    if sample_ctx is None:
        moba_o, cast = _moba_prompt(p, to_cast, nb=nb, t=t)
        w_out = cast[0]
    else:
        cache_k, cache_v, page_table, layer = sample_ctx
        bmean = _block_means(cache_k, page_table, layer)
        ids = _gate_topk(p, jnp.transpose(bmean, (0, 2, 1, 3)), nb=nb, td=t)
        idx_flat = jnp.transpose(ids[:, :, :MOBA_TOPK, :t], (0, 1, 3, 2)).reshape(-1)
        moba_o = _moba_sample(p, cache_k, cache_v, page_table, idx_flat, layer=layer, nb=nb, td=t)
    a = _merge(gla_o, moba_o, p, o_norm, tm=min(m, 256))
    x_new = _gemm_res(a, w_out, x, tm=tm, tn=512)
    mk = p[:, COL_MK:COL_MK + MOBA_W].reshape(nb, t, MOBA_HEADS, MOBA_DH)
    mv = p[:, COL_MV:COL_MV + MOBA_W].reshape(nb, t, MOBA_HEADS, MOBA_DH)
    return x_new, mk, mv, state, cast


def _layer_c(x, ns, t, x0_re, x0_im, weights):
    norm, w_in, bb, cc, ab_re, ab_im, d, w_glu, b_glu, w_out = weights
    m = ns * t
    tm = min(m, GEMM_TM)
    uz = _gemm_norm(x, norm, w_in, tm=tm, tn=512)
    y, xr, xi = _s5_core(uz, bb, cc, ab_re, ab_im, d, x0_re, x0_im, ns=ns, t=t, nt=min(t, S5_TIME_TILE))
    v = _gemm_glu(y, w_glu, uz, b_glu, tm=tm, tn=512)
    x_new = _gemm_res(v, w_out, x, tm=tm, tn=512)
    return x_new, xr, xi


def kernel(x_prompt, x_sample, cache_k, cache_v, state_gla, state_s5_re, state_s5_im, page_table, norm_a, w_in_a, w_gla_f2, b_gla_f, gla_out_norm, moba_q_norm, moba_k_norm, w_out_a, norm_c, w_in_c, s5_lambda_re, s5_lambda_im, s5_log_dt, s5_b_re, s5_b_im, s5_c_re, s5_c_im, s5_d, w_glu, b_glu, w_out_c):
    nbp, tp, d = x_prompt.shape
    nbs, ts, _ = x_sample.shape
    depth = norm_a.shape[0] + norm_c.shape[0]
    past_len = page_table.shape[1] * PAGE_SIZE
    assert past_len % MOBA_BLOCK == 0 and past_len // MOBA_BLOCK >= MOBA_TOPK
    assert tp % MOBA_BLOCK == 0 and tp % GLA_CHUNK == 0
    pos_p = jnp.arange(tp)
    pos_s = past_len + jnp.arange(ts)
    xp = x_prompt.reshape(nbp * tp, d)
    xs = x_sample.reshape(nbs * ts, d)
    g = d // S5_GROUP
    outs = [[] for _ in range(10)]
    for layer in range(depth):
        i = layer // 2
        if layer % 2 == 0:
            w_main, w_gf = _wprep(jnp.swapaxes(w_in_a[i], 0, 1))
            weights = (norm_a[i], w_main, w_gf, w_gla_f2[i], b_gla_f[i], gla_out_norm[i], moba_q_norm[i],
                       moba_k_norm[i])
            to_cast = [w_out_a[i]]
            if layer + 1 < depth:
                to_cast += [w_in_c[i], w_glu[i], w_out_c[i]]
            zero_state = jnp.zeros((nbp, GLA_HEADS, GLA_DK, GLA_DV), F32)
            xp, mk, mv, sp, cast = _layer_a(xp, pos_p, nbp, tp, zero_state, weights, None, to_cast, None)
            w_out_a_bf, c_weights_bf = cast[0], cast[1:]
            outs[0].append(mk)
            outs[1].append(mv)
            outs[4].append(sp)
            xs, mk, mv, ss, _ = _layer_a(xs, pos_s, nbs, ts, state_gla[i], weights, w_out_a_bf, None,
                                         (cache_k, cache_v, page_table, i))
            outs[2].append(mk)
            outs[3].append(mv)
            outs[5].append(ss)
        else:
            bb_re, bb_im, ab_re, ab_im = _s5_coef(s5_lambda_re[i], s5_lambda_im[i], s5_log_dt[i],
                                                  s5_b_re[i], s5_b_im[i])
            bb = _s5_compact(jnp.swapaxes(bb_re, 1, 2), jnp.swapaxes(bb_im, 1, 2))
            cc = _s5_compact(s5_c_re[i], -s5_c_im[i])
            w_in_bf, w_glu_bf, w_out_bf = c_weights_bf
            weights = (norm_c[i], w_in_bf, bb, cc, ab_re, ab_im, s5_d[i], w_glu_bf, b_glu[i], w_out_bf)
            zeros = jnp.zeros((nbp, g, S5_P), F32)
            xp, xr, xi = _layer_c(xp, nbp, tp, zeros, zeros, weights)
            outs[6].append(xr)
            outs[7].append(xi)
            xs, xr, xi = _layer_c(xs, nbs, ts, state_s5_re[i], state_s5_im[i], weights)
            outs[8].append(xr)
            outs[9].append(xi)
    pk, pv, sk, sv, pg, sg, psr, psi, ssr, ssi = [jnp.stack(o) for o in outs]
    return (xp.reshape(nbp, tp, d), xs.reshape(nbs, ts, d), pk, pv, sk, sv, pg, sg, psr, psi, ssr, ssi)
```

```python
import functools
import math

import jax
import jax.numpy as jnp
from jax import lax
from jax.experimental import pallas as pl
from jax.experimental.pallas import tpu as pltpu

F32 = jnp.float32
BF16 = jnp.bfloat16
I32 = jnp.int32

RMS_EPS = 1e-6
GLA_HEADS = 8
GLA_DK = 128
GLA_DV = 256
GLA_KW = GLA_HEADS * GLA_DK
GLA_VW = GLA_HEADS * GLA_DV
GLA_GATE_RANK = 16
GLA_GATE_TAU = 16.0
GLA_CHUNK = 64
GLA_GROUP_ROWS = 256
MOBA_HEADS = 16
MOBA_DH = 128
MOBA_W = MOBA_HEADS * MOBA_DH
MOBA_BLOCK = 256
MOBA_TOPK = 3
MOBA_PAIR = 1
ROPE_THETA = 10000.0
PAGE_SIZE = 128
BMEAN_BLOCKS_PER_STEP = 4
S5_GROUP = 16
S5_P = 64
S5_SLAB = 128
S5_SLAB_STATES = (S5_SLAB // S5_GROUP) * S5_P
S5_TIME_TILE = 1024

LANES = 128
SUBLANES = 8
VMEM_LIMIT = 52 * 1024 * 1024
NEG_INF = float("-inf")

COL_GQ = 0
COL_GK = COL_GQ + GLA_KW
COL_GV = COL_GK + GLA_KW
COL_GG = COL_GV + GLA_VW
COL_MQ = COL_GG + GLA_VW
COL_MK = COL_MQ + MOBA_W
COL_MV = COL_MK + MOBA_W
COL_MG = COL_MV + MOBA_W
A_COLS = COL_MG + MOBA_W
WPREP_ROWS = 256
GEMM_TM = 1024
EPILOGUE_ROWS = 1024
GEMM_SUB = 256
LHS_DOUBLE_BUFFER_MAX_BYTES = 8 * 1024 * 1024


def _params(*sem):
    return pltpu.CompilerParams(dimension_semantics=sem, vmem_limit_bytes=VMEM_LIMIT)


def _nt(a, b):
    return lax.dot_general(a, b, (((1,), (1,)), ((), ())), preferred_element_type=F32)


def _tn(a, b):
    return lax.dot_general(a, b, (((0,), (0,)), ((), ())), preferred_element_type=F32)


def _silu(x):
    return x / (1.0 + jnp.exp(-x))


def _log_sigmoid(x):
    return jnp.minimum(x, 0.0) - jnp.log(1.0 + jnp.exp(-jnp.abs(x)))


def _rms_rows(x, g):
    ms = jnp.mean(x * x, axis=-1, keepdims=True)
    return x * lax.rsqrt(ms + RMS_EPS) * g


def _lhs_spec(tm, k, itemsize):
    if tm * k * itemsize > LHS_DOUBLE_BUFFER_MAX_BYTES:
        return pl.BlockSpec((tm, k), lambda i, j: (i, 0), pipeline_mode=pl.Buffered(1))
    return pl.BlockSpec((tm, k), lambda i, j: (i, 0))


def _wprep_body(a_ref, b_ref, om_ref, of_ref, *, gate_blk):
    i = pl.program_id(0)
    gr = GLA_GATE_RANK

    @pl.when(i == 0)
    def _():
        of_ref[...] = jnp.zeros_like(of_ref)

    @pl.when(i < gate_blk)
    def _():
        om_ref[...] = a_ref[...].astype(BF16)

    @pl.when(i == gate_blk)
    def _():
        of_ref[:gr, :] = a_ref[:gr, :].astype(BF16)

    @pl.when(i >= gate_blk)
    def _():
        om_ref[:-gr, :] = a_ref[gr:, :].astype(BF16)
        om_ref[-gr:, :] = b_ref[...].astype(BF16)


def _wprep(wt):
    n, d = wt.shape
    tk = WPREP_ROWS
    gr = GLA_GATE_RANK
    gate_lo = COL_GG + GLA_VW
    assert n == A_COLS + gr and gate_lo % tk == 0 and A_COLS % tk == 0 and tk % gr == 0
    last_tail = (n - gr) // gr
    return pl.pallas_call(
        functools.partial(_wprep_body, gate_blk=gate_lo // tk),
        grid=(A_COLS // tk,),
        in_specs=[
            pl.BlockSpec((tk, d), lambda i: (i, 0)),
            pl.BlockSpec((gr, d), lambda i: (jnp.minimum((i + 1) * (tk // gr), last_tail), 0)),
        ],
        out_specs=[pl.BlockSpec((tk, d), lambda i: (i, 0)), pl.BlockSpec((LANES, d), lambda i: (0, 0))],
        out_shape=[jax.ShapeDtypeStruct((A_COLS, d), BF16), jax.ShapeDtypeStruct((LANES, d), BF16)],
        compiler_params=_params("arbitrary"),
        name="a_weight_prep",
    )(wt, wt)


def _aproj_body(x_ref, g_ref, w_ref, wf_ref, qn_ref, kn_ref, cos_ref, sin_ref, o_ref, gf_ref, h_ref, *, tn):
    j = pl.program_id(1)

    @pl.when(j == 0)
    def _():
        h_ref[...] = _rms_rows(x_ref[...], g_ref[...]).astype(BF16)
        gf_ref[...] = _nt(h_ref[...], wf_ref[...])

    o_ref[...] = _nt(h_ref[...], w_ref[...])

    q_lo, k_lo, k_hi = COL_MQ // tn, COL_MK // tn, COL_MV // tn

    @pl.when(jnp.logical_and(j >= q_lo, j < k_hi))
    def _():
        gain = jnp.where(j < k_lo, qn_ref[...], kn_ref[...])
        rb = min(o_ref.shape[0], EPILOGUE_ROWS)

        def rows_block(r, carry):
            rows = pl.ds(pl.multiple_of(r * rb, rb), rb)
            cos = cos_ref[rows, :]
            sin = sin_ref[rows, :]
            for hh in range(tn // MOBA_DH):
                cols = slice(hh * MOBA_DH, (hh + 1) * MOBA_DH)
                y = _rms_rows(o_ref[rows, cols], gain)
                o_ref[rows, cols] = y * cos + pltpu.roll(y, MOBA_DH // 2, 1) * sin
            return carry

        lax.fori_loop(0, o_ref.shape[0] // rb, rows_block, 0)


def _aproj(x, g, w_main, w_gf, qn, kn, cos, sin, *, tm, tn):
    m, d = x.shape
    n = w_main.shape[0]
    assert n == A_COLS and COL_MQ % tn == 0 and COL_MK % tn == 0 and COL_MV % tn == 0
    return pl.pallas_call(
        functools.partial(_aproj_body, tn=tn),
        grid=(m // tm, n // tn),
        in_specs=[
            _lhs_spec(tm, d, 4),
            pl.BlockSpec((1, d), lambda i, j: (0, 0)),
            pl.BlockSpec((tn, d), lambda i, j: (j, 0)),
            pl.BlockSpec((LANES, d), lambda i, j: (0, 0)),
            pl.BlockSpec((1, MOBA_DH), lambda i, j: (0, 0)),
            pl.BlockSpec((1, MOBA_DH), lambda i, j: (0, 0)),
            pl.BlockSpec((tm, MOBA_DH), lambda i, j: (i, 0)),
            pl.BlockSpec((tm, MOBA_DH), lambda i, j: (i, 0)),
        ],
        out_specs=[
            pl.BlockSpec((tm, tn), lambda i, j: (i, j)),
            pl.BlockSpec((tm, LANES), lambda i, j: (i, 0)),
        ],
        out_shape=[jax.ShapeDtypeStruct((m, n), F32), jax.ShapeDtypeStruct((m, LANES), F32)],
        scratch_shapes=[pltpu.VMEM((tm, d), BF16)],
        compiler_params=_params("parallel", "arbitrary"),
        name="a_proj",
    )(x, g.reshape(1, d), w_main, w_gf, qn.reshape(1, -1), kn.reshape(1, -1), cos, sin)


def _gemm_norm_body(x_ref, g_ref, w_ref, o_ref, h_ref):
    @pl.when(pl.program_id(1) == 0)
    def _():
        h_ref[...] = _rms_rows(x_ref[...], g_ref[...]).astype(BF16)

    o_ref[...] = jnp.dot(h_ref[...], w_ref[...], preferred_element_type=F32)


def _gemm_norm(x, g, w, *, tm, tn):
    m, d = x.shape
    n = w.shape[1]
    return pl.pallas_call(
        _gemm_norm_body,
        grid=(m // tm, n // tn),
        in_specs=[
            _lhs_spec(tm, d, 4),
            pl.BlockSpec((1, d), lambda i, j: (0, 0)),
            pl.BlockSpec((d, tn), lambda i, j: (0, j)),
        ],
        out_specs=pl.BlockSpec((tm, tn), lambda i, j: (i, j)),
        out_shape=jax.ShapeDtypeStruct((m, n), F32),
        scratch_shapes=[pltpu.VMEM((tm, d), BF16)],
        compiler_params=_params("parallel", "arbitrary"),
        name="c_proj",
    )(x, g.reshape(1, d), w)


def _gemm_res_body(a_ref, w_ref, r_ref, o_ref):
    o_ref[...] = r_ref[...] + jnp.dot(a_ref[...], w_ref[...], preferred_element_type=F32)


def _gemm_res2_body(a1_ref, a2_ref, w_ref, r_ref, o_ref):
    k1 = a1_ref.shape[1]
    acc = jnp.dot(a1_ref[...], w_ref[:k1, :], preferred_element_type=F32)
    o_ref[...] = r_ref[...] + acc + jnp.dot(a2_ref[...], w_ref[k1:, :], preferred_element_type=F32)


def _gemm_res2(a1, a2, w, res, *, tm, tn):
    m, k1 = a1.shape
    k2 = a2.shape[1]
    n = w.shape[1]
    return pl.pallas_call(
        _gemm_res2_body,
        grid=(m // tm, n // tn),
        in_specs=[
            pl.BlockSpec((tm, k1), lambda i, j: (i, 0)),
            pl.BlockSpec((tm, k2), lambda i, j: (i, 0)),
            pl.BlockSpec((k1 + k2, tn), lambda i, j: (0, j)),
            pl.BlockSpec((tm, tn), lambda i, j: (i, j)),
        ],
        out_specs=pl.BlockSpec((tm, tn), lambda i, j: (i, j)),
        out_shape=jax.ShapeDtypeStruct((m, n), F32),
        compiler_params=_params("parallel", "arbitrary"),
        name="out_proj",
    )(a1, a2, w, res)


def _gemm_res(a, w, res, *, tm, tn):
    m, k = a.shape
    n = w.shape[1]
    return pl.pallas_call(
        _gemm_res_body,
        grid=(m // tm, n // tn),
        in_specs=[
            pl.BlockSpec((tm, k), lambda i, j: (i, 0)),
            pl.BlockSpec((k, tn), lambda i, j: (0, j)),
            pl.BlockSpec((tm, tn), lambda i, j: (i, j)),
        ],
        out_specs=pl.BlockSpec((tm, tn), lambda i, j: (i, j)),
        out_shape=jax.ShapeDtypeStruct((m, n), F32),
        compiler_params=_params("parallel", "arbitrary"),
        name="out_proj",
    )(a, w, res)


def _gemm_glu_body(y_ref, w_ref, z_ref, b_ref, o_ref, h_ref, *, tn):
    j = pl.program_id(1)

    @pl.when(j == 0)
    def _():
        h_ref[...] = y_ref[...].astype(BF16)

    sub = min(tn, GEMM_SUB)
    for c in range(tn // sub):
        cols = slice(c * sub, (c + 1) * sub)
        t = jnp.dot(h_ref[...], w_ref[:, cols], preferred_element_type=F32) + b_ref[:, cols]
        y = y_ref[:, pl.ds(pl.multiple_of(j * tn + c * sub, sub), sub)]
        z = z_ref[:, cols]
        o_ref[:, cols] = (y * z / ((1.0 + jnp.exp(-t)) * (1.0 + jnp.exp(-z)))).astype(o_ref.dtype)


def _gemm_glu(y, w, uz, bias, *, tm, tn):
    m, k = y.shape
    n = w.shape[1]
    zoff = n // tn
    return pl.pallas_call(
        functools.partial(_gemm_glu_body, tn=tn),
        grid=(m // tm, n // tn),
        in_specs=[
            _lhs_spec(tm, k, 4),
            pl.BlockSpec((k, tn), lambda i, j: (0, j)),
            pl.BlockSpec((tm, tn), lambda i, j: (i, zoff + j)),
            pl.BlockSpec((1, tn), lambda i, j: (0, j)),
        ],
        out_specs=pl.BlockSpec((tm, tn), lambda i, j: (i, j)),
        out_shape=jax.ShapeDtypeStruct((m, n), BF16),
        scratch_shapes=[pltpu.VMEM((tm, k), BF16)],
        compiler_params=_params("parallel", "arbitrary"),
        name="glu_proj",
    )(y, w, uz, bias.reshape(1, n))


def _gla_body(q_ref, k_ref, v_ref, gf_ref, wf_ref, bf_ref, s0_ref, gg_ref, on_ref, a_ref, s_ref,
              qd_s, u_s, dec_s, st_s, o_s, *, chunk):
    c = chunk
    t = q_ref.shape[0]
    n = t // c
    grp = GLA_GROUP_ROWS
    cpg = grp // c
    shift = c.bit_length() - 1
    assert 1 << shift == c and t % grp == 0
    scale = GLA_DK ** -0.5
    r = lax.broadcasted_iota(I32, (grp, grp), 0)
    cc = lax.broadcasted_iota(I32, (grp, grp), 1)
    same = jnp.right_shift(r, shift) == jnp.right_shift(cc, shift)
    causal = jnp.logical_and(same, cc <= r)
    causal_b = causal.astype(BF16)
    wf = wf_ref[...].astype(BF16)
    bias = bf_ref[...]

    def group(i, carry):
        rows = pl.ds(pl.multiple_of(i * grp, grp), grp)
        gf = gf_ref[rows, :][:, :GLA_GATE_RANK]
        pre = jnp.dot(gf.astype(BF16), wf, preferred_element_type=F32) + bias
        g = _log_sigmoid(pre) / GLA_GATE_TAU
        g_hi = g.astype(BF16)
        g_md = (g - g_hi.astype(F32)).astype(BF16)
        g_lo = (g - g_hi.astype(F32) - g_md.astype(F32)).astype(BF16)
        b3 = jnp.dot(causal_b, jnp.concatenate([g_hi, g_md, g_lo], axis=1), preferred_element_type=F32)
        b = b3[:, :GLA_DK] + b3[:, GLA_DK:2 * GLA_DK] + b3[:, 2 * GLA_DK:]
        row_of = lambda r0: jnp.concatenate(
            [jnp.broadcast_to(b[j * c + r0:j * c + r0 + 1, :], (c, GLA_DK)) for j in range(cpg)], axis=0)
        bm = row_of(c // 2 - 1)
        be = row_of(c - 1)
        q = q_ref[rows, :] * scale
        k = k_ref[rows, :]
        v = v_ref[rows, :].astype(BF16)
        qe = (q * jnp.exp(b - bm)).astype(BF16)
        ke = (k * jnp.exp(bm - b)).astype(BF16)
        att = jnp.where(causal, _nt(qe, ke), 0.0)
        o_s[rows, :] = jnp.dot(att.astype(BF16), v, preferred_element_type=F32)
        qd_s[rows, :] = (q * jnp.exp(b)).astype(BF16)
        kd = (k * jnp.exp(be - b)).astype(BF16)
        dec = jnp.exp(be)
        for j in range(cpg):
            cj = i * cpg + j
            u_s[cj] = _tn(v[j * c:(j + 1) * c, :], kd[j * c:(j + 1) * c, :])
            dec_s[cj] = dec[j * c:j * c + SUBLANES, :]
        return carry

    lax.fori_loop(0, t // grp, group, 0, unroll=True)

    def advance(cj, st):
        st_s[cj] = st.astype(BF16)
        return st * dec_s[cj][0:1, :] + u_s[cj]

    st = lax.fori_loop(0, n, advance, s0_ref[...].T, unroll=True)
    s_ref[...] = st.T

    on = on_ref[...]

    def inter(cj, carry):
        rows = pl.ds(pl.multiple_of(cj * c, c), c)
        o = o_s[rows, :] + _nt(qd_s[rows, :], st_s[cj])
        a_ref[rows, :] = (_rms_rows(o, on) * _silu(gg_ref[rows, :])).astype(a_ref.dtype)
        return carry

    lax.fori_loop(0, n, inter, 0, unroll=True)


def _gla_small_body(q_ref, k_ref, v_ref, gf_ref, wf_ref, bf_ref, s0_ref, o_ref, s_ref):
    c = q_ref.shape[0]
    scale = GLA_DK ** -0.5
    q = q_ref[...] * scale
    k = k_ref[...]
    v = v_ref[...]
    pre = bf_ref[...] + jnp.zeros((c, GLA_DK), F32)
    for r in range(GLA_GATE_RANK):
        pre = pre + gf_ref[:, r:r + 1] * wf_ref[r:r + 1, :]
    g = _log_sigmoid(pre) / GLA_GATE_TAU
    row = lax.broadcasted_iota(I32, (c, GLA_DK), 0)
    b = jnp.zeros((c, GLA_DK), F32)
    for s in range(c):
        b = b + jnp.where(row >= s, g[s:s + 1, :], 0.0)
    be = b[c - 1:c, :]
    s0 = s0_ref[...]
    o = jnp.dot(q * jnp.exp(b), s0, preferred_element_type=F32)
    for s in range(c):
        e = jnp.exp(jnp.where(row >= s, b - b[s:s + 1, :], NEG_INF))
        a_col = jnp.sum(q * k[s:s + 1, :] * e, axis=-1, keepdims=True)
        o = o + a_col * v[s:s + 1, :]
    o_ref[...] = o
    kd = k * jnp.exp(be - b)
    pad = jnp.concatenate([kd, jnp.exp(be), jnp.zeros((LANES - c - 1, GLA_DK), F32)], axis=0)
    padt = pad.T
    s_new = s0 * padt[:, c:c + 1]
    for s in range(c):
        s_new = s_new + padt[:, s:s + 1] * v[s:s + 1, :]
    s_ref[...] = s_new


def _gla(p, gf, w_f2, b_f, s0, o_norm, *, nb, t):
    small = t % GLA_CHUNK != 0
    kq, kk, kv, kg = COL_GQ // GLA_DK, COL_GK // GLA_DK, COL_GV // GLA_DV, COL_GG // GLA_DV
    in_specs = [
        pl.BlockSpec((t, GLA_DK), lambda b, h: (b, kq + h)),
        pl.BlockSpec((t, GLA_DK), lambda b, h: (b, kk + h)),
        pl.BlockSpec((t, GLA_DV), lambda b, h: (b, kv + h)),
        pl.BlockSpec((t, LANES), lambda b, h: (b, 0)),
        pl.BlockSpec((GLA_GATE_RANK, GLA_DK), lambda b, h: (0, h)),
        pl.BlockSpec((1, GLA_DK), lambda b, h: (0, h)),
        pl.BlockSpec((None, None, GLA_DK, GLA_DV), lambda b, h: (b, h, 0, 0)),
    ]
    operands = [p, p, p, gf, w_f2, b_f.reshape(1, -1), s0]
    state_spec = pl.BlockSpec((None, None, GLA_DK, GLA_DV), lambda b, h: (b, h, 0, 0))
    state_shape = jax.ShapeDtypeStruct((nb, GLA_HEADS, GLA_DK, GLA_DV), F32)
    if small:
        body, scratch, out_dtype = _gla_small_body, [], F32
    else:
        body, out_dtype = functools.partial(_gla_body, chunk=GLA_CHUNK), BF16
        in_specs += [pl.BlockSpec((t, GLA_DV), lambda b, h: (b, kg + h)),
                     pl.BlockSpec((1, GLA_DV), lambda b, h: (0, 0))]
        operands += [p, o_norm.reshape(1, -1)]
        n = t // GLA_CHUNK
        scratch = [
            pltpu.VMEM((t, GLA_DK), BF16),
            pltpu.VMEM((n, GLA_DV, GLA_DK), F32),
            pltpu.VMEM((n, SUBLANES, GLA_DK), F32),
            pltpu.VMEM((n, GLA_DV, GLA_DK), BF16),
            pltpu.VMEM((t, GLA_DV), F32),
        ]
    return pl.pallas_call(
        body,
        grid=(nb, GLA_HEADS),
        in_specs=in_specs,
        out_specs=[pl.BlockSpec((t, GLA_DV), lambda b, h: (b, h)), state_spec],
        out_shape=[jax.ShapeDtypeStruct((nb * t, GLA_VW), out_dtype), state_shape],
        scratch_shapes=scratch,
        compiler_params=_params("parallel", "parallel"),
        name="gla_small" if small else "gla",
    )(*operands)


def _moba_prompt_body(q_ref, k_ref, v_ref, mg_ref, *rest, nblk, ncast):
    cast_in, o_ref, cast_out = rest[:ncast], rest[ncast], rest[ncast + 1:]
    for wi, wo in zip(cast_in, cast_out):
        wo[...] = wi[...].astype(BF16)

    blk = MOBA_BLOCK
    t = nblk * blk
    shift = blk.bit_length() - 1
    assert 1 << shift == blk and nblk <= LANES and nblk % MOBA_PAIR == 0
    scale = MOBA_DH ** -0.5
    q = q_ref[...]
    k = k_ref[...]
    qb = (q * scale).astype(BF16)
    kb = k.astype(BF16)
    vb = v_ref[...].astype(BF16)

    km = jnp.concatenate([jnp.mean(k[n * blk:(n + 1) * blk, :], axis=0, keepdims=True) for n in range(nblk)],
                         axis=0)
    gate = lax.dot_general(km, q, (((1,), (1,)), ((), ())),
                           precision=lax.Precision.HIGHEST, preferred_element_type=F32)
    nrow = lax.broadcasted_iota(I32, (nblk, t), 0)
    qblk = jnp.right_shift(lax.broadcasted_iota(I32, (nblk, t), 1), shift)
    valid = nrow < qblk
    gm = jnp.where(valid, gate, NEG_INF)
    rank = jnp.zeros((nblk, t), I32)
    for m in range(nblk):
        g_m = gm[m:m + 1, :]
        beats = jnp.logical_or(g_m > gm, jnp.logical_and(g_m == gm, m < nrow))
        rank = rank + beats.astype(I32)
    sel = jnp.logical_and(valid, rank < MOBA_TOPK).astype(F32)
    sel_c = jnp.concatenate([sel, jnp.zeros((LANES - nblk, t), F32)], axis=0).T
    bias_c = jnp.where(sel_c > 0.0, 0.0, NEG_INF)

    causal = lax.broadcasted_iota(I32, (blk, blk), 1) <= lax.broadcasted_iota(I32, (blk, blk), 0)
    masked = jnp.full((blk, blk), NEG_INF, F32)
    pair = MOBA_PAIR
    for v in range(nblk // pair):
        lo = pair * v * blk
        nbi = pair * (v + 1)
        s = _nt(qb[lo:lo + pair * blk, :], kb[0:nbi * blk, :])
        bias = bias_c[lo:lo + pair * blk, :]
        pieces = []
        for n in range(nbi):
            sn = s[:, n * blk:(n + 1) * blk]
            if n < pair * v:
                pieces.append(sn + bias[:, n:n + 1])
                continue
            j = n - pair * v
            parts = []
            for i in range(pair):
                sni = sn[i * blk:(i + 1) * blk, :]
                if i == j:
                    parts.append(jnp.where(causal, sni, NEG_INF))
                elif i > j:
                    parts.append(sni + bias[i * blk:(i + 1) * blk, n:n + 1])
                else:
                    parts.append(masked)
            pieces.append(jnp.concatenate(parts, axis=0))
        s = jnp.concatenate(pieces, axis=1)
        m = jnp.max(s, axis=-1, keepdims=True)
        p = jnp.exp(s - m)
        l = jnp.sum(p, axis=-1, keepdims=True)
        acc = jnp.dot(p.astype(BF16), vb[0:nbi * blk, :], preferred_element_type=F32)
        rows = slice(lo, lo + pair * blk)
        o_ref[rows, :] = (acc / l * _silu(mg_ref[rows, :])).astype(o_ref.dtype)


def _moba_prompt(p, to_cast, *, nb, t):
    nblk = t // MOBA_BLOCK
    kq, kk, kv, kg = COL_MQ // MOBA_DH, COL_MK // MOBA_DH, COL_MV // MOBA_DH, COL_MG // MOBA_DH
    steps = nb * MOBA_HEADS
    bf16_rows = 2 * SUBLANES
    cast_specs = []
    for w in to_cast:
        assert w.shape[0] % (steps * bf16_rows) == 0
        cast_specs.append(pl.BlockSpec((w.shape[0] // steps, w.shape[1]), lambda b, h: (b * MOBA_HEADS + h, 0)))
    outs = pl.pallas_call(
        functools.partial(_moba_prompt_body, nblk=nblk, ncast=len(to_cast)),
        grid=(nb, MOBA_HEADS),
        in_specs=[
            pl.BlockSpec((t, MOBA_DH), lambda b, h: (b, kq + h)),
            pl.BlockSpec((t, MOBA_DH), lambda b, h: (b, kk + h)),
            pl.BlockSpec((t, MOBA_DH), lambda b, h: (b, kv + h)),
            pl.BlockSpec((t, MOBA_DH), lambda b, h: (b, kg + h)),
        ] + cast_specs,
        out_specs=[pl.BlockSpec((t, MOBA_DH), lambda b, h: (b, h))] + cast_specs,
        out_shape=[jax.ShapeDtypeStruct((nb * t, MOBA_W), BF16)]
        + [jax.ShapeDtypeStruct(w.shape, BF16) for w in to_cast],
        compiler_params=_params("parallel", "parallel"),
        name="moba_prompt",
    )(p, p, p, p, *to_cast)
    return outs[0], list(outs[1:])


def _bmean_body(pt_ref, *refs, ppb):
    pages, o_ref = refs[:-1], refs[-1]
    for j in range(len(pages) // ppb):
        tot = jnp.sum(pages[j * ppb][...], axis=0)
        for r in pages[j * ppb + 1:(j + 1) * ppb]:
            tot = tot + jnp.sum(r[...], axis=0)
        o_ref[j] = tot * (1.0 / MOBA_BLOCK)


def _block_means(cache_k, page_table, layer):
    nb, n_pages = page_table.shape
    ppb = MOBA_BLOCK // PAGE_SIZE
    n_full = n_pages // ppb
    bps = BMEAN_BLOCKS_PER_STEP
    assert n_full % bps == 0
    pps = bps * ppb
    page_block = (None, None, PAGE_SIZE, MOBA_HEADS, MOBA_DH)
    page_spec = lambda j: pl.BlockSpec(page_block, lambda b, n, pt: (layer, pt[b, pps * n + j], 0, 0, 0))
    return pl.pallas_call(
        functools.partial(_bmean_body, ppb=ppb),
        grid_spec=pltpu.PrefetchScalarGridSpec(
            num_scalar_prefetch=1,
            grid=(nb, n_full // bps),
            in_specs=[page_spec(j) for j in range(pps)],
            out_specs=pl.BlockSpec((None, bps, MOBA_HEADS, MOBA_DH), lambda b, n, pt: (b, n, 0, 0)),
        ),
        out_shape=jax.ShapeDtypeStruct((nb, n_full, MOBA_HEADS, MOBA_DH), F32),
        compiler_params=_params("parallel", "arbitrary"),
        name="moba_block_means",
    )(page_table, *([cache_k] * pps))


def _gate_topk_body(q_ref, bm_ref, o_ref, *, td):
    n_full = bm_ref.shape[1]
    lane = lax.broadcasted_iota(I32, (n_full, LANES), 1)
    row = lax.broadcasted_iota(I32, (n_full, LANES), 0)
    orow = lax.broadcasted_iota(I32, (SUBLANES, LANES), 0)
    for h in range(MOBA_HEADS):
        km = bm_ref[h]
        g = jnp.full((n_full, LANES), NEG_INF, F32)
        for t in range(td):
            qv = q_ref[t:t + 1, h * MOBA_DH:(h + 1) * MOBA_DH]
            g = jnp.where(lane == t, jnp.sum(km * qv, axis=-1, keepdims=True), g)
        outv = jnp.zeros((SUBLANES, LANES), I32)
        for j in range(MOBA_TOPK):
            mx = jnp.max(g, axis=0, keepdims=True)
            idx = jnp.min(jnp.where(g == mx, row, n_full), axis=0, keepdims=True)
            g = jnp.where(row == idx, NEG_INF, g)
            outv = jnp.where(orow == j, idx, outv)
        o_ref[h] = outv


def _gate_topk(p, bmean_t, *, nb, td):
    n_full = bmean_t.shape[2]
    assert td <= SUBLANES and MOBA_TOPK <= SUBLANES
    kq = COL_MQ // MOBA_W
    return pl.pallas_call(
        functools.partial(_gate_topk_body, td=td),
        grid=(nb,),
        in_specs=[
            pl.BlockSpec((td, MOBA_W), lambda b: (b, kq)),
            pl.BlockSpec((None, MOBA_HEADS, n_full, MOBA_DH), lambda b: (b, 0, 0, 0)),
        ],
        out_specs=pl.BlockSpec((None, MOBA_HEADS, SUBLANES, LANES), lambda b: (b, 0, 0, 0)),
        out_shape=jax.ShapeDtypeStruct((nb, MOBA_HEADS, SUBLANES, LANES), I32),
        compiler_params=_params("parallel"),
        name="moba_gate_topk",
    )(p, bmean_t)


def _moba_sample_body(idx_ref, pt_ref, q_ref, k_ref, v_ref, ck_ref, cv_ref, o_ref, kbuf, vbuf, sem, *, layer, td):
    step = pl.program_id(0)
    nsteps = pl.num_programs(0)
    slot = lax.rem(step, 2)
    ppb = MOBA_BLOCK // PAGE_SIZE
    per_q = MOBA_TOPK * ppb
    nsel = MOBA_TOPK * MOBA_BLOCK
    scale = MOBA_DH ** -0.5
    slots = [(t, j, pg) for t in range(td) for j in range(MOBA_TOPK) for pg in range(ppb)]

    def copies(st, sl, t, j, pg):
        b = lax.div(st, MOBA_HEADS)
        h = lax.rem(st, MOBA_HEADS)
        blk = idx_ref[(st * td + t) * MOBA_TOPK + j]
        page = pt_ref[b, blk * ppb + pg]
        dst = pl.ds((t * per_q + j * ppb + pg) * PAGE_SIZE, PAGE_SIZE)
        return (pltpu.make_async_copy(ck_ref.at[layer, page, :, h, :], kbuf.at[sl, dst, :], sem.at[sl, 0]),
                pltpu.make_async_copy(cv_ref.at[layer, page, :, h, :], vbuf.at[sl, dst, :], sem.at[sl, 1]))

    def issue(st, sl):
        for c in slots:
            ck, cv = copies(st, sl, *c)
            ck.start()
            cv.start()

    @pl.when(step == 0)
    def _():
        issue(step, slot)

    @pl.when(step + 1 < nsteps)
    def _():
        issue(step + 1, 1 - slot)

    for c in slots:
        ck, cv = copies(step, slot, *c)
        ck.wait()
        cv.wait()

    rows = 2 * SUBLANES
    q = q_ref[...]
    qp = jnp.concatenate([q, jnp.zeros((rows - td, MOBA_DH), F32)], axis=0).astype(BF16)
    s_sel = _nt(qp, kbuf[slot].astype(BF16)) * scale
    r = lax.broadcasted_iota(I32, (rows, td * nsel), 0)
    c = lax.broadcasted_iota(I32, (rows, td * nsel), 1)
    mine = jnp.logical_and(c >= r * nsel, c < (r + 1) * nsel)
    s_sel = jnp.where(mine, s_sel, NEG_INF)

    k_new = k_ref[...]
    v_new = v_ref[...]
    rn = lax.broadcasted_iota(I32, (td, LANES), 0)
    cn = lax.broadcasted_iota(I32, (td, LANES), 1)
    s_new = jnp.full((td, LANES), NEG_INF, F32)
    for t in range(td):
        col = jnp.sum(q * k_new[t:t + 1, :], axis=-1, keepdims=True) * scale
        s_new = jnp.where(jnp.logical_and(cn == t, rn >= t), col, s_new)

    s_sel = s_sel[:td, :]
    m = jnp.maximum(jnp.max(s_sel, axis=-1, keepdims=True), jnp.max(s_new, axis=-1, keepdims=True))
    p_sel = jnp.exp(s_sel - m)
    p_new = jnp.exp(s_new - m)
    l = jnp.sum(p_sel, axis=-1, keepdims=True) + jnp.sum(p_new, axis=-1, keepdims=True)
    pp = jnp.concatenate([p_sel, jnp.zeros((rows - td, td * nsel), F32)], axis=0).astype(BF16)
    acc = jnp.dot(pp, vbuf[slot].astype(BF16), preferred_element_type=F32)[:td, :]
    for t in range(td):
        acc = acc + p_new[:, t:t + 1] * v_new[t:t + 1, :]
    o_ref[...] = acc / l


def _moba_sample(p, cache_k, cache_v, page_table, idx_flat, *, layer, nb, td):
    kq, kk, kv = COL_MQ // MOBA_DH, COL_MK // MOBA_DH, COL_MV // MOBA_DH
    nrows = td * MOBA_TOPK * MOBA_BLOCK
    row_spec = lambda off: pl.BlockSpec(
        (td, MOBA_DH), lambda s, idx, pt: (lax.div(s, MOBA_HEADS), off + lax.rem(s, MOBA_HEADS)))
    return pl.pallas_call(
        functools.partial(_moba_sample_body, layer=layer, td=td),
        grid_spec=pltpu.PrefetchScalarGridSpec(
            num_scalar_prefetch=2,
            grid=(nb * MOBA_HEADS,),
            in_specs=[
                row_spec(kq), row_spec(kk), row_spec(kv),
                pl.BlockSpec(memory_space=pl.ANY),
                pl.BlockSpec(memory_space=pl.ANY),
            ],
            out_specs=row_spec(0),
            scratch_shapes=[
                pltpu.VMEM((2, nrows, MOBA_DH), F32),
                pltpu.VMEM((2, nrows, MOBA_DH), F32),
                pltpu.SemaphoreType.DMA((2, 2)),
            ],
        ),
        out_shape=jax.ShapeDtypeStruct((nb * td, MOBA_W), F32),
        compiler_params=_params("arbitrary"),
        name="moba_sample",
    )(idx_flat, page_table, p, p, p, cache_k, cache_v)


def _merge_body(go_ref, gg_ref, mo_ref, mg_ref, on_ref, a_ref):
    on = on_ref[...]
    for h in range(GLA_HEADS):
        sl = slice(h * GLA_DV, (h + 1) * GLA_DV)
        g = _rms_rows(go_ref[:, sl], on) * _silu(gg_ref[:, sl])
        a_ref[:, sl] = g.astype(a_ref.dtype)
    a_ref[:, GLA_VW:] = (mo_ref[...] * _silu(mg_ref[...])).astype(a_ref.dtype)


def _merge(gla_o, moba_o, p, o_norm, *, tm):
    m = gla_o.shape[0]
    kg, km = COL_GG // GLA_VW, COL_MG // MOBA_W
    return pl.pallas_call(
        _merge_body,
        grid=(m // tm,),
        in_specs=[
            pl.BlockSpec((tm, GLA_VW), lambda i: (i, 0)),
            pl.BlockSpec((tm, GLA_VW), lambda i: (i, kg)),
            pl.BlockSpec((tm, MOBA_W), lambda i: (i, 0)),
            pl.BlockSpec((tm, MOBA_W), lambda i: (i, km)),
            pl.BlockSpec((1, GLA_DV), lambda i: (0, 0)),
        ],
        out_specs=pl.BlockSpec((tm, GLA_VW + MOBA_W), lambda i: (i, 0)),
        out_shape=jax.ShapeDtypeStruct((m, GLA_VW + MOBA_W), BF16),
        compiler_params=_params("parallel"),
        name="a_merge",
    )(gla_o, p, moba_o, p, o_norm.reshape(1, -1))


def _s5_coef_body(lr_ref, li_ref, ldt_ref, br_ref, bi_ref, bbr_ref, bbi_ref, abr_ref, abi_ref):
    lr = lr_ref[...]
    li = li_ref[...]
    dt = jnp.exp(ldt_ref[...])
    mag = jnp.exp(lr * dt)
    ab_re = mag * jnp.cos(li * dt)
    ab_im = mag * jnp.sin(li * dt)
    den = lr * lr + li * li
    nr = ab_re - 1.0
    cr = (nr * lr + ab_im * li) / den
    ci = (ab_im * lr - nr * li) / den
    br = br_ref[...]
    bi = bi_ref[...]
    bbr_ref[...] = cr * br - ci * bi
    bbi_ref[...] = cr * bi + ci * br
    abr_ref[...] = ab_re
    abi_ref[...] = ab_im


def _s5_coef(lam_re, lam_im, log_dt, b_re, b_im):
    g, p = lam_re.shape
    w = p * S5_GROUP
    rep = lambda a: jnp.repeat(a, S5_GROUP, axis=1)
    full = pl.BlockSpec((g, w), lambda: (0, 0))
    outs = pl.pallas_call(
        _s5_coef_body,
        in_specs=[full] * 5,
        out_specs=[full] * 4,
        out_shape=[jax.ShapeDtypeStruct((g, w), F32)] * 4,
        name="s5_coef",
    )(rep(lam_re), rep(lam_im), jnp.broadcast_to(log_dt[:, None], (g, w)),
      b_re.reshape(g, w), b_im.reshape(g, w))
    bbr, bbi, abr, abi = outs
    return (bbr.reshape(g, p, S5_GROUP), bbi.reshape(g, p, S5_GROUP),
            abr[:, ::S5_GROUP], abi[:, ::S5_GROUP])


def _s5_expand(w2):
    per_tile = LANES // S5_P
    rg = jnp.right_shift(lax.broadcasted_iota(I32, (S5_SLAB, LANES), 0), S5_GROUP.bit_length() - 1)
    lg = jnp.right_shift(lax.broadcasted_iota(I32, (S5_SLAB, LANES), 1), S5_P.bit_length() - 1)
    pieces = []
    for part in range(2):
        w = w2[:, part * LANES:(part + 1) * LANES]
        for kk in range(S5_SLAB_STATES // LANES):
            pieces.append(jnp.where(rg == per_tile * kk + lg, w, 0.0))
    return jnp.concatenate(pieces, axis=1)


def _s5_body(u_ref, bb_ref, cc_ref, ar_ref, ai_ref, d_ref, x0r_ref, x0i_ref,
             y_ref, xr_ref, xi_ref, lhs, bu, ybuf, cr, ci, bd_in, bd_out, *, ns, nt):
    ti = pl.program_id(1)
    tpv = SUBLANES // ns
    nst = S5_SLAB_STATES
    lo_rows = SUBLANES - ns
    nlb = S5_SLAB // LANES

    @pl.when(ti == 0)
    def _():
        cr[...] = jnp.zeros_like(cr)
        ci[...] = jnp.zeros_like(ci)
        cr[lo_rows:, :] = x0r_ref[...]
        ci[lo_rows:, :] = x0i_ref[...]
        e_in = _s5_expand(bb_ref[...])
        bd_in[:S5_SLAB, :] = e_in.astype(BF16)
        if tpv == 2:
            e_re, e_im = e_in[:, :nst], e_in[:, nst:]
            a_re, a_im = ar_ref[...], ai_ref[...]
            bd_in[S5_SLAB:, :] = jnp.concatenate([a_re * e_re - a_im * e_im, a_re * e_im + a_im * e_re],
                                                 axis=1).astype(BF16)
        bd_out[...] = _s5_expand(cc_ref[...]).astype(BF16)

    for b in range(ns):
        for j in range(nlb):
            lhs[j, pl.ds(b, nt, stride=ns), :] = u_ref[b, :, j * LANES:(j + 1) * LANES]
    u_rows = jnp.concatenate([lhs[j] for j in range(nlb)], axis=1)
    if tpv == 2:
        tbit = jnp.right_shift(lax.broadcasted_iota(I32, u_rows.shape, 0), ns.bit_length() - 1)
        u_prev = jnp.where(jnp.bitwise_and(tbit, 1) == 1, pltpu.roll(u_rows, ns, 0), 0.0)
        lhs_rows = jnp.concatenate([u_rows, u_prev], axis=1)
    else:
        lhs_rows = u_rows
    bu[...] = jnp.dot(lhs_rows.astype(BF16), bd_in[...], preferred_element_type=F32)

    ar = ar_ref[...]
    ai = ai_ref[...]
    if tpv == 1:
        def step(v, carry):
            xr, xi = carry
            sl = pl.ds(pl.multiple_of(v * SUBLANES, SUBLANES), SUBLANES)
            nxr = ar * xr - ai * xi + bu[sl, :nst]
            nxi = ar * xi + ai * xr + bu[sl, nst:]
            bu[sl, :nst] = nxr
            bu[sl, nst:] = nxi
            return nxr, nxi
    else:
        assert tpv == 2
        lo = lax.broadcasted_iota(I32, (SUBLANES, nst), 0) < ns
        c2r = jnp.where(lo, ar, ar * ar - ai * ai)
        c2i = jnp.where(lo, ai, 2.0 * ar * ai)

        def step(v, carry):
            xr, xi = carry
            sl = pl.ds(pl.multiple_of(v * SUBLANES, SUBLANES), SUBLANES)
            pr = jnp.where(lo, pltpu.roll(xr, ns, 0), xr)
            pi = jnp.where(lo, pltpu.roll(xi, ns, 0), xi)
            nxr = bu[sl, :nst] + c2r * pr - c2i * pi
            nxi = bu[sl, nst:] + c2r * pi + c2i * pr
            bu[sl, :nst] = nxr
            bu[sl, nst:] = nxi
            return nxr, nxi

    nv = (nt * ns) // SUBLANES
    xr, xi = lax.fori_loop(0, nv, step, (cr[...], ci[...]), unroll=True)
    cr[...] = xr
    ci[...] = xi

    yv = _nt(bu[...].astype(BF16), bd_out[...]) + d_ref[...] * u_rows
    yv = 0.5 * yv * (1.0 + lax.erf(yv * math.sqrt(0.5)))
    for j in range(nlb):
        ybuf[j] = yv[:, j * LANES:(j + 1) * LANES]
    for b in range(ns):
        for j in range(nlb):
            y_ref[b, :, j * LANES:(j + 1) * LANES] = ybuf[j, pl.ds(b, nt, stride=ns), :]

    @pl.when(ti == pl.num_programs(1) - 1)
    def _():
        xr_ref[...] = cr[lo_rows:, :]
        xi_ref[...] = ci[lo_rows:, :]


def _s5_core(uz, bb, cc, ab_re, ab_im, d, x0_re, x0_im, *, ns, t, nt):
    w = d.shape[0]
    nslab = w // S5_SLAB
    nst = S5_SLAB_STATES
    assert SUBLANES % ns == 0 and t % nt == 0 and (nt * ns) % SUBLANES == 0
    u3 = uz.reshape(ns, t, uz.shape[1])
    to_slab = lambda a: jnp.transpose(a.reshape(ns, nslab, nst), (1, 0, 2))
    st_spec = pl.BlockSpec((None, ns, nst), lambda s, i: (s, 0, 0))
    y, xr, xi = pl.pallas_call(
        functools.partial(_s5_body, ns=ns, nt=nt),
        grid=(nslab, t // nt),
        in_specs=[
            pl.BlockSpec((ns, nt, S5_SLAB), lambda s, i: (0, i, s)),
            pl.BlockSpec((None, S5_SLAB, 2 * LANES), lambda s, i: (s, 0, 0)),
            pl.BlockSpec((None, S5_SLAB, 2 * LANES), lambda s, i: (s, 0, 0)),
            pl.BlockSpec((None, 1, nst), lambda s, i: (s, 0, 0)),
            pl.BlockSpec((None, 1, nst), lambda s, i: (s, 0, 0)),
            pl.BlockSpec((1, S5_SLAB), lambda s, i: (0, s)),
            st_spec, st_spec,
        ],
        out_specs=[
            pl.BlockSpec((ns, nt, S5_SLAB), lambda s, i: (0, i, s)),
            st_spec, st_spec,
        ],
        out_shape=[
            jax.ShapeDtypeStruct((ns, t, w), F32),
            jax.ShapeDtypeStruct((nslab, ns, nst), F32),
            jax.ShapeDtypeStruct((nslab, ns, nst), F32),
        ],
        scratch_shapes=[
            pltpu.VMEM((S5_SLAB // LANES, nt * ns, LANES), F32),
            pltpu.VMEM((nt * ns, 2 * nst), F32),
            pltpu.VMEM((S5_SLAB // LANES, nt * ns, LANES), F32),
            pltpu.VMEM((SUBLANES, nst), F32),
            pltpu.VMEM((SUBLANES, nst), F32),
            pltpu.VMEM((S5_SLAB * (SUBLANES // ns), 2 * nst), BF16),
            pltpu.VMEM((S5_SLAB, 2 * nst), BF16),
        ],
        compiler_params=_params("parallel", "arbitrary"),
        name="s5_core",
    )(u3, bb, cc, ab_re.reshape(nslab, 1, nst), ab_im.reshape(nslab, 1, nst), d.reshape(1, w),
      to_slab(x0_re), to_slab(x0_im))
    g = w // S5_GROUP
    from_slab = lambda a: jnp.transpose(a, (1, 0, 2)).reshape(ns, g, S5_P)
    return y.reshape(ns * t, w), from_slab(xr), from_slab(xi)


def _s5_compact(re, im):
    reps = LANES // S5_P
    flat = lambda a: jnp.tile(a.reshape(-1, S5_SLAB, S5_P), (1, 1, reps))
    return jnp.concatenate([flat(re), flat(im)], axis=2)


def _rope_tables(pos, reps):
    half = MOBA_DH // 2
    inv_freq = ROPE_THETA ** (-jnp.arange(half, dtype=F32) / half)
    ang = pos.astype(F32)[:, None] * inv_freq[None, :]
    cos = jnp.cos(ang)
    sin = jnp.sin(ang)
    cos2 = jnp.concatenate([cos, cos], axis=-1)
    sin2 = jnp.concatenate([-sin, sin], axis=-1)
    return jnp.tile(cos2, (reps, 1)), jnp.tile(sin2, (reps, 1))


def _layer_a(x, pos, nb, t, s0, weights, w_out, to_cast, sample_ctx):
    norm, w_main, w_gf, w_f2, b_f, o_norm, q_norm, k_norm = weights
    cast = []
    m = nb * t
    tm = min(m, GEMM_TM)
    cos, sin = _rope_tables(pos, nb)
    p, gf = _aproj(x, norm, w_main, w_gf, q_norm, k_norm, cos, sin, tm=tm, tn=512)
    gla_o, state = _gla(p, gf, w_f2, b_f, s0, o_norm, nb=nb, t=t)
    if sample_ctx is None:
        moba_a, cast = _moba_prompt(p, to_cast, nb=nb, t=t)
        x_new = _gemm_res2(gla_o, moba_a, cast[0], x, tm=tm, tn=512)
    else:
        cache_k, cache_v, page_table, layer = sample_ctx
        bmean = _block_means(cache_k, page_table, layer)
        ids = _gate_topk(p, jnp.transpose(bmean, (0, 2, 1, 3)), nb=nb, td=t)
        idx_flat = jnp.transpose(ids[:, :, :MOBA_TOPK, :t], (0, 1, 3, 2)).reshape(-1)
        moba_o = _moba_sample(p, cache_k, cache_v, page_table, idx_flat, layer=layer, nb=nb, td=t)
        a = _merge(gla_o, moba_o, p, o_norm, tm=min(m, 256))
        x_new = _gemm_res(a, w_out, x, tm=tm, tn=512)
    mk = p[:, COL_MK:COL_MK + MOBA_W].reshape(nb, t, MOBA_HEADS, MOBA_DH)
    mv = p[:, COL_MV:COL_MV + MOBA_W].reshape(nb, t, MOBA_HEADS, MOBA_DH)
    return x_new, mk, mv, state, cast


def _layer_c(x, ns, t, x0_re, x0_im, weights):
    norm, w_in, bb, cc, ab_re, ab_im, d, w_glu, b_glu, w_out = weights
    m = ns * t
    tm = min(m, GEMM_TM)
    uz = _gemm_norm(x, norm, w_in, tm=tm, tn=512)
    y, xr, xi = _s5_core(uz, bb, cc, ab_re, ab_im, d, x0_re, x0_im, ns=ns, t=t, nt=min(t, S5_TIME_TILE))
    v = _gemm_glu(y, w_glu, uz, b_glu, tm=tm, tn=512)
    x_new = _gemm_res(v, w_out, x, tm=tm, tn=512)
    return x_new, xr, xi


def kernel(x_prompt, x_sample, cache_k, cache_v, state_gla, state_s5_re, state_s5_im, page_table, norm_a, w_in_a, w_gla_f2, b_gla_f, gla_out_norm, moba_q_norm, moba_k_norm, w_out_a, norm_c, w_in_c, s5_lambda_re, s5_lambda_im, s5_log_dt, s5_b_re, s5_b_im, s5_c_re, s5_c_im, s5_d, w_glu, b_glu, w_out_c):
    nbp, tp, d = x_prompt.shape
    nbs, ts, _ = x_sample.shape
    depth = norm_a.shape[0] + norm_c.shape[0]
    past_len = page_table.shape[1] * PAGE_SIZE
    assert past_len % MOBA_BLOCK == 0 and past_len // MOBA_BLOCK >= MOBA_TOPK
    assert tp % MOBA_BLOCK == 0 and tp % GLA_CHUNK == 0
    pos_p = jnp.arange(tp)
    pos_s = past_len + jnp.arange(ts)
    xp = x_prompt.reshape(nbp * tp, d)
    xs = x_sample.reshape(nbs * ts, d)
    g = d // S5_GROUP
    outs = [[] for _ in range(10)]
    for layer in range(depth):
        i = layer // 2
        if layer % 2 == 0:
            w_main, w_gf = _wprep(jnp.swapaxes(w_in_a[i], 0, 1))
            weights = (norm_a[i], w_main, w_gf, w_gla_f2[i], b_gla_f[i], gla_out_norm[i], moba_q_norm[i],
                       moba_k_norm[i])
            to_cast = [w_out_a[i]]
            if layer + 1 < depth:
                to_cast += [w_in_c[i], w_glu[i], w_out_c[i]]
            zero_state = jnp.zeros((nbp, GLA_HEADS, GLA_DK, GLA_DV), F32)
            xp, mk, mv, sp, cast = _layer_a(xp, pos_p, nbp, tp, zero_state, weights, None, to_cast, None)
            w_out_a_bf, c_weights_bf = cast[0], cast[1:]
            outs[0].append(mk)
            outs[1].append(mv)
            outs[4].append(sp)
            xs, mk, mv, ss, _ = _layer_a(xs, pos_s, nbs, ts, state_gla[i], weights, w_out_a_bf, None,
                                         (cache_k, cache_v, page_table, i))
            outs[2].append(mk)
            outs[3].append(mv)
            outs[5].append(ss)
        else:
            bb_re, bb_im, ab_re, ab_im = _s5_coef(s5_lambda_re[i], s5_lambda_im[i], s5_log_dt[i],
                                                  s5_b_re[i], s5_b_im[i])
            bb = _s5_compact(jnp.swapaxes(bb_re, 1, 2), jnp.swapaxes(bb_im, 1, 2))
            cc = _s5_compact(s5_c_re[i], -s5_c_im[i])
            w_in_bf, w_glu_bf, w_out_bf = c_weights_bf
            weights = (norm_c[i], w_in_bf, bb, cc, ab_re, ab_im, s5_d[i], w_glu_bf, b_glu[i], w_out_bf)
            zeros = jnp.zeros((nbp, g, S5_P), F32)
            xp, xr, xi = _layer_c(xp, nbp, tp, zeros, zeros, weights)
            outs[6].append(xr)
            outs[7].append(xi)
            xs, xr, xi = _layer_c(xs, nbs, ts, state_s5_re[i], state_s5_im[i], weights)
            outs[8].append(xr)
            outs[9].append(xi)
    pk, pv, sk, sv, pg, sg, psr, psi, ssr, ssi = [jnp.stack(o) for o in outs]
    return (xp.reshape(nbp, tp, d), xs.reshape(nbs, ts, d), pk, pv, sk, sv, pg, sg, psr, psi, ssr, ssi)
```

```python
import functools
import math

import jax
import jax.numpy as jnp
from jax import lax
from jax.experimental import pallas as pl
from jax.experimental.pallas import tpu as pltpu

F32 = jnp.float32
BF16 = jnp.bfloat16
I32 = jnp.int32

RMS_EPS = 1e-6
GLA_HEADS = 8
GLA_DK = 128
GLA_DV = 256
GLA_KW = GLA_HEADS * GLA_DK
GLA_VW = GLA_HEADS * GLA_DV
GLA_GATE_RANK = 16
GLA_GATE_TAU = 16.0
GLA_CHUNK = 64
GLA_GROUP_ROWS = 256
MOBA_HEADS = 16
MOBA_DH = 128
MOBA_W = MOBA_HEADS * MOBA_DH
MOBA_BLOCK = 256
MOBA_TOPK = 3
MOBA_PAIR = 1
ROPE_THETA = 10000.0
PAGE_SIZE = 128
BMEAN_BLOCKS_PER_STEP = 8
S5_GROUP = 16
S5_P = 64
S5_SLAB = 128
S5_SLAB_STATES = (S5_SLAB // S5_GROUP) * S5_P
S5_TIME_TILE = 1024

LANES = 128
SUBLANES = 8
VMEM_LIMIT = 52 * 1024 * 1024
NEG_INF = float("-inf")

COL_GQ = 0
COL_GK = COL_GQ + GLA_KW
COL_GV = COL_GK + GLA_KW
COL_GG = COL_GV + GLA_VW
COL_MQ = COL_GG + GLA_VW
COL_MK = COL_MQ + MOBA_W
COL_MV = COL_MK + MOBA_W
COL_MG = COL_MV + MOBA_W
A_COLS = COL_MG + MOBA_W
WPREP_ROWS = 256
GEMM_TM = 1024
EPILOGUE_ROWS = 1024
GEMM_SUB = 256
LHS_DOUBLE_BUFFER_MAX_BYTES = 8 * 1024 * 1024


def _params(*sem):
    return pltpu.CompilerParams(dimension_semantics=sem, vmem_limit_bytes=VMEM_LIMIT)


def _nt(a, b):
    return lax.dot_general(a, b, (((1,), (1,)), ((), ())), preferred_element_type=F32)


def _tn(a, b):
    return lax.dot_general(a, b, (((0,), (0,)), ((), ())), preferred_element_type=F32)


def _silu(x):
    return x / (1.0 + jnp.exp(-x))


def _log_sigmoid(x):
    return jnp.minimum(x, 0.0) - jnp.log(1.0 + jnp.exp(-jnp.abs(x)))


def _rms_rows(x, g):
    ms = jnp.mean(x * x, axis=-1, keepdims=True)
    return x * lax.rsqrt(ms + RMS_EPS) * g


def _lhs_spec(tm, k, itemsize):
    if tm * k * itemsize > LHS_DOUBLE_BUFFER_MAX_BYTES:
        return pl.BlockSpec((tm, k), lambda i, j: (i, 0), pipeline_mode=pl.Buffered(1))
    return pl.BlockSpec((tm, k), lambda i, j: (i, 0))


def _wprep_body(a_ref, b_ref, om_ref, of_ref, *, gate_blk):
    i = pl.program_id(0)
    gr = GLA_GATE_RANK

    @pl.when(i == 0)
    def _():
        of_ref[...] = jnp.zeros_like(of_ref)

    @pl.when(i < gate_blk)
    def _():
        om_ref[...] = a_ref[...].astype(BF16)

    @pl.when(i == gate_blk)
    def _():
        of_ref[:gr, :] = a_ref[:gr, :].astype(BF16)

    @pl.when(i >= gate_blk)
    def _():
        om_ref[:-gr, :] = a_ref[gr:, :].astype(BF16)
        om_ref[-gr:, :] = b_ref[...].astype(BF16)


def _wprep(wt):
    n, d = wt.shape
    tk = WPREP_ROWS
    gr = GLA_GATE_RANK
    gate_lo = COL_GG + GLA_VW
    assert n == A_COLS + gr and gate_lo % tk == 0 and A_COLS % tk == 0 and tk % gr == 0
    last_tail = (n - gr) // gr
    return pl.pallas_call(
        functools.partial(_wprep_body, gate_blk=gate_lo // tk),
        grid=(A_COLS // tk,),
        in_specs=[
            pl.BlockSpec((tk, d), lambda i: (i, 0)),
            pl.BlockSpec((gr, d), lambda i: (jnp.minimum((i + 1) * (tk // gr), last_tail), 0)),
        ],
        out_specs=[pl.BlockSpec((tk, d), lambda i: (i, 0)), pl.BlockSpec((LANES, d), lambda i: (0, 0))],
        out_shape=[jax.ShapeDtypeStruct((A_COLS, d), BF16), jax.ShapeDtypeStruct((LANES, d), BF16)],
        compiler_params=_params("arbitrary"),
        name="a_weight_prep",
    )(wt, wt)


def _aproj_body(x_ref, g_ref, w_ref, wf_ref, qn_ref, kn_ref, cos_ref, sin_ref, o_ref, gf_ref, h_ref, *, tn):
    j = pl.program_id(1)

    @pl.when(j == 0)
    def _():
        h_ref[...] = _rms_rows(x_ref[...], g_ref[...]).astype(BF16)
        gf_ref[...] = _nt(h_ref[...], wf_ref[...])

    o_ref[...] = _nt(h_ref[...], w_ref[...])

    q_lo, k_lo, k_hi = COL_MQ // tn, COL_MK // tn, COL_MV // tn

    @pl.when(jnp.logical_and(j >= q_lo, j < k_hi))
    def _():
        gain = jnp.where(j < k_lo, qn_ref[...], kn_ref[...])
        rb = min(o_ref.shape[0], EPILOGUE_ROWS)

        def rows_block(r, carry):
            rows = pl.ds(pl.multiple_of(r * rb, rb), rb)
            cos = cos_ref[rows, :]
            sin = sin_ref[rows, :]
            for hh in range(tn // MOBA_DH):
                cols = slice(hh * MOBA_DH, (hh + 1) * MOBA_DH)
                y = _rms_rows(o_ref[rows, cols], gain)
                o_ref[rows, cols] = y * cos + pltpu.roll(y, MOBA_DH // 2, 1) * sin
            return carry

        lax.fori_loop(0, o_ref.shape[0] // rb, rows_block, 0)


def _aproj(x, g, w_main, w_gf, qn, kn, cos, sin, *, tm, tn):
    m, d = x.shape
    n = w_main.shape[0]
    assert n == A_COLS and COL_MQ % tn == 0 and COL_MK % tn == 0 and COL_MV % tn == 0
    return pl.pallas_call(
        functools.partial(_aproj_body, tn=tn),
        grid=(m // tm, n // tn),
        in_specs=[
            _lhs_spec(tm, d, 4),
            pl.BlockSpec((1, d), lambda i, j: (0, 0)),
            pl.BlockSpec((tn, d), lambda i, j: (j, 0)),
            pl.BlockSpec((LANES, d), lambda i, j: (0, 0)),
            pl.BlockSpec((1, MOBA_DH), lambda i, j: (0, 0)),
            pl.BlockSpec((1, MOBA_DH), lambda i, j: (0, 0)),
            pl.BlockSpec((tm, MOBA_DH), lambda i, j: (i, 0)),
            pl.BlockSpec((tm, MOBA_DH), lambda i, j: (i, 0)),
        ],
        out_specs=[
            pl.BlockSpec((tm, tn), lambda i, j: (i, j)),
            pl.BlockSpec((tm, LANES), lambda i, j: (i, 0)),
        ],
        out_shape=[jax.ShapeDtypeStruct((m, n), F32), jax.ShapeDtypeStruct((m, LANES), F32)],
        scratch_shapes=[pltpu.VMEM((tm, d), BF16)],
        compiler_params=_params("parallel", "arbitrary"),
        name="a_proj",
    )(x, g.reshape(1, d), w_main, w_gf, qn.reshape(1, -1), kn.reshape(1, -1), cos, sin)


def _gemm_norm_body(x_ref, g_ref, w_ref, o_ref, h_ref):
    @pl.when(pl.program_id(1) == 0)
    def _():
        h_ref[...] = _rms_rows(x_ref[...], g_ref[...]).astype(BF16)

    o_ref[...] = jnp.dot(h_ref[...], w_ref[...], preferred_element_type=F32)


def _gemm_norm(x, g, w, *, tm, tn):
    m, d = x.shape
    n = w.shape[1]
    return pl.pallas_call(
        _gemm_norm_body,
        grid=(m // tm, n // tn),
        in_specs=[
            _lhs_spec(tm, d, 4),
            pl.BlockSpec((1, d), lambda i, j: (0, 0)),
            pl.BlockSpec((d, tn), lambda i, j: (0, j)),
        ],
        out_specs=pl.BlockSpec((tm, tn), lambda i, j: (i, j)),
        out_shape=jax.ShapeDtypeStruct((m, n), F32),
        scratch_shapes=[pltpu.VMEM((tm, d), BF16)],
        compiler_params=_params("parallel", "arbitrary"),
        name="c_proj",
    )(x, g.reshape(1, d), w)


def _gemm_res_body(a_ref, w_ref, r_ref, o_ref):
    o_ref[...] = r_ref[...] + jnp.dot(a_ref[...], w_ref[...], preferred_element_type=F32)


def _gemm_res2_body(a1_ref, a2_ref, w_ref, r_ref, o_ref):
    k1 = a1_ref.shape[1]
    acc = jnp.dot(a1_ref[...], w_ref[:k1, :], preferred_element_type=F32)
    o_ref[...] = r_ref[...] + acc + jnp.dot(a2_ref[...], w_ref[k1:, :], preferred_element_type=F32)


def _gemm_res2(a1, a2, w, res, *, tm, tn):
    m, k1 = a1.shape
    k2 = a2.shape[1]
    n = w.shape[1]
    return pl.pallas_call(
        _gemm_res2_body,
        grid=(m // tm, n // tn),
        in_specs=[
            pl.BlockSpec((tm, k1), lambda i, j: (i, 0)),
            pl.BlockSpec((tm, k2), lambda i, j: (i, 0)),
            pl.BlockSpec((k1 + k2, tn), lambda i, j: (0, j)),
            pl.BlockSpec((tm, tn), lambda i, j: (i, j)),
        ],
        out_specs=pl.BlockSpec((tm, tn), lambda i, j: (i, j)),
        out_shape=jax.ShapeDtypeStruct((m, n), F32),
        compiler_params=_params("parallel", "arbitrary"),
        name="out_proj",
    )(a1, a2, w, res)


def _gemm_res(a, w, res, *, tm, tn):
    m, k = a.shape
    n = w.shape[1]
    return pl.pallas_call(
        _gemm_res_body,
        grid=(m // tm, n // tn),
        in_specs=[
            pl.BlockSpec((tm, k), lambda i, j: (i, 0)),
            pl.BlockSpec((k, tn), lambda i, j: (0, j)),
            pl.BlockSpec((tm, tn), lambda i, j: (i, j)),
        ],
        out_specs=pl.BlockSpec((tm, tn), lambda i, j: (i, j)),
        out_shape=jax.ShapeDtypeStruct((m, n), F32),
        compiler_params=_params("parallel", "arbitrary"),
        name="out_proj",
    )(a, w, res)


def _gemm_glu_body(y_ref, w_ref, z_ref, b_ref, o_ref, h_ref, *, tn):
    j = pl.program_id(1)

    @pl.when(j == 0)
    def _():
        h_ref[...] = y_ref[...].astype(BF16)

    sub = min(tn, GEMM_SUB)
    for c in range(tn // sub):
        cols = slice(c * sub, (c + 1) * sub)
        t = jnp.dot(h_ref[...], w_ref[:, cols], preferred_element_type=F32) + b_ref[:, cols]
        y = y_ref[:, pl.ds(pl.multiple_of(j * tn + c * sub, sub), sub)]
        z = z_ref[:, cols]
        o_ref[:, cols] = (y * z / ((1.0 + jnp.exp(-t)) * (1.0 + jnp.exp(-z)))).astype(o_ref.dtype)


def _gemm_glu(y, w, uz, bias, *, tm, tn):
    m, k = y.shape
    n = w.shape[1]
    zoff = n // tn
    return pl.pallas_call(
        functools.partial(_gemm_glu_body, tn=tn),
        grid=(m // tm, n // tn),
        in_specs=[
            _lhs_spec(tm, k, 4),
            pl.BlockSpec((k, tn), lambda i, j: (0, j)),
            pl.BlockSpec((tm, tn), lambda i, j: (i, zoff + j)),
            pl.BlockSpec((1, tn), lambda i, j: (0, j)),
        ],
        out_specs=pl.BlockSpec((tm, tn), lambda i, j: (i, j)),
        out_shape=jax.ShapeDtypeStruct((m, n), BF16),
        scratch_shapes=[pltpu.VMEM((tm, k), BF16)],
        compiler_params=_params("parallel", "arbitrary"),
        name="glu_proj",
    )(y, w, uz, bias.reshape(1, n))


def _gla_body(q_ref, k_ref, v_ref, gf_ref, wf_ref, bf_ref, s0_ref, gg_ref, on_ref, a_ref, s_ref,
              qd_s, u_s, dec_s, st_s, o_s, *, chunk):
    c = chunk
    t = q_ref.shape[0]
    n = t // c
    grp = GLA_GROUP_ROWS
    cpg = grp // c
    shift = c.bit_length() - 1
    assert 1 << shift == c and t % grp == 0
    scale = GLA_DK ** -0.5
    r = lax.broadcasted_iota(I32, (grp, grp), 0)
    cc = lax.broadcasted_iota(I32, (grp, grp), 1)
    same = jnp.right_shift(r, shift) == jnp.right_shift(cc, shift)
    causal = jnp.logical_and(same, cc <= r)
    causal_b = causal.astype(BF16)
    wf = wf_ref[...].astype(BF16)
    bias = bf_ref[...]

    def group(i, carry):
        rows = pl.ds(pl.multiple_of(i * grp, grp), grp)
        gf = gf_ref[rows, :][:, :GLA_GATE_RANK]
        pre = jnp.dot(gf.astype(BF16), wf, preferred_element_type=F32) + bias
        g = _log_sigmoid(pre) / GLA_GATE_TAU
        g_hi = g.astype(BF16)
        g_md = (g - g_hi.astype(F32)).astype(BF16)
        g_lo = (g - g_hi.astype(F32) - g_md.astype(F32)).astype(BF16)
        b3 = jnp.dot(causal_b, jnp.concatenate([g_hi, g_md, g_lo], axis=1), preferred_element_type=F32)
        b = b3[:, :GLA_DK] + b3[:, GLA_DK:2 * GLA_DK] + b3[:, 2 * GLA_DK:]
        row_of = lambda r0: jnp.concatenate(
            [jnp.broadcast_to(b[j * c + r0:j * c + r0 + 1, :], (c, GLA_DK)) for j in range(cpg)], axis=0)
        bm = row_of(c // 2 - 1)
        be = row_of(c - 1)
        q = q_ref[rows, :] * scale
        k = k_ref[rows, :]
        v = v_ref[rows, :].astype(BF16)
        qe = (q * jnp.exp(b - bm)).astype(BF16)
        ke = (k * jnp.exp(bm - b)).astype(BF16)
        att = jnp.where(causal, _nt(qe, ke), 0.0)
        o_s[rows, :] = jnp.dot(att.astype(BF16), v, preferred_element_type=F32)
        qd_s[rows, :] = (q * jnp.exp(b)).astype(BF16)
        kd = (k * jnp.exp(be - b)).astype(BF16)
        dec = jnp.exp(be)
        for j in range(cpg):
            cj = i * cpg + j
            u_s[cj] = _tn(v[j * c:(j + 1) * c, :], kd[j * c:(j + 1) * c, :])
            dec_s[cj] = dec[j * c:j * c + SUBLANES, :]
        return carry

    lax.fori_loop(0, t // grp, group, 0, unroll=True)

    def advance(cj, st):
        st_s[cj] = st.astype(BF16)
        return st * dec_s[cj][0:1, :] + u_s[cj]

    st = lax.fori_loop(0, n, advance, s0_ref[...].T, unroll=True)
    s_ref[...] = st.T

    on = on_ref[...]

    def inter(cj, carry):
        rows = pl.ds(pl.multiple_of(cj * c, c), c)
        o = o_s[rows, :] + _nt(qd_s[rows, :], st_s[cj])
        a_ref[rows, :] = (_rms_rows(o, on) * _silu(gg_ref[rows, :])).astype(a_ref.dtype)
        return carry

    lax.fori_loop(0, n, inter, 0, unroll=True)


def _gla_small_body(q_ref, k_ref, v_ref, gf_ref, wf_ref, bf_ref, s0_ref, o_ref, s_ref):
    c = q_ref.shape[0]
    scale = GLA_DK ** -0.5
    row = lax.broadcasted_iota(I32, (c, GLA_DK), 0)
    for h in range(GLA_HEADS):
        kc = slice(h * GLA_DK, (h + 1) * GLA_DK)
        vc = slice(h * GLA_DV, (h + 1) * GLA_DV)
        q = q_ref[:, kc] * scale
        k = k_ref[:, kc]
        v = v_ref[:, vc]
        pre = bf_ref[:, kc] + jnp.zeros((c, GLA_DK), F32)
        for r in range(GLA_GATE_RANK):
            pre = pre + gf_ref[:, r:r + 1] * wf_ref[r:r + 1, kc]
        g = _log_sigmoid(pre) / GLA_GATE_TAU
        b = jnp.zeros((c, GLA_DK), F32)
        for s in range(c):
            b = b + jnp.where(row >= s, g[s:s + 1, :], 0.0)
        be = b[c - 1:c, :]
        s0 = s0_ref[h]
        o = jnp.dot(q * jnp.exp(b), s0, preferred_element_type=F32)
        for s in range(c):
            e = jnp.exp(jnp.where(row >= s, b - b[s:s + 1, :], NEG_INF))
            a_col = jnp.sum(q * k[s:s + 1, :] * e, axis=-1, keepdims=True)
            o = o + a_col * v[s:s + 1, :]
        o_ref[:, vc] = o
        kd = k * jnp.exp(be - b)
        pad = jnp.concatenate([kd, jnp.exp(be), jnp.zeros((LANES - c - 1, GLA_DK), F32)], axis=0)
        padt = pad.T
        s_new = s0 * padt[:, c:c + 1]
        for s in range(c):
            s_new = s_new + padt[:, s:s + 1] * v[s:s + 1, :]
        s_ref[h] = s_new


def _gla_small(p, gf, w_f2, b_f, s0, *, nb, t):
    assert t <= SUBLANES
    state_spec = pl.BlockSpec((None, GLA_HEADS, GLA_DK, GLA_DV), lambda b: (b, 0, 0, 0))
    return pl.pallas_call(
        _gla_small_body,
        grid=(nb,),
        in_specs=[
            pl.BlockSpec((t, GLA_KW), lambda b: (b, COL_GQ // GLA_KW)),
            pl.BlockSpec((t, GLA_KW), lambda b: (b, COL_GK // GLA_KW)),
            pl.BlockSpec((t, GLA_VW), lambda b: (b, COL_GV // GLA_VW)),
            pl.BlockSpec((t, LANES), lambda b: (b, 0)),
            pl.BlockSpec((GLA_GATE_RANK, GLA_KW), lambda b: (0, 0)),
            pl.BlockSpec((1, GLA_KW), lambda b: (0, 0)),
            state_spec,
        ],
        out_specs=[pl.BlockSpec((t, GLA_VW), lambda b: (b, 0)), state_spec],
        out_shape=[jax.ShapeDtypeStruct((nb * t, GLA_VW), F32),
                   jax.ShapeDtypeStruct((nb, GLA_HEADS, GLA_DK, GLA_DV), F32)],
        compiler_params=_params("parallel"),
        name="gla_small",
    )(p, p, p, gf, w_f2, b_f.reshape(1, -1), s0)


def _gla(p, gf, w_f2, b_f, s0, o_norm, *, nb, t):
    assert t % GLA_CHUNK == 0
    kq, kk, kv, kg = COL_GQ // GLA_DK, COL_GK // GLA_DK, COL_GV // GLA_DV, COL_GG // GLA_DV
    state_spec = pl.BlockSpec((None, None, GLA_DK, GLA_DV), lambda b, h: (b, h, 0, 0))
    n = t // GLA_CHUNK
    return pl.pallas_call(
        functools.partial(_gla_body, chunk=GLA_CHUNK),
        grid=(nb, GLA_HEADS),
        in_specs=[
            pl.BlockSpec((t, GLA_DK), lambda b, h: (b, kq + h)),
            pl.BlockSpec((t, GLA_DK), lambda b, h: (b, kk + h)),
            pl.BlockSpec((t, GLA_DV), lambda b, h: (b, kv + h)),
            pl.BlockSpec((t, LANES), lambda b, h: (b, 0)),
            pl.BlockSpec((GLA_GATE_RANK, GLA_DK), lambda b, h: (0, h)),
            pl.BlockSpec((1, GLA_DK), lambda b, h: (0, h)),
            state_spec,
            pl.BlockSpec((t, GLA_DV), lambda b, h: (b, kg + h)),
            pl.BlockSpec((1, GLA_DV), lambda b, h: (0, 0)),
        ],
        out_specs=[pl.BlockSpec((t, GLA_DV), lambda b, h: (b, h)), state_spec],
        out_shape=[jax.ShapeDtypeStruct((nb * t, GLA_VW), BF16),
                   jax.ShapeDtypeStruct((nb, GLA_HEADS, GLA_DK, GLA_DV), F32)],
        scratch_shapes=[
            pltpu.VMEM((t, GLA_DK), BF16),
            pltpu.VMEM((n, GLA_DV, GLA_DK), F32),
            pltpu.VMEM((n, SUBLANES, GLA_DK), F32),
            pltpu.VMEM((n, GLA_DV, GLA_DK), BF16),
            pltpu.VMEM((t, GLA_DV), F32),
        ],
        compiler_params=_params("parallel", "parallel"),
        name="gla",
    )(p, p, p, gf, w_f2, b_f.reshape(1, -1), s0, p, o_norm.reshape(1, -1))


def _moba_prompt_body(q_ref, k_ref, v_ref, mg_ref, *rest, nblk, ncast):
    cast_in, o_ref, cast_out = rest[:ncast], rest[ncast], rest[ncast + 1:]
    for wi, wo in zip(cast_in, cast_out):
        wo[...] = wi[...].astype(BF16)

    blk = MOBA_BLOCK
    t = nblk * blk
    shift = blk.bit_length() - 1
    assert 1 << shift == blk and nblk <= LANES and nblk % MOBA_PAIR == 0
    scale = MOBA_DH ** -0.5
    q = q_ref[...]
    k = k_ref[...]
    qb = (q * scale).astype(BF16)
    kb = k.astype(BF16)
    vb = v_ref[...].astype(BF16)

    km = jnp.concatenate([jnp.mean(k[n * blk:(n + 1) * blk, :], axis=0, keepdims=True) for n in range(nblk)],
                         axis=0)
    gate = lax.dot_general(km, q, (((1,), (1,)), ((), ())),
                           precision=lax.Precision.HIGHEST, preferred_element_type=F32)
    nrow = lax.broadcasted_iota(I32, (nblk, t), 0)
    qblk = jnp.right_shift(lax.broadcasted_iota(I32, (nblk, t), 1), shift)
    valid = nrow < qblk
    gm = jnp.where(valid, gate, NEG_INF)
    rank = jnp.zeros((nblk, t), I32)
    for m in range(nblk):
        g_m = gm[m:m + 1, :]
        beats = jnp.logical_or(g_m > gm, jnp.logical_and(g_m == gm, m < nrow))
        rank = rank + beats.astype(I32)
    sel = jnp.logical_and(valid, rank < MOBA_TOPK).astype(F32)
    sel_c = jnp.concatenate([sel, jnp.zeros((LANES - nblk, t), F32)], axis=0).T
    bias_c = jnp.where(sel_c > 0.0, 0.0, NEG_INF)

    causal = lax.broadcasted_iota(I32, (blk, blk), 1) <= lax.broadcasted_iota(I32, (blk, blk), 0)
    masked = jnp.full((blk, blk), NEG_INF, F32)
    pair = MOBA_PAIR
    for v in range(nblk // pair):
        lo = pair * v * blk
        nbi = pair * (v + 1)
        s = _nt(qb[lo:lo + pair * blk, :], kb[0:nbi * blk, :])
        bias = bias_c[lo:lo + pair * blk, :]
        pieces = []
        for n in range(nbi):
            sn = s[:, n * blk:(n + 1) * blk]
            if n < pair * v:
                pieces.append(sn + bias[:, n:n + 1])
                continue
            j = n - pair * v
            parts = []
            for i in range(pair):
                sni = sn[i * blk:(i + 1) * blk, :]
                if i == j:
                    parts.append(jnp.where(causal, sni, NEG_INF))
                elif i > j:
                    parts.append(sni + bias[i * blk:(i + 1) * blk, n:n + 1])
                else:
                    parts.append(masked)
            pieces.append(jnp.concatenate(parts, axis=0))
        s = jnp.concatenate(pieces, axis=1)
        m = jnp.max(s, axis=-1, keepdims=True)
        p = jnp.exp(s - m)
        l = jnp.sum(p, axis=-1, keepdims=True)
        acc = jnp.dot(p.astype(BF16), vb[0:nbi * blk, :], preferred_element_type=F32)
        rows = slice(lo, lo + pair * blk)
        o_ref[rows, :] = (acc / l * _silu(mg_ref[rows, :])).astype(o_ref.dtype)


def _moba_prompt(p, to_cast, *, nb, t):
    nblk = t // MOBA_BLOCK
    kq, kk, kv, kg = COL_MQ // MOBA_DH, COL_MK // MOBA_DH, COL_MV // MOBA_DH, COL_MG // MOBA_DH
    steps = nb * MOBA_HEADS
    bf16_rows = 2 * SUBLANES
    cast_specs = []
    for w in to_cast:
        assert w.shape[0] % (steps * bf16_rows) == 0
        cast_specs.append(pl.BlockSpec((w.shape[0] // steps, w.shape[1]), lambda b, h: (b * MOBA_HEADS + h, 0)))
    outs = pl.pallas_call(
        functools.partial(_moba_prompt_body, nblk=nblk, ncast=len(to_cast)),
        grid=(nb, MOBA_HEADS),
        in_specs=[
            pl.BlockSpec((t, MOBA_DH), lambda b, h: (b, kq + h)),
            pl.BlockSpec((t, MOBA_DH), lambda b, h: (b, kk + h)),
            pl.BlockSpec((t, MOBA_DH), lambda b, h: (b, kv + h)),
            pl.BlockSpec((t, MOBA_DH), lambda b, h: (b, kg + h)),
        ] + cast_specs,
        out_specs=[pl.BlockSpec((t, MOBA_DH), lambda b, h: (b, h))] + cast_specs,
        out_shape=[jax.ShapeDtypeStruct((nb * t, MOBA_W), BF16)]
        + [jax.ShapeDtypeStruct(w.shape, BF16) for w in to_cast],
        compiler_params=_params("parallel", "parallel"),
        name="moba_prompt",
    )(p, p, p, p, *to_cast)
    return outs[0], list(outs[1:])


def _bmean_body(pt_ref, *refs, ppb):
    pages, o_ref = refs[:-1], refs[-1]
    for j in range(len(pages) // ppb):
        tot = jnp.sum(pages[j * ppb][...], axis=0)
        for r in pages[j * ppb + 1:(j + 1) * ppb]:
            tot = tot + jnp.sum(r[...], axis=0)
        o_ref[j] = tot * (1.0 / MOBA_BLOCK)


def _block_means(cache_k, page_table, layer):
    nb, n_pages = page_table.shape
    ppb = MOBA_BLOCK // PAGE_SIZE
    n_full = n_pages // ppb
    bps = BMEAN_BLOCKS_PER_STEP
    assert n_full % bps == 0
    pps = bps * ppb
    page_block = (None, None, PAGE_SIZE, MOBA_HEADS, MOBA_DH)
    page_spec = lambda j: pl.BlockSpec(page_block, lambda b, n, pt: (layer, pt[b, pps * n + j], 0, 0, 0))
    return pl.pallas_call(
        functools.partial(_bmean_body, ppb=ppb),
        grid_spec=pltpu.PrefetchScalarGridSpec(
            num_scalar_prefetch=1,
            grid=(nb, n_full // bps),
            in_specs=[page_spec(j) for j in range(pps)],
            out_specs=pl.BlockSpec((None, bps, MOBA_HEADS, MOBA_DH), lambda b, n, pt: (b, n, 0, 0)),
        ),
        out_shape=jax.ShapeDtypeStruct((nb, n_full, MOBA_HEADS, MOBA_DH), F32),
        compiler_params=_params("parallel", "arbitrary"),
        name="moba_block_means",
    )(page_table, *([cache_k] * pps))


def _gate_topk_body(q_ref, bm_ref, o_ref, *, td):
    n_full = bm_ref.shape[1]
    lane = lax.broadcasted_iota(I32, (n_full, LANES), 1)
    row = lax.broadcasted_iota(I32, (n_full, LANES), 0)
    orow = lax.broadcasted_iota(I32, (SUBLANES, LANES), 0)
    for h in range(MOBA_HEADS):
        km = bm_ref[h]
        g = jnp.full((n_full, LANES), NEG_INF, F32)
        for t in range(td):
            qv = q_ref[t:t + 1, h * MOBA_DH:(h + 1) * MOBA_DH]
            g = jnp.where(lane == t, jnp.sum(km * qv, axis=-1, keepdims=True), g)
        outv = jnp.zeros((SUBLANES, LANES), I32)
        for j in range(MOBA_TOPK):
            mx = jnp.max(g, axis=0, keepdims=True)
            idx = jnp.min(jnp.where(g == mx, row, n_full), axis=0, keepdims=True)
            g = jnp.where(row == idx, NEG_INF, g)
            outv = jnp.where(orow == j, idx, outv)
        o_ref[h] = outv


def _gate_topk(p, bmean_t, *, nb, td):
    n_full = bmean_t.shape[2]
    assert td <= SUBLANES and MOBA_TOPK <= SUBLANES
    kq = COL_MQ // MOBA_W
    return pl.pallas_call(
        functools.partial(_gate_topk_body, td=td),
        grid=(nb,),
        in_specs=[
            pl.BlockSpec((td, MOBA_W), lambda b: (b, kq)),
            pl.BlockSpec((None, MOBA_HEADS, n_full, MOBA_DH), lambda b: (b, 0, 0, 0)),
        ],
        out_specs=pl.BlockSpec((None, MOBA_HEADS, SUBLANES, LANES), lambda b: (b, 0, 0, 0)),
        out_shape=jax.ShapeDtypeStruct((nb, MOBA_HEADS, SUBLANES, LANES), I32),
        compiler_params=_params("parallel"),
        name="moba_gate_topk",
    )(p, bmean_t)


def _moba_sample_body(idx_ref, pt_ref, q_ref, k_ref, v_ref, ck_ref, cv_ref, o_ref, kbuf, vbuf, sem, *, layer, td):
    step = pl.program_id(0)
    nsteps = pl.num_programs(0)
    slot = lax.rem(step, 2)
    ppb = MOBA_BLOCK // PAGE_SIZE
    per_q = MOBA_TOPK * ppb
    nsel = MOBA_TOPK * MOBA_BLOCK
    scale = MOBA_DH ** -0.5
    slots = [(t, j, pg) for t in range(td) for j in range(MOBA_TOPK) for pg in range(ppb)]

    def copies(st, sl, t, j, pg):
        b = lax.div(st, MOBA_HEADS)
        h = lax.rem(st, MOBA_HEADS)
        blk = idx_ref[(st * td + t) * MOBA_TOPK + j]
        page = pt_ref[b, blk * ppb + pg]
        dst = pl.ds((t * per_q + j * ppb + pg) * PAGE_SIZE, PAGE_SIZE)
        return (pltpu.make_async_copy(ck_ref.at[layer, page, :, h, :], kbuf.at[sl, dst, :], sem.at[sl, 0]),
                pltpu.make_async_copy(cv_ref.at[layer, page, :, h, :], vbuf.at[sl, dst, :], sem.at[sl, 1]))

    def issue(st, sl):
        for c in slots:
            ck, cv = copies(st, sl, *c)
            ck.start()
            cv.start()

    @pl.when(step == 0)
    def _():
        issue(step, slot)

    @pl.when(step + 1 < nsteps)
    def _():
        issue(step + 1, 1 - slot)

    for c in slots:
        ck, cv = copies(step, slot, *c)
        ck.wait()
        cv.wait()

    rows = 2 * SUBLANES
    q = q_ref[...]
    qp = jnp.concatenate([q, jnp.zeros((rows - td, MOBA_DH), F32)], axis=0).astype(BF16)
    s_sel = _nt(qp, kbuf[slot].astype(BF16)) * scale
    r = lax.broadcasted_iota(I32, (rows, td * nsel), 0)
    c = lax.broadcasted_iota(I32, (rows, td * nsel), 1)
    mine = jnp.logical_and(c >= r * nsel, c < (r + 1) * nsel)
    s_sel = jnp.where(mine, s_sel, NEG_INF)

    k_new = k_ref[...]
    v_new = v_ref[...]
    rn = lax.broadcasted_iota(I32, (td, LANES), 0)
    cn = lax.broadcasted_iota(I32, (td, LANES), 1)
    s_new = jnp.full((td, LANES), NEG_INF, F32)
    for t in range(td):
        col = jnp.sum(q * k_new[t:t + 1, :], axis=-1, keepdims=True) * scale
        s_new = jnp.where(jnp.logical_and(cn == t, rn >= t), col, s_new)

    s_sel = s_sel[:td, :]
    m = jnp.maximum(jnp.max(s_sel, axis=-1, keepdims=True), jnp.max(s_new, axis=-1, keepdims=True))
    p_sel = jnp.exp(s_sel - m)
    p_new = jnp.exp(s_new - m)
    l = jnp.sum(p_sel, axis=-1, keepdims=True) + jnp.sum(p_new, axis=-1, keepdims=True)
    pp = jnp.concatenate([p_sel, jnp.zeros((rows - td, td * nsel), F32)], axis=0).astype(BF16)
    acc = jnp.dot(pp, vbuf[slot].astype(BF16), preferred_element_type=F32)[:td, :]
    for t in range(td):
        acc = acc + p_new[:, t:t + 1] * v_new[t:t + 1, :]
    o_ref[...] = acc / l


def _moba_sample(p, cache_k, cache_v, page_table, idx_flat, *, layer, nb, td):
    kq, kk, kv = COL_MQ // MOBA_DH, COL_MK // MOBA_DH, COL_MV // MOBA_DH
    nrows = td * MOBA_TOPK * MOBA_BLOCK
    row_spec = lambda off: pl.BlockSpec(
        (td, MOBA_DH), lambda s, idx, pt: (lax.div(s, MOBA_HEADS), off + lax.rem(s, MOBA_HEADS)))
    return pl.pallas_call(
        functools.partial(_moba_sample_body, layer=layer, td=td),
        grid_spec=pltpu.PrefetchScalarGridSpec(
            num_scalar_prefetch=2,
            grid=(nb * MOBA_HEADS,),
            in_specs=[
                row_spec(kq), row_spec(kk), row_spec(kv),
                pl.BlockSpec(memory_space=pl.ANY),
                pl.BlockSpec(memory_space=pl.ANY),
            ],
            out_specs=row_spec(0),
            scratch_shapes=[
                pltpu.VMEM((2, nrows, MOBA_DH), F32),
                pltpu.VMEM((2, nrows, MOBA_DH), F32),
                pltpu.SemaphoreType.DMA((2, 2)),
            ],
        ),
        out_shape=jax.ShapeDtypeStruct((nb * td, MOBA_W), F32),
        compiler_params=_params("arbitrary"),
        name="moba_sample",
    )(idx_flat, page_table, p, p, p, cache_k, cache_v)


def _merge_body(go_ref, gg_ref, mo_ref, mg_ref, on_ref, a_ref):
    on = on_ref[...]
    for h in range(GLA_HEADS):
        sl = slice(h * GLA_DV, (h + 1) * GLA_DV)
        g = _rms_rows(go_ref[:, sl], on) * _silu(gg_ref[:, sl])
        a_ref[:, sl] = g.astype(a_ref.dtype)
    a_ref[:, GLA_VW:] = (mo_ref[...] * _silu(mg_ref[...])).astype(a_ref.dtype)


def _merge(gla_o, moba_o, p, o_norm, *, tm):
    m = gla_o.shape[0]
    kg, km = COL_GG // GLA_VW, COL_MG // MOBA_W
    return pl.pallas_call(
        _merge_body,
        grid=(m // tm,),
        in_specs=[
            pl.BlockSpec((tm, GLA_VW), lambda i: (i, 0)),
            pl.BlockSpec((tm, GLA_VW), lambda i: (i, kg)),
            pl.BlockSpec((tm, MOBA_W), lambda i: (i, 0)),
            pl.BlockSpec((tm, MOBA_W), lambda i: (i, km)),
            pl.BlockSpec((1, GLA_DV), lambda i: (0, 0)),
        ],
        out_specs=pl.BlockSpec((tm, GLA_VW + MOBA_W), lambda i: (i, 0)),
        out_shape=jax.ShapeDtypeStruct((m, GLA_VW + MOBA_W), BF16),
        compiler_params=_params("parallel"),
        name="a_merge",
    )(gla_o, p, moba_o, p, o_norm.reshape(1, -1))


def _s5_coef_body(lr_ref, li_ref, ldt_ref, br_ref, bi_ref, bbr_ref, bbi_ref, abr_ref, abi_ref):
    lr = lr_ref[...]
    li = li_ref[...]
    dt = jnp.exp(ldt_ref[...])
    mag = jnp.exp(lr * dt)
    ab_re = mag * jnp.cos(li * dt)
    ab_im = mag * jnp.sin(li * dt)
    den = lr * lr + li * li
    nr = ab_re - 1.0
    cr = (nr * lr + ab_im * li) / den
    ci = (ab_im * lr - nr * li) / den
    br = br_ref[...]
    bi = bi_ref[...]
    bbr_ref[...] = cr * br - ci * bi
    bbi_ref[...] = cr * bi + ci * br
    abr_ref[...] = ab_re
    abi_ref[...] = ab_im


def _s5_coef(lam_re, lam_im, log_dt, b_re, b_im):
    g, p = lam_re.shape
    w = p * S5_GROUP
    rep = lambda a: jnp.repeat(a, S5_GROUP, axis=1)
    full = pl.BlockSpec((g, w), lambda: (0, 0))
    outs = pl.pallas_call(
        _s5_coef_body,
        in_specs=[full] * 5,
        out_specs=[full] * 4,
        out_shape=[jax.ShapeDtypeStruct((g, w), F32)] * 4,
        name="s5_coef",
    )(rep(lam_re), rep(lam_im), jnp.broadcast_to(log_dt[:, None], (g, w)),
      b_re.reshape(g, w), b_im.reshape(g, w))
    bbr, bbi, abr, abi = outs
    return (bbr.reshape(g, p, S5_GROUP), bbi.reshape(g, p, S5_GROUP),
            abr[:, ::S5_GROUP], abi[:, ::S5_GROUP])


def _s5_expand(w2):
    per_tile = LANES // S5_P
    rg = jnp.right_shift(lax.broadcasted_iota(I32, (S5_SLAB, LANES), 0), S5_GROUP.bit_length() - 1)
    lg = jnp.right_shift(lax.broadcasted_iota(I32, (S5_SLAB, LANES), 1), S5_P.bit_length() - 1)
    pieces = []
    for part in range(2):
        w = w2[:, part * LANES:(part + 1) * LANES]
        for kk in range(S5_SLAB_STATES // LANES):
            pieces.append(jnp.where(rg == per_tile * kk + lg, w, 0.0))
    return jnp.concatenate(pieces, axis=1)


def _s5_body(u_ref, bb_ref, cc_ref, ar_ref, ai_ref, d_ref, x0r_ref, x0i_ref,
             y_ref, xr_ref, xi_ref, lhs, bu, ybuf, cr, ci, bd_in, bd_out, *, ns, nt):
    ti = pl.program_id(1)
    tpv = SUBLANES // ns
    nst = S5_SLAB_STATES
    lo_rows = SUBLANES - ns
    nlb = S5_SLAB // LANES

    @pl.when(ti == 0)
    def _():
        cr[...] = jnp.zeros_like(cr)
        ci[...] = jnp.zeros_like(ci)
        cr[lo_rows:, :] = x0r_ref[...]
        ci[lo_rows:, :] = x0i_ref[...]
        e_in = _s5_expand(bb_ref[...])
        bd_in[:S5_SLAB, :] = e_in.astype(BF16)
        if tpv == 2:
            e_re, e_im = e_in[:, :nst], e_in[:, nst:]
            a_re, a_im = ar_ref[...], ai_ref[...]
            bd_in[S5_SLAB:, :] = jnp.concatenate([a_re * e_re - a_im * e_im, a_re * e_im + a_im * e_re],
                                                 axis=1).astype(BF16)
        bd_out[...] = _s5_expand(cc_ref[...]).astype(BF16)

    for b in range(ns):
        for j in range(nlb):
            lhs[j, pl.ds(b, nt, stride=ns), :] = u_ref[b, :, j * LANES:(j + 1) * LANES]
    u_rows = jnp.concatenate([lhs[j] for j in range(nlb)], axis=1)
    if tpv == 2:
        tbit = jnp.right_shift(lax.broadcasted_iota(I32, u_rows.shape, 0), ns.bit_length() - 1)
        u_prev = jnp.where(jnp.bitwise_and(tbit, 1) == 1, pltpu.roll(u_rows, ns, 0), 0.0)
        lhs_rows = jnp.concatenate([u_rows, u_prev], axis=1)
    else:
        lhs_rows = u_rows
    bu[...] = jnp.dot(lhs_rows.astype(BF16), bd_in[...], preferred_element_type=F32)

    ar = ar_ref[...]
    ai = ai_ref[...]
    if tpv == 1:
        def step(v, carry):
            xr, xi = carry
            sl = pl.ds(pl.multiple_of(v * SUBLANES, SUBLANES), SUBLANES)
            nxr = ar * xr - ai * xi + bu[sl, :nst]
            nxi = ar * xi + ai * xr + bu[sl, nst:]
            bu[sl, :nst] = nxr
            bu[sl, nst:] = nxi
            return nxr, nxi
    else:
        assert tpv == 2
        lo = lax.broadcasted_iota(I32, (SUBLANES, nst), 0) < ns
        c2r = jnp.where(lo, ar, ar * ar - ai * ai)
        c2i = jnp.where(lo, ai, 2.0 * ar * ai)

        def step(v, carry):
            xr, xi = carry
            sl = pl.ds(pl.multiple_of(v * SUBLANES, SUBLANES), SUBLANES)
            pr = jnp.where(lo, pltpu.roll(xr, ns, 0), xr)
            pi = jnp.where(lo, pltpu.roll(xi, ns, 0), xi)
            nxr = bu[sl, :nst] + c2r * pr - c2i * pi
            nxi = bu[sl, nst:] + c2r * pi + c2i * pr
            bu[sl, :nst] = nxr
            bu[sl, nst:] = nxi
            return nxr, nxi

    nv = (nt * ns) // SUBLANES
    xr, xi = lax.fori_loop(0, nv, step, (cr[...], ci[...]), unroll=True)
    cr[...] = xr
    ci[...] = xi

    yv = _nt(bu[...].astype(BF16), bd_out[...]) + d_ref[...] * u_rows
    yv = 0.5 * yv * (1.0 + lax.erf(yv * math.sqrt(0.5)))
    for j in range(nlb):
        ybuf[j] = yv[:, j * LANES:(j + 1) * LANES]
    for b in range(ns):
        for j in range(nlb):
            y_ref[b, :, j * LANES:(j + 1) * LANES] = ybuf[j, pl.ds(b, nt, stride=ns), :]

    @pl.when(ti == pl.num_programs(1) - 1)
    def _():
        xr_ref[...] = cr[lo_rows:, :]
        xi_ref[...] = ci[lo_rows:, :]


def _s5_core(uz, bb, cc, ab_re, ab_im, d, x0_re, x0_im, *, ns, t, nt):
    w = d.shape[0]
    nslab = w // S5_SLAB
    nst = S5_SLAB_STATES
    assert SUBLANES % ns == 0 and t % nt == 0 and (nt * ns) % SUBLANES == 0
    u3 = uz.reshape(ns, t, uz.shape[1])
    to_slab = lambda a: jnp.transpose(a.reshape(ns, nslab, nst), (1, 0, 2))
    st_spec = pl.BlockSpec((None, ns, nst), lambda s, i: (s, 0, 0))
    y, xr, xi = pl.pallas_call(
        functools.partial(_s5_body, ns=ns, nt=nt),
        grid=(nslab, t // nt),
        in_specs=[
            pl.BlockSpec((ns, nt, S5_SLAB), lambda s, i: (0, i, s)),
            pl.BlockSpec((None, S5_SLAB, 2 * LANES), lambda s, i: (s, 0, 0)),
            pl.BlockSpec((None, S5_SLAB, 2 * LANES), lambda s, i: (s, 0, 0)),
            pl.BlockSpec((None, 1, nst), lambda s, i: (s, 0, 0)),
            pl.BlockSpec((None, 1, nst), lambda s, i: (s, 0, 0)),
            pl.BlockSpec((1, S5_SLAB), lambda s, i: (0, s)),
            st_spec, st_spec,
        ],
        out_specs=[
            pl.BlockSpec((ns, nt, S5_SLAB), lambda s, i: (0, i, s)),
            st_spec, st_spec,
        ],
        out_shape=[
            jax.ShapeDtypeStruct((ns, t, w), F32),
            jax.ShapeDtypeStruct((nslab, ns, nst), F32),
            jax.ShapeDtypeStruct((nslab, ns, nst), F32),
        ],
        scratch_shapes=[
            pltpu.VMEM((S5_SLAB // LANES, nt * ns, LANES), F32),
            pltpu.VMEM((nt * ns, 2 * nst), F32),
            pltpu.VMEM((S5_SLAB // LANES, nt * ns, LANES), F32),
            pltpu.VMEM((SUBLANES, nst), F32),
            pltpu.VMEM((SUBLANES, nst), F32),
            pltpu.VMEM((S5_SLAB * (SUBLANES // ns), 2 * nst), BF16),
            pltpu.VMEM((S5_SLAB, 2 * nst), BF16),
        ],
        compiler_params=_params("parallel", "arbitrary"),
        name="s5_core",
    )(u3, bb, cc, ab_re.reshape(nslab, 1, nst), ab_im.reshape(nslab, 1, nst), d.reshape(1, w),
      to_slab(x0_re), to_slab(x0_im))
    g = w // S5_GROUP
    from_slab = lambda a: jnp.transpose(a, (1, 0, 2)).reshape(ns, g, S5_P)
    return y.reshape(ns * t, w), from_slab(xr), from_slab(xi)


def _s5_compact(re, im):
    reps = LANES // S5_P
    flat = lambda a: jnp.tile(a.reshape(-1, S5_SLAB, S5_P), (1, 1, reps))
    return jnp.concatenate([flat(re), flat(im)], axis=2)


def _rope_tables(pos, reps):
    half = MOBA_DH // 2
    inv_freq = ROPE_THETA ** (-jnp.arange(half, dtype=F32) / half)
    ang = pos.astype(F32)[:, None] * inv_freq[None, :]
    cos = jnp.cos(ang)
    sin = jnp.sin(ang)
    cos2 = jnp.concatenate([cos, cos], axis=-1)
    sin2 = jnp.concatenate([-sin, sin], axis=-1)
    return jnp.tile(cos2, (reps, 1)), jnp.tile(sin2, (reps, 1))


def _layer_a(x, pos, nb, t, s0, weights, w_out, to_cast, sample_ctx):
    norm, w_main, w_gf, w_f2, b_f, o_norm, q_norm, k_norm = weights
    cast = []
    m = nb * t
    tm = min(m, GEMM_TM)
    cos, sin = _rope_tables(pos, nb)
    p, gf = _aproj(x, norm, w_main, w_gf, q_norm, k_norm, cos, sin, tm=tm, tn=512)
    if sample_ctx is None:
        gla_a, state = _gla(p, gf, w_f2, b_f, s0, o_norm, nb=nb, t=t)
        moba_a, cast = _moba_prompt(p, to_cast, nb=nb, t=t)
        x_new = _gemm_res2(gla_a, moba_a, cast[0], x, tm=tm, tn=512)
    else:
        gla_o, state = _gla_small(p, gf, w_f2, b_f, s0, nb=nb, t=t)
        cache_k, cache_v, page_table, layer = sample_ctx
        bmean = _block_means(cache_k, page_table, layer)
        ids = _gate_topk(p, jnp.transpose(bmean, (0, 2, 1, 3)), nb=nb, td=t)
        idx_flat = jnp.transpose(ids[:, :, :MOBA_TOPK, :t], (0, 1, 3, 2)).reshape(-1)
        moba_o = _moba_sample(p, cache_k, cache_v, page_table, idx_flat, layer=layer, nb=nb, td=t)
        a = _merge(gla_o, moba_o, p, o_norm, tm=min(m, 256))
        x_new = _gemm_res(a, w_out, x, tm=tm, tn=512)
    mk = p[:, COL_MK:COL_MK + MOBA_W].reshape(nb, t, MOBA_HEADS, MOBA_DH)
    mv = p[:, COL_MV:COL_MV + MOBA_W].reshape(nb, t, MOBA_HEADS, MOBA_DH)
    return x_new, mk, mv, state, cast


def _layer_c(x, ns, t, x0_re, x0_im, weights):
    norm, w_in, bb, cc, ab_re, ab_im, d, w_glu, b_glu, w_out = weights
    m = ns * t
    tm = min(m, GEMM_TM)
    uz = _gemm_norm(x, norm, w_in, tm=tm, tn=512)
    y, xr, xi = _s5_core(uz, bb, cc, ab_re, ab_im, d, x0_re, x0_im, ns=ns, t=t, nt=min(t, S5_TIME_TILE))
    v = _gemm_glu(y, w_glu, uz, b_glu, tm=tm, tn=512)
    x_new = _gemm_res(v, w_out, x, tm=tm, tn=512)
    return x_new, xr, xi


def kernel(x_prompt, x_sample, cache_k, cache_v, state_gla, state_s5_re, state_s5_im, page_table, norm_a, w_in_a, w_gla_f2, b_gla_f, gla_out_norm, moba_q_norm, moba_k_norm, w_out_a, norm_c, w_in_c, s5_lambda_re, s5_lambda_im, s5_log_dt, s5_b_re, s5_b_im, s5_c_re, s5_c_im, s5_d, w_glu, b_glu, w_out_c):
    nbp, tp, d = x_prompt.shape
    nbs, ts, _ = x_sample.shape
    depth = norm_a.shape[0] + norm_c.shape[0]
    past_len = page_table.shape[1] * PAGE_SIZE
    assert past_len % MOBA_BLOCK == 0 and past_len // MOBA_BLOCK >= MOBA_TOPK
    assert tp % MOBA_BLOCK == 0 and tp % GLA_CHUNK == 0
    pos_p = jnp.arange(tp)
    pos_s = past_len + jnp.arange(ts)
    xp = x_prompt.reshape(nbp * tp, d)
    xs = x_sample.reshape(nbs * ts, d)
    g = d // S5_GROUP
    outs = [[] for _ in range(10)]
    for layer in range(depth):
        i = layer // 2
        if layer % 2 == 0:
            w_main, w_gf = _wprep(jnp.swapaxes(w_in_a[i], 0, 1))
            weights = (norm_a[i], w_main, w_gf, w_gla_f2[i], b_gla_f[i], gla_out_norm[i], moba_q_norm[i],
                       moba_k_norm[i])
            to_cast = [w_out_a[i]]
            if layer + 1 < depth:
                to_cast += [w_in_c[i], w_glu[i], w_out_c[i]]
            zero_state = jnp.zeros((nbp, GLA_HEADS, GLA_DK, GLA_DV), F32)
            xp, mk, mv, sp, cast = _layer_a(xp, pos_p, nbp, tp, zero_state, weights, None, to_cast, None)
            w_out_a_bf, c_weights_bf = cast[0], cast[1:]
            outs[0].append(mk)
            outs[1].append(mv)
            outs[4].append(sp)
            xs, mk, mv, ss, _ = _layer_a(xs, pos_s, nbs, ts, state_gla[i], weights, w_out_a_bf, None,
                                         (cache_k, cache_v, page_table, i))
            outs[2].append(mk)
            outs[3].append(mv)
            outs[5].append(ss)
        else:
            bb_re, bb_im, ab_re, ab_im = _s5_coef(s5_lambda_re[i], s5_lambda_im[i], s5_log_dt[i],
                                                  s5_b_re[i], s5_b_im[i])
            bb = _s5_compact(jnp.swapaxes(bb_re, 1, 2), jnp.swapaxes(bb_im, 1, 2))
            cc = _s5_compact(s5_c_re[i], -s5_c_im[i])
            w_in_bf, w_glu_bf, w_out_bf = c_weights_bf
            weights = (norm_c[i], w_in_bf, bb, cc, ab_re, ab_im, s5_d[i], w_glu_bf, b_glu[i], w_out_bf)
            zeros = jnp.zeros((nbp, g, S5_P), F32)
            xp, xr, xi = _layer_c(xp, nbp, tp, zeros, zeros, weights)
            outs[6].append(xr)
            outs[7].append(xi)
            xs, xr, xi = _layer_c(xs, nbs, ts, state_s5_re[i], state_s5_im[i], weights)
            outs[8].append(xr)
            outs[9].append(xi)
    pk, pv, sk, sv, pg, sg, psr, psi, ssr, ssi = [jnp.stack(o) for o in outs]
    return (xp.reshape(nbp, tp, d), xs.reshape(nbs, ts, d), pk, pv, sk, sv, pg, sg, psr, psi, ssr, ssi)
```

```python
import functools
import math

import jax
import jax.numpy as jnp
from jax import lax
from jax.experimental import pallas as pl
from jax.experimental.pallas import tpu as pltpu

F32 = jnp.float32
BF16 = jnp.bfloat16
I32 = jnp.int32

RMS_EPS = 1e-6
GLA_HEADS = 8
GLA_DK = 128
GLA_DV = 256
GLA_KW = GLA_HEADS * GLA_DK
GLA_VW = GLA_HEADS * GLA_DV
GLA_GATE_RANK = 16
GLA_GATE_TAU = 16.0
GLA_CHUNK = 64
GLA_GROUP_ROWS = 256
MOBA_HEADS = 16
MOBA_DH = 128
MOBA_W = MOBA_HEADS * MOBA_DH
MOBA_BLOCK = 256
MOBA_TOPK = 3
MOBA_PAIR = 1
ROPE_THETA = 10000.0
PAGE_SIZE = 128
BMEAN_BLOCKS_PER_STEP = 8
S5_GROUP = 16
S5_P = 64
S5_SLAB = 128
S5_SLAB_STATES = (S5_SLAB // S5_GROUP) * S5_P
S5_TIME_TILE = 1024

LANES = 128
SUBLANES = 8
VMEM_LIMIT = 52 * 1024 * 1024
NEG_INF = float("-inf")

COL_GQ = 0
COL_GK = COL_GQ + GLA_KW
COL_GV = COL_GK + GLA_KW
COL_GG = COL_GV + GLA_VW
COL_MQ = COL_GG + GLA_VW
COL_MK = COL_MQ + MOBA_W
COL_MV = COL_MK + MOBA_W
COL_MG = COL_MV + MOBA_W
A_COLS = COL_MG + MOBA_W
WPREP_ROWS = 256
GEMM_TM = 1024
EPILOGUE_ROWS = 1024
GEMM_SUB = 256
LHS_DOUBLE_BUFFER_MAX_BYTES = 8 * 1024 * 1024


def _params(*sem):
    return pltpu.CompilerParams(dimension_semantics=sem, vmem_limit_bytes=VMEM_LIMIT)


def _nt(a, b):
    return lax.dot_general(a, b, (((1,), (1,)), ((), ())), preferred_element_type=F32)


def _tn(a, b):
    return lax.dot_general(a, b, (((0,), (0,)), ((), ())), preferred_element_type=F32)


def _silu(x):
    return x / (1.0 + jnp.exp(-x))


def _log_sigmoid(x):
    return jnp.minimum(x, 0.0) - jnp.log(1.0 + jnp.exp(-jnp.abs(x)))


def _rms_rows(x, g):
    ms = jnp.mean(x * x, axis=-1, keepdims=True)
    return x * lax.rsqrt(ms + RMS_EPS) * g


def _lhs_spec(tm, k, itemsize):
    if tm * k * itemsize > LHS_DOUBLE_BUFFER_MAX_BYTES:
        return pl.BlockSpec((tm, k), lambda i, j: (i, 0), pipeline_mode=pl.Buffered(1))
    return pl.BlockSpec((tm, k), lambda i, j: (i, 0))


def _wprep_body(a_ref, b_ref, om_ref, of_ref, *, gate_blk):
    i = pl.program_id(0)
    gr = GLA_GATE_RANK

    @pl.when(i == 0)
    def _():
        of_ref[...] = jnp.zeros_like(of_ref)

    @pl.when(i < gate_blk)
    def _():
        om_ref[...] = a_ref[...].astype(BF16)

    @pl.when(i == gate_blk)
    def _():
        of_ref[:gr, :] = a_ref[:gr, :].astype(BF16)

    @pl.when(i >= gate_blk)
    def _():
        om_ref[:-gr, :] = a_ref[gr:, :].astype(BF16)
        om_ref[-gr:, :] = b_ref[...].astype(BF16)


def _wprep(wt):
    n, d = wt.shape
    tk = WPREP_ROWS
    gr = GLA_GATE_RANK
    gate_lo = COL_GG + GLA_VW
    assert n == A_COLS + gr and gate_lo % tk == 0 and A_COLS % tk == 0 and tk % gr == 0
    last_tail = (n - gr) // gr
    return pl.pallas_call(
        functools.partial(_wprep_body, gate_blk=gate_lo // tk),
        grid=(A_COLS // tk,),
        in_specs=[
            pl.BlockSpec((tk, d), lambda i: (i, 0)),
            pl.BlockSpec((gr, d), lambda i: (jnp.minimum((i + 1) * (tk // gr), last_tail), 0)),
        ],
        out_specs=[pl.BlockSpec((tk, d), lambda i: (i, 0)), pl.BlockSpec((LANES, d), lambda i: (0, 0))],
        out_shape=[jax.ShapeDtypeStruct((A_COLS, d), BF16), jax.ShapeDtypeStruct((LANES, d), BF16)],
        compiler_params=_params("arbitrary"),
        name="a_weight_prep",
    )(wt, wt)


def _aproj_body(x_ref, g_ref, w_ref, wf_ref, qn_ref, kn_ref, cos_ref, sin_ref, o_ref, gf_ref, h_ref, *, tn):
    j = pl.program_id(1)

    @pl.when(j == 0)
    def _():
        h_ref[...] = _rms_rows(x_ref[...], g_ref[...]).astype(BF16)
        gf_ref[...] = _nt(h_ref[...], wf_ref[...])

    o_ref[...] = _nt(h_ref[...], w_ref[...])

    q_lo, k_lo, k_hi = COL_MQ // tn, COL_MK // tn, COL_MV // tn

    @pl.when(jnp.logical_and(j >= q_lo, j < k_hi))
    def _():
        gain = jnp.where(j < k_lo, qn_ref[...], kn_ref[...])
        rb = min(o_ref.shape[0], EPILOGUE_ROWS)

        def rows_block(r, carry):
            rows = pl.ds(pl.multiple_of(r * rb, rb), rb)
            cos = cos_ref[rows, :]
            sin = sin_ref[rows, :]
            for hh in range(tn // MOBA_DH):
                cols = slice(hh * MOBA_DH, (hh + 1) * MOBA_DH)
                y = _rms_rows(o_ref[rows, cols], gain)
                o_ref[rows, cols] = y * cos + pltpu.roll(y, MOBA_DH // 2, 1) * sin
            return carry

        lax.fori_loop(0, o_ref.shape[0] // rb, rows_block, 0)


def _aproj(x, g, w_main, w_gf, qn, kn, cos, sin, *, tm, tn):
    m, d = x.shape
    n = w_main.shape[0]
    assert n == A_COLS and COL_MQ % tn == 0 and COL_MK % tn == 0 and COL_MV % tn == 0
    return pl.pallas_call(
        functools.partial(_aproj_body, tn=tn),
        grid=(m // tm, n // tn),
        in_specs=[
            _lhs_spec(tm, d, 4),
            pl.BlockSpec((1, d), lambda i, j: (0, 0)),
            pl.BlockSpec((tn, d), lambda i, j: (j, 0)),
            pl.BlockSpec((LANES, d), lambda i, j: (0, 0)),
            pl.BlockSpec((1, MOBA_DH), lambda i, j: (0, 0)),
            pl.BlockSpec((1, MOBA_DH), lambda i, j: (0, 0)),
            pl.BlockSpec((tm, MOBA_DH), lambda i, j: (i, 0)),
            pl.BlockSpec((tm, MOBA_DH), lambda i, j: (i, 0)),
        ],
        out_specs=[
            pl.BlockSpec((tm, tn), lambda i, j: (i, j)),
            pl.BlockSpec((tm, LANES), lambda i, j: (i, 0)),
        ],
        out_shape=[jax.ShapeDtypeStruct((m, n), F32), jax.ShapeDtypeStruct((m, LANES), F32)],
        scratch_shapes=[pltpu.VMEM((tm, d), BF16)],
        compiler_params=_params("parallel", "arbitrary"),
        name="a_proj",
    )(x, g.reshape(1, d), w_main, w_gf, qn.reshape(1, -1), kn.reshape(1, -1), cos, sin)


def _gemm_norm_body(x_ref, g_ref, w_ref, o_ref, h_ref):
    @pl.when(pl.program_id(1) == 0)
    def _():
        h_ref[...] = _rms_rows(x_ref[...], g_ref[...]).astype(BF16)

    o_ref[...] = jnp.dot(h_ref[...], w_ref[...], preferred_element_type=F32)


def _gemm_norm(x, g, w, *, tm, tn):
    m, d = x.shape
    n = w.shape[1]
    return pl.pallas_call(
        _gemm_norm_body,
        grid=(m // tm, n // tn),
        in_specs=[
            _lhs_spec(tm, d, 4),
            pl.BlockSpec((1, d), lambda i, j: (0, 0)),
            pl.BlockSpec((d, tn), lambda i, j: (0, j)),
        ],
        out_specs=pl.BlockSpec((tm, tn), lambda i, j: (i, j)),
        out_shape=jax.ShapeDtypeStruct((m, n), F32),
        scratch_shapes=[pltpu.VMEM((tm, d), BF16)],
        compiler_params=_params("parallel", "arbitrary"),
        name="c_proj",
    )(x, g.reshape(1, d), w)


def _gemm_res_body(a_ref, w_ref, r_ref, o_ref):
    o_ref[...] = r_ref[...] + jnp.dot(a_ref[...], w_ref[...], preferred_element_type=F32)


def _gemm_res2_body(a1_ref, a2_ref, w_ref, r_ref, o_ref):
    k1 = a1_ref.shape[1]
    acc = jnp.dot(a1_ref[...], w_ref[:k1, :], preferred_element_type=F32)
    o_ref[...] = r_ref[...] + acc + jnp.dot(a2_ref[...], w_ref[k1:, :], preferred_element_type=F32)


def _gemm_res2(a1, a2, w, res, *, tm, tn):
    m, k1 = a1.shape
    k2 = a2.shape[1]
    n = w.shape[1]
    return pl.pallas_call(
        _gemm_res2_body,
        grid=(m // tm, n // tn),
        in_specs=[
            pl.BlockSpec((tm, k1), lambda i, j: (i, 0)),
            pl.BlockSpec((tm, k2), lambda i, j: (i, 0)),
            pl.BlockSpec((k1 + k2, tn), lambda i, j: (0, j)),
            pl.BlockSpec((tm, tn), lambda i, j: (i, j)),
        ],
        out_specs=pl.BlockSpec((tm, tn), lambda i, j: (i, j)),
        out_shape=jax.ShapeDtypeStruct((m, n), F32),
        compiler_params=_params("parallel", "arbitrary"),
        name="out_proj",
    )(a1, a2, w, res)


def _gemm_res(a, w, res, *, tm, tn):
    m, k = a.shape
    n = w.shape[1]
    return pl.pallas_call(
        _gemm_res_body,
        grid=(m // tm, n // tn),
        in_specs=[
            pl.BlockSpec((tm, k), lambda i, j: (i, 0)),
            pl.BlockSpec((k, tn), lambda i, j: (0, j)),
            pl.BlockSpec((tm, tn), lambda i, j: (i, j)),
        ],
        out_specs=pl.BlockSpec((tm, tn), lambda i, j: (i, j)),
        out_shape=jax.ShapeDtypeStruct((m, n), F32),
        compiler_params=_params("parallel", "arbitrary"),
        name="out_proj",
    )(a, w, res)


def _gemm_glu_body(y_ref, w_ref, z_ref, b_ref, o_ref, h_ref, *, tn):
    j = pl.program_id(1)

    @pl.when(j == 0)
    def _():
        h_ref[...] = y_ref[...].astype(BF16)

    sub = min(tn, GEMM_SUB)
    for c in range(tn // sub):
        cols = slice(c * sub, (c + 1) * sub)
        t = jnp.dot(h_ref[...], w_ref[:, cols], preferred_element_type=F32) + b_ref[:, cols]
        y = y_ref[:, pl.ds(pl.multiple_of(j * tn + c * sub, sub), sub)]
        z = z_ref[:, cols]
        gate = (1.0 + jnp.tanh(0.5 * t)) * (1.0 + jnp.tanh(0.5 * z))
        o_ref[:, cols] = (0.25 * y * z * gate).astype(o_ref.dtype)


def _gemm_glu(y, w, uz, bias, *, tm, tn):
    m, k = y.shape
    n = w.shape[1]
    zoff = n // tn
    return pl.pallas_call(
        functools.partial(_gemm_glu_body, tn=tn),
        grid=(m // tm, n // tn),
        in_specs=[
            _lhs_spec(tm, k, 4),
            pl.BlockSpec((k, tn), lambda i, j: (0, j)),
            pl.BlockSpec((tm, tn), lambda i, j: (i, zoff + j)),
            pl.BlockSpec((1, tn), lambda i, j: (0, j)),
        ],
        out_specs=pl.BlockSpec((tm, tn), lambda i, j: (i, j)),
        out_shape=jax.ShapeDtypeStruct((m, n), BF16),
        scratch_shapes=[pltpu.VMEM((tm, k), BF16)],
        compiler_params=_params("parallel", "arbitrary"),
        name="glu_proj",
    )(y, w, uz, bias.reshape(1, n))


def _gla_body(q_ref, k_ref, v_ref, gf_ref, wf_ref, bf_ref, s0_ref, gg_ref, on_ref, a_ref, s_ref,
              qd_s, u_s, dec_s, st_s, o_s, *, chunk):
    c = chunk
    t = q_ref.shape[0]
    n = t // c
    grp = GLA_GROUP_ROWS
    cpg = grp // c
    shift = c.bit_length() - 1
    assert 1 << shift == c and t % grp == 0
    scale = GLA_DK ** -0.5
    r = lax.broadcasted_iota(I32, (grp, grp), 0)
    cc = lax.broadcasted_iota(I32, (grp, grp), 1)
    same = jnp.right_shift(r, shift) == jnp.right_shift(cc, shift)
    causal = jnp.logical_and(same, cc <= r)
    causal_b = causal.astype(BF16)
    wf = wf_ref[...].astype(BF16)
    bias = bf_ref[...]

    def group(i, carry):
        rows = pl.ds(pl.multiple_of(i * grp, grp), grp)
        gf = gf_ref[rows, :][:, :GLA_GATE_RANK]
        pre = jnp.dot(gf.astype(BF16), wf, preferred_element_type=F32) + bias
        g = _log_sigmoid(pre) / GLA_GATE_TAU
        g_hi = g.astype(BF16)
        g_md = (g - g_hi.astype(F32)).astype(BF16)
        g_lo = (g - g_hi.astype(F32) - g_md.astype(F32)).astype(BF16)
        b3 = jnp.dot(causal_b, jnp.concatenate([g_hi, g_md, g_lo], axis=1), preferred_element_type=F32)
        b = b3[:, :GLA_DK] + b3[:, GLA_DK:2 * GLA_DK] + b3[:, 2 * GLA_DK:]
        row_of = lambda r0: jnp.concatenate(
            [jnp.broadcast_to(b[j * c + r0:j * c + r0 + 1, :], (c, GLA_DK)) for j in range(cpg)], axis=0)
        bm = row_of(c // 2 - 1)
        be = row_of(c - 1)
        q = q_ref[rows, :] * scale
        k = k_ref[rows, :]
        v = v_ref[rows, :].astype(BF16)
        qe = (q * jnp.exp(b - bm)).astype(BF16)
        ke = (k * jnp.exp(bm - b)).astype(BF16)
        att = jnp.where(causal, _nt(qe, ke), 0.0)
        o_s[rows, :] = jnp.dot(att.astype(BF16), v, preferred_element_type=F32)
        qd_s[rows, :] = (q * jnp.exp(b)).astype(BF16)
        kd = (k * jnp.exp(be - b)).astype(BF16)
        dec = jnp.exp(be)
        for j in range(cpg):
            cj = i * cpg + j
            u_s[cj] = _tn(v[j * c:(j + 1) * c, :], kd[j * c:(j + 1) * c, :])
            dec_s[cj] = dec[j * c:j * c + SUBLANES, :]
        return carry

    lax.fori_loop(0, t // grp, group, 0, unroll=True)

    def advance(cj, st):
        st_s[cj] = st.astype(BF16)
        return st * dec_s[cj][0:1, :] + u_s[cj]

    st = lax.fori_loop(0, n, advance, s0_ref[...].T, unroll=True)
    s_ref[...] = st.T

    on = on_ref[...]

    def inter(cj, carry):
        rows = pl.ds(pl.multiple_of(cj * c, c), c)
        o = o_s[rows, :] + _nt(qd_s[rows, :], st_s[cj])
        a_ref[rows, :] = (_rms_rows(o, on) * _silu(gg_ref[rows, :])).astype(a_ref.dtype)
        return carry

    lax.fori_loop(0, n, inter, 0, unroll=True)


def _gla_small_body(q_ref, k_ref, v_ref, gf_ref, wf_ref, bf_ref, s0_ref, o_ref, s_ref):
    c = q_ref.shape[0]
    scale = GLA_DK ** -0.5
    row = lax.broadcasted_iota(I32, (c, GLA_DK), 0)
    for h in range(GLA_HEADS):
        kc = slice(h * GLA_DK, (h + 1) * GLA_DK)
        vc = slice(h * GLA_DV, (h + 1) * GLA_DV)
        q = q_ref[:, kc] * scale
        k = k_ref[:, kc]
        v = v_ref[:, vc]
        pre = bf_ref[:, kc] + jnp.zeros((c, GLA_DK), F32)
        for r in range(GLA_GATE_RANK):
            pre = pre + gf_ref[:, r:r + 1] * wf_ref[r:r + 1, kc]
        g = _log_sigmoid(pre) / GLA_GATE_TAU
        b = jnp.zeros((c, GLA_DK), F32)
        for s in range(c):
            b = b + jnp.where(row >= s, g[s:s + 1, :], 0.0)
        be = b[c - 1:c, :]
        s0 = s0_ref[h]
        o = jnp.dot(q * jnp.exp(b), s0, preferred_element_type=F32)
        for s in range(c):
            e = jnp.exp(jnp.where(row >= s, b - b[s:s + 1, :], NEG_INF))
            a_col = jnp.sum(q * k[s:s + 1, :] * e, axis=-1, keepdims=True)
            o = o + a_col * v[s:s + 1, :]
        o_ref[:, vc] = o
        kd = k * jnp.exp(be - b)
        pad = jnp.concatenate([kd, jnp.exp(be), jnp.zeros((LANES - c - 1, GLA_DK), F32)], axis=0)
        padt = pad.T
        s_new = s0 * padt[:, c:c + 1]
        for s in range(c):
            s_new = s_new + padt[:, s:s + 1] * v[s:s + 1, :]
        s_ref[h] = s_new


def _gla_small(p, gf, w_f2, b_f, s0, *, nb, t):
    assert t <= SUBLANES
    state_spec = pl.BlockSpec((None, GLA_HEADS, GLA_DK, GLA_DV), lambda b: (b, 0, 0, 0))
    return pl.pallas_call(
        _gla_small_body,
        grid=(nb,),
        in_specs=[
            pl.BlockSpec((t, GLA_KW), lambda b: (b, COL_GQ // GLA_KW)),
            pl.BlockSpec((t, GLA_KW), lambda b: (b, COL_GK // GLA_KW)),
            pl.BlockSpec((t, GLA_VW), lambda b: (b, COL_GV // GLA_VW)),
            pl.BlockSpec((t, LANES), lambda b: (b, 0)),
            pl.BlockSpec((GLA_GATE_RANK, GLA_KW), lambda b: (0, 0)),
            pl.BlockSpec((1, GLA_KW), lambda b: (0, 0)),
            state_spec,
        ],
        out_specs=[pl.BlockSpec((t, GLA_VW), lambda b: (b, 0)), state_spec],
        out_shape=[jax.ShapeDtypeStruct((nb * t, GLA_VW), F32),
                   jax.ShapeDtypeStruct((nb, GLA_HEADS, GLA_DK, GLA_DV), F32)],
        compiler_params=_params("parallel"),
        name="gla_small",
    )(p, p, p, gf, w_f2, b_f.reshape(1, -1), s0)


def _gla(p, gf, w_f2, b_f, s0, o_norm, *, nb, t):
    assert t % GLA_CHUNK == 0
    kq, kk, kv, kg = COL_GQ // GLA_DK, COL_GK // GLA_DK, COL_GV // GLA_DV, COL_GG // GLA_DV
    state_spec = pl.BlockSpec((None, None, GLA_DK, GLA_DV), lambda b, h: (b, h, 0, 0))
    n = t // GLA_CHUNK
    return pl.pallas_call(
        functools.partial(_gla_body, chunk=GLA_CHUNK),
        grid=(nb, GLA_HEADS),
        in_specs=[
            pl.BlockSpec((t, GLA_DK), lambda b, h: (b, kq + h)),
            pl.BlockSpec((t, GLA_DK), lambda b, h: (b, kk + h)),
            pl.BlockSpec((t, GLA_DV), lambda b, h: (b, kv + h)),
            pl.BlockSpec((t, LANES), lambda b, h: (b, 0)),
            pl.BlockSpec((GLA_GATE_RANK, GLA_DK), lambda b, h: (0, h)),
            pl.BlockSpec((1, GLA_DK), lambda b, h: (0, h)),
            state_spec,
            pl.BlockSpec((t, GLA_DV), lambda b, h: (b, kg + h)),
            pl.BlockSpec((1, GLA_DV), lambda b, h: (0, 0)),
        ],
        out_specs=[pl.BlockSpec((t, GLA_DV), lambda b, h: (b, h)), state_spec],
        out_shape=[jax.ShapeDtypeStruct((nb * t, GLA_VW), BF16),
                   jax.ShapeDtypeStruct((nb, GLA_HEADS, GLA_DK, GLA_DV), F32)],
        scratch_shapes=[
            pltpu.VMEM((t, GLA_DK), BF16),
            pltpu.VMEM((n, GLA_DV, GLA_DK), F32),
            pltpu.VMEM((n, SUBLANES, GLA_DK), F32),
            pltpu.VMEM((n, GLA_DV, GLA_DK), BF16),
            pltpu.VMEM((t, GLA_DV), F32),
        ],
        compiler_params=_params("parallel", "parallel"),
        name="gla",
    )(p, p, p, gf, w_f2, b_f.reshape(1, -1), s0, p, o_norm.reshape(1, -1))


def _moba_prompt_body(q_ref, k_ref, v_ref, mg_ref, *rest, nblk, ncast):
    cast_in, o_ref, cast_out = rest[:ncast], rest[ncast], rest[ncast + 1:]
    for wi, wo in zip(cast_in, cast_out):
        wo[...] = wi[...].astype(BF16)

    blk = MOBA_BLOCK
    t = nblk * blk
    shift = blk.bit_length() - 1
    assert 1 << shift == blk and nblk <= LANES and nblk % MOBA_PAIR == 0
    scale = MOBA_DH ** -0.5
    q = q_ref[...]
    k = k_ref[...]
    qb = (q * scale).astype(BF16)
    kb = k.astype(BF16)
    vb = v_ref[...].astype(BF16)

    km = jnp.concatenate([jnp.mean(k[n * blk:(n + 1) * blk, :], axis=0, keepdims=True) for n in range(nblk)],
                         axis=0)
    gate = lax.dot_general(km, q, (((1,), (1,)), ((), ())),
                           precision=lax.Precision.HIGHEST, preferred_element_type=F32)
    nrow = lax.broadcasted_iota(I32, (nblk, t), 0)
    qblk = jnp.right_shift(lax.broadcasted_iota(I32, (nblk, t), 1), shift)
    valid = nrow < qblk
    gm = jnp.where(valid, gate, NEG_INF)
    rank = jnp.zeros((nblk, t), I32)
    for m in range(nblk):
        g_m = gm[m:m + 1, :]
        beats = jnp.logical_or(g_m > gm, jnp.logical_and(g_m == gm, m < nrow))
        rank = rank + beats.astype(I32)
    sel = jnp.logical_and(valid, rank < MOBA_TOPK).astype(F32)
    sel_c = jnp.concatenate([sel, jnp.zeros((LANES - nblk, t), F32)], axis=0).T
    bias_c = jnp.where(sel_c > 0.0, 0.0, NEG_INF)

    causal = lax.broadcasted_iota(I32, (blk, blk), 1) <= lax.broadcasted_iota(I32, (blk, blk), 0)
    masked = jnp.full((blk, blk), NEG_INF, F32)
    pair = MOBA_PAIR
    for v in range(nblk // pair):
        lo = pair * v * blk
        nbi = pair * (v + 1)
        s = _nt(qb[lo:lo + pair * blk, :], kb[0:nbi * blk, :])
        bias = bias_c[lo:lo + pair * blk, :]
        pieces = []
        for n in range(nbi):
            sn = s[:, n * blk:(n + 1) * blk]
            if n < pair * v:
                pieces.append(sn + bias[:, n:n + 1])
                continue
            j = n - pair * v
            parts = []
            for i in range(pair):
                sni = sn[i * blk:(i + 1) * blk, :]
                if i == j:
                    parts.append(jnp.where(causal, sni, NEG_INF))
                elif i > j:
                    parts.append(sni + bias[i * blk:(i + 1) * blk, n:n + 1])
                else:
                    parts.append(masked)
            pieces.append(jnp.concatenate(parts, axis=0))
        s = jnp.concatenate(pieces, axis=1)
        m = jnp.max(s, axis=-1, keepdims=True)
        p = jnp.exp(s - m)
        l = jnp.sum(p, axis=-1, keepdims=True)
        acc = jnp.dot(p.astype(BF16), vb[0:nbi * blk, :], preferred_element_type=F32)
        rows = slice(lo, lo + pair * blk)
        o_ref[rows, :] = (acc / l * _silu(mg_ref[rows, :])).astype(o_ref.dtype)


def _moba_prompt(p, to_cast, *, nb, t):
    nblk = t // MOBA_BLOCK
    kq, kk, kv, kg = COL_MQ // MOBA_DH, COL_MK // MOBA_DH, COL_MV // MOBA_DH, COL_MG // MOBA_DH
    steps = nb * MOBA_HEADS
    bf16_rows = 2 * SUBLANES
    cast_specs = []
    for w in to_cast:
        assert w.shape[0] % (steps * bf16_rows) == 0
        cast_specs.append(pl.BlockSpec((w.shape[0] // steps, w.shape[1]), lambda b, h: (b * MOBA_HEADS + h, 0)))
    outs = pl.pallas_call(
        functools.partial(_moba_prompt_body, nblk=nblk, ncast=len(to_cast)),
        grid=(nb, MOBA_HEADS),
        in_specs=[
            pl.BlockSpec((t, MOBA_DH), lambda b, h: (b, kq + h)),
            pl.BlockSpec((t, MOBA_DH), lambda b, h: (b, kk + h)),
            pl.BlockSpec((t, MOBA_DH), lambda b, h: (b, kv + h)),
            pl.BlockSpec((t, MOBA_DH), lambda b, h: (b, kg + h)),
        ] + cast_specs,
        out_specs=[pl.BlockSpec((t, MOBA_DH), lambda b, h: (b, h))] + cast_specs,
        out_shape=[jax.ShapeDtypeStruct((nb * t, MOBA_W), BF16)]
        + [jax.ShapeDtypeStruct(w.shape, BF16) for w in to_cast],
        compiler_params=_params("parallel", "parallel"),
        name="moba_prompt",
    )(p, p, p, p, *to_cast)
    return outs[0], list(outs[1:])


def _bmean_body(pt_ref, *refs, ppb):
    pages, o_ref = refs[:-1], refs[-1]
    for j in range(len(pages) // ppb):
        tot = jnp.sum(pages[j * ppb][...], axis=0)
        for r in pages[j * ppb + 1:(j + 1) * ppb]:
            tot = tot + jnp.sum(r[...], axis=0)
        o_ref[j] = tot * (1.0 / MOBA_BLOCK)


def _block_means(cache_k, page_table, layer):
    nb, n_pages = page_table.shape
    ppb = MOBA_BLOCK // PAGE_SIZE
    n_full = n_pages // ppb
    bps = BMEAN_BLOCKS_PER_STEP
    assert n_full % bps == 0
    pps = bps * ppb
    page_block = (None, None, PAGE_SIZE, MOBA_HEADS, MOBA_DH)
    page_spec = lambda j: pl.BlockSpec(page_block, lambda b, n, pt: (layer, pt[b, pps * n + j], 0, 0, 0))
    return pl.pallas_call(
        functools.partial(_bmean_body, ppb=ppb),
        grid_spec=pltpu.PrefetchScalarGridSpec(
            num_scalar_prefetch=1,
            grid=(nb, n_full // bps),
            in_specs=[page_spec(j) for j in range(pps)],
            out_specs=pl.BlockSpec((None, bps, MOBA_HEADS, MOBA_DH), lambda b, n, pt: (b, n, 0, 0)),
        ),
        out_shape=jax.ShapeDtypeStruct((nb, n_full, MOBA_HEADS, MOBA_DH), F32),
        compiler_params=_params("parallel", "arbitrary"),
        name="moba_block_means",
    )(page_table, *([cache_k] * pps))


def _gate_topk_body(q_ref, bm_ref, o_ref, *, td):
    n_full = bm_ref.shape[1]
    lane = lax.broadcasted_iota(I32, (n_full, LANES), 1)
    row = lax.broadcasted_iota(I32, (n_full, LANES), 0)
    orow = lax.broadcasted_iota(I32, (SUBLANES, LANES), 0)
    for h in range(MOBA_HEADS):
        km = bm_ref[h]
        g = jnp.full((n_full, LANES), NEG_INF, F32)
        for t in range(td):
            qv = q_ref[t:t + 1, h * MOBA_DH:(h + 1) * MOBA_DH]
            g = jnp.where(lane == t, jnp.sum(km * qv, axis=-1, keepdims=True), g)
        outv = jnp.zeros((SUBLANES, LANES), I32)
        for j in range(MOBA_TOPK):
            mx = jnp.max(g, axis=0, keepdims=True)
            idx = jnp.min(jnp.where(g == mx, row, n_full), axis=0, keepdims=True)
            g = jnp.where(row == idx, NEG_INF, g)
            outv = jnp.where(orow == j, idx, outv)
        o_ref[h] = outv


def _gate_topk(p, bmean_t, *, nb, td):
    n_full = bmean_t.shape[2]
    assert td <= SUBLANES and MOBA_TOPK <= SUBLANES
    kq = COL_MQ // MOBA_W
    return pl.pallas_call(
        functools.partial(_gate_topk_body, td=td),
        grid=(nb,),
        in_specs=[
            pl.BlockSpec((td, MOBA_W), lambda b: (b, kq)),
            pl.BlockSpec((None, MOBA_HEADS, n_full, MOBA_DH), lambda b: (b, 0, 0, 0)),
        ],
        out_specs=pl.BlockSpec((None, MOBA_HEADS, SUBLANES, LANES), lambda b: (b, 0, 0, 0)),
        out_shape=jax.ShapeDtypeStruct((nb, MOBA_HEADS, SUBLANES, LANES), I32),
        compiler_params=_params("parallel"),
        name="moba_gate_topk",
    )(p, bmean_t)


def _moba_sample_body(idx_ref, pt_ref, q_ref, k_ref, v_ref, ck_ref, cv_ref, o_ref, kbuf, vbuf, sem, *, layer, td):
    step = pl.program_id(0)
    nsteps = pl.num_programs(0)
    slot = lax.rem(step, 2)
    ppb = MOBA_BLOCK // PAGE_SIZE
    per_q = MOBA_TOPK * ppb
    nsel = MOBA_TOPK * MOBA_BLOCK
    scale = MOBA_DH ** -0.5
    slots = [(t, j, pg) for t in range(td) for j in range(MOBA_TOPK) for pg in range(ppb)]

    def copies(st, sl, t, j, pg):
        b = lax.div(st, MOBA_HEADS)
        h = lax.rem(st, MOBA_HEADS)
        blk = idx_ref[(st * td + t) * MOBA_TOPK + j]
        page = pt_ref[b, blk * ppb + pg]
        dst = pl.ds((t * per_q + j * ppb + pg) * PAGE_SIZE, PAGE_SIZE)
        return (pltpu.make_async_copy(ck_ref.at[layer, page, :, h, :], kbuf.at[sl, dst, :], sem.at[sl, 0]),
                pltpu.make_async_copy(cv_ref.at[layer, page, :, h, :], vbuf.at[sl, dst, :], sem.at[sl, 1]))

    def issue(st, sl):
        for c in slots:
            ck, cv = copies(st, sl, *c)
            ck.start()
            cv.start()

    @pl.when(step == 0)
    def _():
        issue(step, slot)

    @pl.when(step + 1 < nsteps)
    def _():
        issue(step + 1, 1 - slot)

    for c in slots:
        ck, cv = copies(step, slot, *c)
        ck.wait()
        cv.wait()

    rows = 2 * SUBLANES
    q = q_ref[...]
    qp = jnp.concatenate([q, jnp.zeros((rows - td, MOBA_DH), F32)], axis=0).astype(BF16)
    s_sel = _nt(qp, kbuf[slot].astype(BF16)) * scale
    r = lax.broadcasted_iota(I32, (rows, td * nsel), 0)
    c = lax.broadcasted_iota(I32, (rows, td * nsel), 1)
    mine = jnp.logical_and(c >= r * nsel, c < (r + 1) * nsel)
    s_sel = jnp.where(mine, s_sel, NEG_INF)

    k_new = k_ref[...]
    v_new = v_ref[...]
    rn = lax.broadcasted_iota(I32, (td, LANES), 0)
    cn = lax.broadcasted_iota(I32, (td, LANES), 1)
    s_new = jnp.full((td, LANES), NEG_INF, F32)
    for t in range(td):
        col = jnp.sum(q * k_new[t:t + 1, :], axis=-1, keepdims=True) * scale
        s_new = jnp.where(jnp.logical_and(cn == t, rn >= t), col, s_new)

    s_sel = s_sel[:td, :]
    m = jnp.maximum(jnp.max(s_sel, axis=-1, keepdims=True), jnp.max(s_new, axis=-1, keepdims=True))
    p_sel = jnp.exp(s_sel - m)
    p_new = jnp.exp(s_new - m)
    l = jnp.sum(p_sel, axis=-1, keepdims=True) + jnp.sum(p_new, axis=-1, keepdims=True)
    pp = jnp.concatenate([p_sel, jnp.zeros((rows - td, td * nsel), F32)], axis=0).astype(BF16)
    acc = jnp.dot(pp, vbuf[slot].astype(BF16), preferred_element_type=F32)[:td, :]
    for t in range(td):
        acc = acc + p_new[:, t:t + 1] * v_new[t:t + 1, :]
    o_ref[...] = acc / l


def _moba_sample(p, cache_k, cache_v, page_table, idx_flat, *, layer, nb, td):
    kq, kk, kv = COL_MQ // MOBA_DH, COL_MK // MOBA_DH, COL_MV // MOBA_DH
    nrows = td * MOBA_TOPK * MOBA_BLOCK
    row_spec = lambda off: pl.BlockSpec(
        (td, MOBA_DH), lambda s, idx, pt: (lax.div(s, MOBA_HEADS), off + lax.rem(s, MOBA_HEADS)))
    return pl.pallas_call(
        functools.partial(_moba_sample_body, layer=layer, td=td),
        grid_spec=pltpu.PrefetchScalarGridSpec(
            num_scalar_prefetch=2,
            grid=(nb * MOBA_HEADS,),
            in_specs=[
                row_spec(kq), row_spec(kk), row_spec(kv),
                pl.BlockSpec(memory_space=pl.ANY),
                pl.BlockSpec(memory_space=pl.ANY),
            ],
            out_specs=row_spec(0),
            scratch_shapes=[
                pltpu.VMEM((2, nrows, MOBA_DH), F32),
                pltpu.VMEM((2, nrows, MOBA_DH), F32),
                pltpu.SemaphoreType.DMA((2, 2)),
            ],
        ),
        out_shape=jax.ShapeDtypeStruct((nb * td, MOBA_W), F32),
        compiler_params=_params("arbitrary"),
        name="moba_sample",
    )(idx_flat, page_table, p, p, p, cache_k, cache_v)


def _merge_body(go_ref, gg_ref, mo_ref, mg_ref, on_ref, a_ref):
    on = on_ref[...]
    for h in range(GLA_HEADS):
        sl = slice(h * GLA_DV, (h + 1) * GLA_DV)
        g = _rms_rows(go_ref[:, sl], on) * _silu(gg_ref[:, sl])
        a_ref[:, sl] = g.astype(a_ref.dtype)
    a_ref[:, GLA_VW:] = (mo_ref[...] * _silu(mg_ref[...])).astype(a_ref.dtype)


def _merge(gla_o, moba_o, p, o_norm, *, tm):
    m = gla_o.shape[0]
    kg, km = COL_GG // GLA_VW, COL_MG // MOBA_W
    return pl.pallas_call(
        _merge_body,
        grid=(m // tm,),
        in_specs=[
            pl.BlockSpec((tm, GLA_VW), lambda i: (i, 0)),
            pl.BlockSpec((tm, GLA_VW), lambda i: (i, kg)),
            pl.BlockSpec((tm, MOBA_W), lambda i: (i, 0)),
            pl.BlockSpec((tm, MOBA_W), lambda i: (i, km)),
            pl.BlockSpec((1, GLA_DV), lambda i: (0, 0)),
        ],
        out_specs=pl.BlockSpec((tm, GLA_VW + MOBA_W), lambda i: (i, 0)),
        out_shape=jax.ShapeDtypeStruct((m, GLA_VW + MOBA_W), BF16),
        compiler_params=_params("parallel"),
        name="a_merge",
    )(gla_o, p, moba_o, p, o_norm.reshape(1, -1))


def _s5_coef_body(lr_ref, li_ref, ldt_ref, br_ref, bi_ref, bbr_ref, bbi_ref, abr_ref, abi_ref):
    lr = lr_ref[...]
    li = li_ref[...]
    dt = jnp.exp(ldt_ref[...])
    mag = jnp.exp(lr * dt)
    ab_re = mag * jnp.cos(li * dt)
    ab_im = mag * jnp.sin(li * dt)
    den = lr * lr + li * li
    nr = ab_re - 1.0
    cr = (nr * lr + ab_im * li) / den
    ci = (ab_im * lr - nr * li) / den
    br = br_ref[...]
    bi = bi_ref[...]
    bbr_ref[...] = cr * br - ci * bi
    bbi_ref[...] = cr * bi + ci * br
    abr_ref[...] = ab_re
    abi_ref[...] = ab_im


def _s5_coef(lam_re, lam_im, log_dt, b_re, b_im):
    g, p = lam_re.shape
    w = p * S5_GROUP
    rep = lambda a: jnp.repeat(a, S5_GROUP, axis=1)
    full = pl.BlockSpec((g, w), lambda: (0, 0))
    outs = pl.pallas_call(
        _s5_coef_body,
        in_specs=[full] * 5,
        out_specs=[full] * 4,
        out_shape=[jax.ShapeDtypeStruct((g, w), F32)] * 4,
        name="s5_coef",
    )(rep(lam_re), rep(lam_im), jnp.broadcast_to(log_dt[:, None], (g, w)),
      b_re.reshape(g, w), b_im.reshape(g, w))
    bbr, bbi, abr, abi = outs
    return (bbr.reshape(g, p, S5_GROUP), bbi.reshape(g, p, S5_GROUP),
            abr[:, ::S5_GROUP], abi[:, ::S5_GROUP])


def _s5_expand(w2):
    per_tile = LANES // S5_P
    rg = jnp.right_shift(lax.broadcasted_iota(I32, (S5_SLAB, LANES), 0), S5_GROUP.bit_length() - 1)
    lg = jnp.right_shift(lax.broadcasted_iota(I32, (S5_SLAB, LANES), 1), S5_P.bit_length() - 1)
    pieces = []
    for part in range(2):
        w = w2[:, part * LANES:(part + 1) * LANES]
        for kk in range(S5_SLAB_STATES // LANES):
            pieces.append(jnp.where(rg == per_tile * kk + lg, w, 0.0))
    return jnp.concatenate(pieces, axis=1)


def _s5_body(u_ref, bb_ref, cc_ref, ar_ref, ai_ref, d_ref, x0r_ref, x0i_ref,
             y_ref, xr_ref, xi_ref, lhs, bu, ybuf, cr, ci, bd_in, bd_out, *, ns, nt):
    ti = pl.program_id(1)
    tpv = SUBLANES // ns
    nst = S5_SLAB_STATES
    lo_rows = SUBLANES - ns
    nlb = S5_SLAB // LANES

    @pl.when(ti == 0)
    def _():
        cr[...] = jnp.zeros_like(cr)
        ci[...] = jnp.zeros_like(ci)
        cr[lo_rows:, :] = x0r_ref[...]
        ci[lo_rows:, :] = x0i_ref[...]
        e_in = _s5_expand(bb_ref[...])
        bd_in[:S5_SLAB, :] = e_in.astype(BF16)
        if tpv == 2:
            e_re, e_im = e_in[:, :nst], e_in[:, nst:]
            a_re, a_im = ar_ref[...], ai_ref[...]
            bd_in[S5_SLAB:, :] = jnp.concatenate([a_re * e_re - a_im * e_im, a_re * e_im + a_im * e_re],
                                                 axis=1).astype(BF16)
        bd_out[...] = _s5_expand(cc_ref[...]).astype(BF16)

    for b in range(ns):
        for j in range(nlb):
            lhs[j, pl.ds(b, nt, stride=ns), :] = u_ref[b, :, j * LANES:(j + 1) * LANES]
    u_rows = jnp.concatenate([lhs[j] for j in range(nlb)], axis=1)
    if tpv == 2:
        tbit = jnp.right_shift(lax.broadcasted_iota(I32, u_rows.shape, 0), ns.bit_length() - 1)
        u_prev = jnp.where(jnp.bitwise_and(tbit, 1) == 1, pltpu.roll(u_rows, ns, 0), 0.0)
        lhs_rows = jnp.concatenate([u_rows, u_prev], axis=1)
    else:
        lhs_rows = u_rows
    bu[...] = jnp.dot(lhs_rows.astype(BF16), bd_in[...], preferred_element_type=F32)

    ar = ar_ref[...]
    ai = ai_ref[...]
    if tpv == 1:
        def step(v, carry):
            xr, xi = carry
            sl = pl.ds(pl.multiple_of(v * SUBLANES, SUBLANES), SUBLANES)
            nxr = ar * xr - ai * xi + bu[sl, :nst]
            nxi = ar * xi + ai * xr + bu[sl, nst:]
            bu[sl, :nst] = nxr
            bu[sl, nst:] = nxi
            return nxr, nxi
    else:
        assert tpv == 2
        lo = lax.broadcasted_iota(I32, (SUBLANES, nst), 0) < ns
        c2r = jnp.where(lo, ar, ar * ar - ai * ai)
        c2i = jnp.where(lo, ai, 2.0 * ar * ai)

        def step(v, carry):
            xr, xi = carry
            sl = pl.ds(pl.multiple_of(v * SUBLANES, SUBLANES), SUBLANES)
            pr = jnp.where(lo, pltpu.roll(xr, ns, 0), xr)
            pi = jnp.where(lo, pltpu.roll(xi, ns, 0), xi)
            nxr = bu[sl, :nst] + c2r * pr - c2i * pi
            nxi = bu[sl, nst:] + c2r * pi + c2i * pr
            bu[sl, :nst] = nxr
            bu[sl, nst:] = nxi
            return nxr, nxi

    nv = (nt * ns) // SUBLANES
    xr, xi = lax.fori_loop(0, nv, step, (cr[...], ci[...]), unroll=True)
    cr[...] = xr
    ci[...] = xi

    yv = _nt(bu[...].astype(BF16), bd_out[...]) + d_ref[...] * u_rows
    yv = 0.5 * yv * (1.0 + lax.erf(yv * math.sqrt(0.5)))
    for j in range(nlb):
        ybuf[j] = yv[:, j * LANES:(j + 1) * LANES]
    for b in range(ns):
        for j in range(nlb):
            y_ref[b, :, j * LANES:(j + 1) * LANES] = ybuf[j, pl.ds(b, nt, stride=ns), :]

    @pl.when(ti == pl.num_programs(1) - 1)
    def _():
        xr_ref[...] = cr[lo_rows:, :]
        xi_ref[...] = ci[lo_rows:, :]


def _s5_core(uz, bb, cc, ab_re, ab_im, d, x0_re, x0_im, *, ns, t, nt):
    w = d.shape[0]
    nslab = w // S5_SLAB
    nst = S5_SLAB_STATES
    assert SUBLANES % ns == 0 and t % nt == 0 and (nt * ns) % SUBLANES == 0
    u3 = uz.reshape(ns, t, uz.shape[1])
    to_slab = lambda a: jnp.transpose(a.reshape(ns, nslab, nst), (1, 0, 2))
    st_spec = pl.BlockSpec((None, ns, nst), lambda s, i: (s, 0, 0))
    y, xr, xi = pl.pallas_call(
        functools.partial(_s5_body, ns=ns, nt=nt),
        grid=(nslab, t // nt),
        in_specs=[
            pl.BlockSpec((ns, nt, S5_SLAB), lambda s, i: (0, i, s)),
            pl.BlockSpec((None, S5_SLAB, 2 * LANES), lambda s, i: (s, 0, 0)),
            pl.BlockSpec((None, S5_SLAB, 2 * LANES), lambda s, i: (s, 0, 0)),
            pl.BlockSpec((None, 1, nst), lambda s, i: (s, 0, 0)),
            pl.BlockSpec((None, 1, nst), lambda s, i: (s, 0, 0)),
            pl.BlockSpec((1, S5_SLAB), lambda s, i: (0, s)),
            st_spec, st_spec,
        ],
        out_specs=[
            pl.BlockSpec((ns, nt, S5_SLAB), lambda s, i: (0, i, s)),
            st_spec, st_spec,
        ],
        out_shape=[
            jax.ShapeDtypeStruct((ns, t, w), F32),
            jax.ShapeDtypeStruct((nslab, ns, nst), F32),
            jax.ShapeDtypeStruct((nslab, ns, nst), F32),
        ],
        scratch_shapes=[
            pltpu.VMEM((S5_SLAB // LANES, nt * ns, LANES), F32),
            pltpu.VMEM((nt * ns, 2 * nst), F32),
            pltpu.VMEM((S5_SLAB // LANES, nt * ns, LANES), F32),
            pltpu.VMEM((SUBLANES, nst), F32),
            pltpu.VMEM((SUBLANES, nst), F32),
            pltpu.VMEM((S5_SLAB * (SUBLANES // ns), 2 * nst), BF16),
            pltpu.VMEM((S5_SLAB, 2 * nst), BF16),
        ],
        compiler_params=_params("parallel", "arbitrary"),
        name="s5_core",
    )(u3, bb, cc, ab_re.reshape(nslab, 1, nst), ab_im.reshape(nslab, 1, nst), d.reshape(1, w),
      to_slab(x0_re), to_slab(x0_im))
    g = w // S5_GROUP
    from_slab = lambda a: jnp.transpose(a, (1, 0, 2)).reshape(ns, g, S5_P)
    return y.reshape(ns * t, w), from_slab(xr), from_slab(xi)


def _s5_compact(re, im):
    reps = LANES // S5_P
    flat = lambda a: jnp.tile(a.reshape(-1, S5_SLAB, S5_P), (1, 1, reps))
    return jnp.concatenate([flat(re), flat(im)], axis=2)


def _rope_tables(pos, reps):
    half = MOBA_DH // 2
    inv_freq = ROPE_THETA ** (-jnp.arange(half, dtype=F32) / half)
    ang = pos.astype(F32)[:, None] * inv_freq[None, :]
    cos = jnp.cos(ang)
    sin = jnp.sin(ang)
    cos2 = jnp.concatenate([cos, cos], axis=-1)
    sin2 = jnp.concatenate([-sin, sin], axis=-1)
    return jnp.tile(cos2, (reps, 1)), jnp.tile(sin2, (reps, 1))


def _layer_a(x, pos, nb, t, s0, weights, w_out, to_cast, sample_ctx):
    norm, w_main, w_gf, w_f2, b_f, o_norm, q_norm, k_norm = weights
    cast = []
    m = nb * t
    tm = min(m, GEMM_TM)
    cos, sin = _rope_tables(pos, nb)
    p, gf = _aproj(x, norm, w_main, w_gf, q_norm, k_norm, cos, sin, tm=tm, tn=512)
    if sample_ctx is None:
        gla_a, state = _gla(p, gf, w_f2, b_f, s0, o_norm, nb=nb, t=t)
        moba_a, cast = _moba_prompt(p, to_cast, nb=nb, t=t)
        x_new = _gemm_res2(gla_a, moba_a, cast[0], x, tm=tm, tn=512)
    else:
        gla_o, state = _gla_small(p, gf, w_f2, b_f, s0, nb=nb, t=t)
        cache_k, cache_v, page_table, layer = sample_ctx
        bmean = _block_means(cache_k, page_table, layer)
        ids = _gate_topk(p, jnp.transpose(bmean, (0, 2, 1, 3)), nb=nb, td=t)
        idx_flat = jnp.transpose(ids[:, :, :MOBA_TOPK, :t], (0, 1, 3, 2)).reshape(-1)
        moba_o = _moba_sample(p, cache_k, cache_v, page_table, idx_flat, layer=layer, nb=nb, td=t)
        a = _merge(gla_o, moba_o, p, o_norm, tm=min(m, 256))
        x_new = _gemm_res(a, w_out, x, tm=tm, tn=512)
    mk = p[:, COL_MK:COL_MK + MOBA_W].reshape(nb, t, MOBA_HEADS, MOBA_DH)
    mv = p[:, COL_MV:COL_MV + MOBA_W].reshape(nb, t, MOBA_HEADS, MOBA_DH)
    return x_new, mk, mv, state, cast


def _layer_c(x, ns, t, x0_re, x0_im, weights):
    norm, w_in, bb, cc, ab_re, ab_im, d, w_glu, b_glu, w_out = weights
    m = ns * t
    tm = min(m, GEMM_TM)
    uz = _gemm_norm(x, norm, w_in, tm=tm, tn=512)
    y, xr, xi = _s5_core(uz, bb, cc, ab_re, ab_im, d, x0_re, x0_im, ns=ns, t=t, nt=min(t, S5_TIME_TILE))
    v = _gemm_glu(y, w_glu, uz, b_glu, tm=tm, tn=512)
    x_new = _gemm_res(v, w_out, x, tm=tm, tn=512)
    return x_new, xr, xi


def kernel(x_prompt, x_sample, cache_k, cache_v, state_gla, state_s5_re, state_s5_im, page_table, norm_a, w_in_a, w_gla_f2, b_gla_f, gla_out_norm, moba_q_norm, moba_k_norm, w_out_a, norm_c, w_in_c, s5_lambda_re, s5_lambda_im, s5_log_dt, s5_b_re, s5_b_im, s5_c_re, s5_c_im, s5_d, w_glu, b_glu, w_out_c):
    nbp, tp, d = x_prompt.shape
    nbs, ts, _ = x_sample.shape
    depth = norm_a.shape[0] + norm_c.shape[0]
    past_len = page_table.shape[1] * PAGE_SIZE
    assert past_len % MOBA_BLOCK == 0 and past_len // MOBA_BLOCK >= MOBA_TOPK
    assert tp % MOBA_BLOCK == 0 and tp % GLA_CHUNK == 0
    pos_p = jnp.arange(tp)
    pos_s = past_len + jnp.arange(ts)
    xp = x_prompt.reshape(nbp * tp, d)
    xs = x_sample.reshape(nbs * ts, d)
    g = d // S5_GROUP
    outs = [[] for _ in range(10)]
    for layer in range(depth):
        i = layer // 2
        if layer % 2 == 0:
            w_main, w_gf = _wprep(jnp.swapaxes(w_in_a[i], 0, 1))
            weights = (norm_a[i], w_main, w_gf, w_gla_f2[i], b_gla_f[i], gla_out_norm[i], moba_q_norm[i],
                       moba_k_norm[i])
            to_cast = [w_out_a[i]]
            if layer + 1 < depth:
                to_cast += [w_in_c[i], w_glu[i], w_out_c[i]]
            zero_state = jnp.zeros((nbp, GLA_HEADS, GLA_DK, GLA_DV), F32)
            xp, mk, mv, sp, cast = _layer_a(xp, pos_p, nbp, tp, zero_state, weights, None, to_cast, None)
            w_out_a_bf, c_weights_bf = cast[0], cast[1:]
            outs[0].append(mk)
            outs[1].append(mv)
            outs[4].append(sp)
            xs, mk, mv, ss, _ = _layer_a(xs, pos_s, nbs, ts, state_gla[i], weights, w_out_a_bf, None,
                                         (cache_k, cache_v, page_table, i))
            outs[2].append(mk)
            outs[3].append(mv)
            outs[5].append(ss)
        else:
            bb_re, bb_im, ab_re, ab_im = _s5_coef(s5_lambda_re[i], s5_lambda_im[i], s5_log_dt[i],
                                                  s5_b_re[i], s5_b_im[i])
            bb = _s5_compact(jnp.swapaxes(bb_re, 1, 2), jnp.swapaxes(bb_im, 1, 2))
            cc = _s5_compact(s5_c_re[i], -s5_c_im[i])
            w_in_bf, w_glu_bf, w_out_bf = c_weights_bf
            weights = (norm_c[i], w_in_bf, bb, cc, ab_re, ab_im, s5_d[i], w_glu_bf, b_glu[i], w_out_bf)
            zeros = jnp.zeros((nbp, g, S5_P), F32)
            xp, xr, xi = _layer_c(xp, nbp, tp, zeros, zeros, weights)
            outs[6].append(xr)
            outs[7].append(xi)
            xs, xr, xi = _layer_c(xs, nbs, ts, state_s5_re[i], state_s5_im[i], weights)
            outs[8].append(xr)
            outs[9].append(xi)
    pk, pv, sk, sv, pg, sg, psr, psi, ssr, ssi = [jnp.stack(o) for o in outs]
    return (xp.reshape(nbp, tp, d), xs.reshape(nbs, ts, d), pk, pv, sk, sv, pg, sg, psr, psi, ssr, ssi)
```
